```python
import math
import jax, jax.numpy as jnp
from jax import lax
import numpy as np

D_MODEL = 1024
BATCH = 4
SEQ = 4096
DEPTH = 2
DEC_BATCH = 128
DEC_SEQ = 8
PAST_LEN = 2048
PAGE_SIZE = 128

H_A = 8
DH_A = 64
DV_A = 2 * DH_A
W_A = H_A * DV_A
G_B = 8
CHUNK_B = 128
W_B = D_MODEL
CG_B = W_B // G_B
H_C = 4
DK_C = D_MODEL // 2 // H_C
DV_C = D_MODEL // H_C
W_C = H_C * DV_C
GATE_RANK = 16
GLA_TAU = 16.0
GLA_CHUNK = 64
N_BUCKETS = 32
MAX_DISTANCE = 128
Q_BLOCK = 128
EPS = 1e-6
N_EVEN = (DEPTH + 1) // 2
N_ODD = DEPTH // 2
IN0 = 2 * H_A * 2 * DH_A + 2 * W_A + 3 * W_B
IN1 = 2 * H_C * DK_C + 2 * W_C + GATE_RANK

kernel_name = 'hybrid_diffattn_sgmlp_gla_step'


def rms_norm(x, g):
    xf = x.astype(jnp.float32)
    y = xf * lax.rsqrt(jnp.mean(xf * xf, axis=-1, keepdims=True) + EPS)
    return (y * g.astype(jnp.float32)).astype(x.dtype)


def layer_norm(x, g, b):
    xf = x.astype(jnp.float32)
    xc = xf - jnp.mean(xf, axis=-1, keepdims=True)
    y = xc * lax.rsqrt(jnp.mean(xc * xc, axis=-1, keepdims=True) + EPS)
    return (y * g.astype(jnp.float32) + b.astype(jnp.float32)).astype(x.dtype)


def t5_bucket(dist):
    n = jnp.maximum(dist, 0)
    max_exact = N_BUCKETS // 2
    nf = jnp.maximum(n, 1).astype(jnp.float32)
    large = max_exact + (jnp.log(nf / max_exact) / math.log(MAX_DISTANCE / max_exact)
                         * (N_BUCKETS - max_exact)).astype(jnp.int32)
    large = jnp.minimum(large, N_BUCKETS - 1)
    return jnp.where(n < max_exact, n, large)


def diff_attn_block(q, k, v, q_pos, k_pos, rel_bias, lam):
    s = jnp.einsum('bqhcd,bkhcd->bhcqk', q, k).astype(jnp.float32)
    bias = rel_bias[t5_bucket(q_pos[:, None] - k_pos[None, :])]
    s = s + jnp.transpose(bias, (2, 0, 1)).astype(jnp.float32)[None, :, None]
    causal = k_pos[None, :] <= q_pos[:, None]
    s = jnp.where(causal, s, -jnp.inf)
    p = jax.nn.softmax(s, axis=-1)
    a = p[:, :, 0] - lam * p[:, :, 1]
    return jnp.einsum('bhqk,bkhe->bqhe', a.astype(v.dtype), v)


def diff_attention(q, k, v, q_pos, k_pos, rel_bias, lam):
    b, lq = q.shape[0], q.shape[1]
    if lq <= Q_BLOCK:
        return diff_attn_block(q, k, v, q_pos, k_pos, rel_bias, lam)
    nb = -(-lq // Q_BLOCK)
    pad = nb * Q_BLOCK - lq
    qp = jnp.pad(q, ((0, 0), (0, pad), (0, 0), (0, 0), (0, 0)))
    pp = jnp.pad(q_pos, (0, pad), mode='edge')
    qb = jnp.moveaxis(qp.reshape(b, nb, Q_BLOCK, *q.shape[2:]), 1, 0)
    pb = pp.reshape(nb, Q_BLOCK)
    ob = lax.map(lambda blk: diff_attn_block(blk[0], k, v, blk[1], k_pos, rel_bias, lam), (qb, pb))
    o = jnp.moveaxis(ob, 0, 1).reshape(b, nb * Q_BLOCK, *ob.shape[3:])
    return o[:, :lq]


def chunk_spatial(v, sp_w, sp_b):
    b, l, g, cg = v.shape
    nc = -(-l // CHUNK_B)
    vp = jnp.pad(v, ((0, 0), (0, nc * CHUNK_B - l), (0, 0), (0, 0)))
    vc = vp.reshape(b, nc, CHUNK_B, g, cg)
    w = sp_w * jnp.tril(jnp.ones((CHUNK_B, CHUNK_B), sp_w.dtype))
    out = jnp.einsum('gts,bcsge->bctge', w, vc) + jnp.transpose(sp_b)[None, None, :, :, None]
    return out.reshape(b, nc * CHUNK_B, g, cg)[:, :l]


def gla_chunked(q, k, v, log_a, s0):
    b, l, h, dk = q.shape
    dv = v.shape[-1]
    c = min(GLA_CHUNK, l)
    nc = -(-l // c)
    pad = nc * c - l

    def prep(t):
        t = jnp.pad(t.astype(jnp.float32), ((0, 0), (0, pad), (0, 0), (0, 0)))
        return jnp.moveaxis(t.reshape(b, nc, c, h, t.shape[-1]), 1, 0)

    qc, kc, vc, ac = prep(q), prep(k), prep(v), prep(log_a)
    tri = jnp.tril(jnp.ones((c, c), bool))[None, :, :, None, None]

    def step(s, inp):
        qi, ki, vi, ai = inp
        bc = jnp.cumsum(ai, axis=1)
        o_inter = jnp.einsum('bthk,bhkv->bthv', qi * jnp.exp(bc), s)
        diff = bc[:, :, None] - bc[:, None, :]
        decay = jnp.exp(jnp.where(tri, diff, -jnp.inf))
        att = jnp.einsum('bthk,btshk,bshk->bhts', qi, decay, ki)
        o_intra = jnp.einsum('bhts,bshv->bthv', att, vi)
        b_last = bc[:, -1]
        s_new = jnp.exp(b_last)[..., None] * s + jnp.einsum(
            'bshk,bshv->bhkv', ki * jnp.exp(b_last[:, None] - bc), vi)
        return s_new, o_inter + o_intra

    s_fin, oc = lax.scan(step, s0.astype(jnp.float32), (qc, kc, vc, ac))
    o = jnp.moveaxis(oc, 0, 1).reshape(b, nc * c, h, dv)[:, :l]
    return o, s_fin


def even_layer(x, k_past, v_past, offset, rel_bias, lam_init, norm_g, w_in, q_g, k_g, lam_p,
               sub_g, ln_g, ln_b, sp_w, sp_b, w_out):
    b, l, _ = x.shape
    hq = H_A * 2 * DH_A
    z = rms_norm(x, norm_g) @ w_in
    cuts = np.cumsum([hq, hq, W_A, W_A, W_B, W_B]).tolist()
    zq, zk, zv, ga, zu, zvb, gb = jnp.split(z, cuts, axis=-1)
    q = rms_norm(zq.reshape(b, l, H_A, 2, DH_A), q_g) * (DH_A ** -0.5)
    k = rms_norm(zk.reshape(b, l, H_A, 2, DH_A), k_g)
    v = zv.reshape(b, l, H_A, DV_A)
    if k_past is None:
        k_all, v_all = k, v
    else:
        k_all = jnp.concatenate([k_past.astype(k.dtype), k], axis=1)
        v_all = jnp.concatenate([v_past.astype(v.dtype), v], axis=1)
    q_pos = offset + jnp.arange(l, dtype=jnp.int32)
    k_pos = jnp.arange(k_all.shape[1], dtype=jnp.int32)
    lf = lam_p.astype(jnp.float32)
    lam = jnp.exp(jnp.sum(lf[0] * lf[1])) - jnp.exp(jnp.sum(lf[2] * lf[3])) + lam_init
    o = diff_attention(q, k_all, v_all, q_pos, k_pos, rel_bias, lam)
    o = rms_norm(o, sub_g) * (1.0 - lam_init)
    o_a = o.reshape(b, l, W_A) * jax.nn.silu(ga)
    u = jax.nn.gelu(zu, approximate=False)
    vb = layer_norm(jax.nn.gelu(zvb, approximate=False), ln_g, ln_b)
    mixed = chunk_spatial(vb.reshape(b, l, G_B, CG_B), sp_w, sp_b).reshape(b, l, W_B)
    o_b = u * mixed * jax.nn.silu(gb)
    y = x + jnp.concatenate([o_a, o_b], axis=-1) @ w_out
    return y, k, v, vb


def odd_layer(x, s0, norm_g, w_in, w_gate, b_gate, o_g, w_out):
    b, l, _ = x.shape
    hk = H_C * DK_C
    z = rms_norm(x, norm_g) @ w_in
    zq, zk, zv, g, za = jnp.split(z, [hk, 2 * hk, 2 * hk + W_C, 2 * hk + 2 * W_C], axis=-1)
    q = zq.reshape(b, l, H_C, DK_C) * (DK_C ** -0.5)
    k = zk.reshape(b, l, H_C, DK_C)
    v = zv.reshape(b, l, H_C, DV_C)
    log_a = jax.nn.log_sigmoid((za @ w_gate + b_gate).astype(jnp.float32)) / GLA_TAU
    o, s = gla_chunked(q, k, v, log_a.reshape(b, l, H_C, DK_C), s0)
    o = rms_norm(o.astype(x.dtype), o_g).reshape(b, l, W_C) * jax.nn.silu(g)
    y = x + o @ w_out
    return y, s


def setup_inputs(seed: int = 0) -> dict:
    key = jax.random.key(seed)
    ks = jax.random.split(key, 32)
    f = jnp.float32
    n_pages = PAST_LEN // PAGE_SIZE
    used = DEC_BATCH * n_pages
    n_pool = used + max(1, used // 4)

    def nrm(k, shape, s):
        return jax.random.normal(k, shape, f) * s

    page_table = jax.random.permutation(ks[5], n_pool)[:used].reshape(DEC_BATCH, n_pages).astype(jnp.int32)
    return {
        'x_prompt': nrm(ks[0], (BATCH, SEQ, D_MODEL), 1.0),
        'x_sample': nrm(ks[1], (DEC_BATCH, DEC_SEQ, D_MODEL), 1.0),
        'cache_k': nrm(ks[2], (N_EVEN, n_pool, PAGE_SIZE, H_A, 2, DH_A), 1.0),
        'cache_v': nrm(ks[3], (N_EVEN, n_pool, PAGE_SIZE, H_A, DV_A), 1.0),
        'state_gla': nrm(ks[4], (N_ODD, DEC_BATCH, H_C, DK_C, DV_C), 1.0),
        'page_table': page_table,
        'rel_bias': nrm(ks[6], (N_BUCKETS, H_A), 0.2),
        'norm0_g': 1.0 + nrm(ks[7], (N_EVEN, D_MODEL), 0.05),
        'w_in0': nrm(ks[8], (N_EVEN, D_MODEL, IN0), D_MODEL ** -0.5),
        'q_norm_g': 1.0 + nrm(ks[9], (N_EVEN, 2, DH_A), 0.05),
        'k_norm_g': 1.0 + nrm(ks[10], (N_EVEN, 2, DH_A), 0.05),
        'lam': nrm(ks[11], (N_EVEN, 4, DH_A), 0.1),
        'subln_g': 1.0 + nrm(ks[12], (N_EVEN, DV_A), 0.05),
        'ln_v_g': 1.0 + nrm(ks[13], (N_EVEN, W_B), 0.05),
        'ln_v_b': nrm(ks[14], (N_EVEN, W_B), 0.02),
        'spatial_w': nrm(ks[15], (N_EVEN, G_B, CHUNK_B, CHUNK_B), CHUNK_B ** -0.5),
        'spatial_b': 1.0 + nrm(ks[16], (N_EVEN, G_B, CHUNK_B), 0.02),
        'w_out0': nrm(ks[17], (N_EVEN, W_A + W_B, D_MODEL), (W_A + W_B) ** -0.5),
        'norm1_g': 1.0 + nrm(ks[18], (N_ODD, D_MODEL), 0.05),
        'w_in1': nrm(ks[19], (N_ODD, D_MODEL, IN1), D_MODEL ** -0.5),
        'w_gate': nrm(ks[20], (N_ODD, GATE_RANK, H_C * DK_C), GATE_RANK ** -0.5),
        'b_gate': nrm(ks[21], (N_ODD, H_C * DK_C), 0.1),
        'gla_norm_g': 1.0 + nrm(ks[22], (N_ODD, DV_C), 0.05),
        'w_out1': nrm(ks[23], (N_ODD, W_C, D_MODEL), W_C ** -0.5),
    }


def reference(x_prompt, x_sample, cache_k, cache_v, state_gla, page_table, rel_bias, norm0_g, w_in0,
              q_norm_g, k_norm_g, lam, subln_g, ln_v_g, ln_v_b, spatial_w, spatial_b, w_out0,
              norm1_g, w_in1, w_gate, b_gate, gla_norm_g, w_out1):
    xp, xs = x_prompt, x_sample
    db = xs.shape[0]
    past_len = page_table.shape[1] * cache_k.shape[2]
    kp_l, vp_l, ks_l, vs_l, vb_l, sp_l, ss_l = [], [], [], [], [], [], []
    for layer in range(DEPTH):
        j = layer // 2
        if layer % 2 == 0:
            lam_init = 0.8 - 0.6 * math.exp(-0.3 * layer)
            wts = (norm0_g[j], w_in0[j], q_norm_g[j], k_norm_g[j], lam[j], subln_g[j],
                   ln_v_g[j], ln_v_b[j], spatial_w[j], spatial_b[j], w_out0[j])
            xp, k_new, v_new, _ = even_layer(xp, None, None, 0, rel_bias, lam_init, *wts)
            k_past = cache_k[j, page_table].reshape(db, past_len, H_A, 2, DH_A)
            v_past = cache_v[j, page_table].reshape(db, past_len, H_A, DV_A)
            xs, k_s, v_s, vb_s = even_layer(xs, k_past, v_past, past_len, rel_bias, lam_init, *wts)
            kp_l.append(k_new)
            vp_l.append(v_new)
            ks_l.append(k_s)
            vs_l.append(v_s)
            vb_l.append(vb_s)
        else:
            wts = (norm1_g[j], w_in1[j], w_gate[j], b_gate[j], gla_norm_g[j], w_out1[j])
            s0 = jnp.zeros((xp.shape[0], H_C, DK_C, DV_C), jnp.float32)
            xp, s_p = odd_layer(xp, s0, *wts)
            xs, s_s = odd_layer(xs, state_gla[j], *wts)
            sp_l.append(s_p.astype(state_gla.dtype))
            ss_l.append(s_s.astype(state_gla.dtype))
    k_prompt = jnp.stack(kp_l)
    v_prompt = jnp.stack(vp_l)
    k_sample = jnp.stack(ks_l)
    v_sample = jnp.stack(vs_l)
    vb_sample = jnp.stack(vb_l)
    gla_prompt = jnp.stack(sp_l)
    gla_sample = jnp.stack(ss_l)
    return (xp, xs, k_prompt, v_prompt, k_sample, v_sample, vb_sample, gla_prompt, gla_sample)
```

```python
import functools
import math

import numpy as np
import jax
import jax.numpy as jnp
from jax import lax
from jax.experimental import pallas as pl
from jax.experimental.pallas import tpu as pltpu

F32 = jnp.float32
BF16 = jnp.bfloat16

H_A = 8
DH_A = 64
DV_A = 2 * DH_A
G_B = 8
CHUNK_B = 128
H_C = 4
GLA_TAU = 16.0
N_BUCKETS = 32
MAX_DISTANCE = 128
EPS = 1e-6
MASK_VALUE = -1e30

LANES = 128
SUBLANES = 8
VMEM_LIMIT_BYTES = 56 * 1024 * 1024

TOKEN_TILE = 256
ATTN_TILE = 256
GLA_CHUNK_P = 128
GLA_GROUP_S = 8


def _cparams(n_axes):
    return pltpu.CompilerParams(
        dimension_semantics=("arbitrary",) * n_axes,
        vmem_limit_bytes=VMEM_LIMIT_BYTES,
    )


def _dot(a, b):
    return jnp.dot(a, b, preferred_element_type=F32)


def _dot_nt(a, b):
    return lax.dot_general(a, b, (((1,), (1,)), ((), ())), preferred_element_type=F32)


def _dot_tn(a, b):
    return lax.dot_general(a, b, (((0,), (0,)), ((), ())), preferred_element_type=F32)


def _silu(x):
    return x * jax.nn.sigmoid(x)


def _gelu(x):
    return 0.5 * x * (1.0 + lax.erf(x * np.float32(math.sqrt(0.5))))


def _rms_rows(x, g):
    return x * lax.rsqrt(jnp.mean(x * x, axis=-1, keepdims=True) + EPS) * g


def _split_bf16(x):
    hi = x.astype(BF16)
    lo = (x - hi.astype(F32)).astype(BF16)
    return hi, lo


def _proj0_body(x_ref, ng_ref, w_ref, red_ref, exp_ref, qg_ref, kg_ref, lng_ref, lnb_ref,
                q_ref, k_ref, v_ref, ga_ref, ub_ref, vb_ref, *, width):
    xb = _rms_rows(x_ref[...], ng_ref[...]).astype(BF16)

    def seg(i):
        return _dot(xb, w_ref[:, i * width:(i + 1) * width])

    def group_rms(z, g):
        ms = _dot((z * z).astype(BF16), red_ref[...])
        hi, lo = _split_bf16(lax.rsqrt(ms + EPS))
        scale = _dot(hi, exp_ref[...]) + _dot(lo, exp_ref[...])
        return z * scale * g

    q_ref[...] = group_rms(seg(0), qg_ref[...]).astype(q_ref.dtype)
    k_ref[...] = group_rms(seg(1), kg_ref[...]).astype(k_ref.dtype)
    v_ref[...] = seg(2).astype(v_ref.dtype)
    ga_ref[...] = _silu(seg(3)).astype(ga_ref.dtype)
    ub_ref[...] = (_gelu(seg(4)) * _silu(seg(6))).astype(ub_ref.dtype)
    hv = _gelu(seg(5))
    hc = hv - jnp.mean(hv, axis=-1, keepdims=True)
    vb = hc * lax.rsqrt(jnp.mean(hc * hc, axis=-1, keepdims=True) + EPS)
    vb_ref[...] = (vb * lng_ref[...] + lnb_ref[...]).astype(vb_ref.dtype)


def _proj0(x2d, ng, w_bf, red, expd, qg, kg, lng, lnb, *, vb_dtype):
    t, d = x2d.shape
    n_in = w_bf.shape[1]
    width = n_in // 7
    tm = min(TOKEN_TILE, t)
    row = lambda i: (i, 0)
    full = lambda i: (0, 0)
    outs = [
        jax.ShapeDtypeStruct((t, width), BF16),
        jax.ShapeDtypeStruct((t, width), F32),
        jax.ShapeDtypeStruct((t, width), F32),
        jax.ShapeDtypeStruct((t, width), BF16),
        jax.ShapeDtypeStruct((t, width), BF16),
        jax.ShapeDtypeStruct((t, width), vb_dtype),
    ]
    return pl.pallas_call(
        functools.partial(_proj0_body, width=width),
        grid=(t // tm,),
        in_specs=[
            pl.BlockSpec((tm, d), row),
            pl.BlockSpec((1, d), full),
            pl.BlockSpec((d, n_in), full),
            pl.BlockSpec(red.shape, full),
            pl.BlockSpec(expd.shape, full),
            pl.BlockSpec((1, width), full),
            pl.BlockSpec((1, width), full),
            pl.BlockSpec((1, width), full),
            pl.BlockSpec((1, width), full),
        ],
        out_specs=[pl.BlockSpec((tm, width), row)] * 6,
        out_shape=outs,
        compiler_params=_cparams(1),
        name="proj0",
    )(x2d, ng, w_bf, red, expd, qg, kg, lng, lnb)


def _lambda_value(lam_ref, lam_init):
    lf = lam_ref[...]
    s1 = jnp.sum(lf[0:1] * lf[1:2], axis=-1, keepdims=True)
    s2 = jnp.sum(lf[2:3] * lf[3:4], axis=-1, keepdims=True)
    return jnp.exp(s1) - jnp.exp(s2) + lam_init


def _attn_prompt_body(lam_ref, q_ref, k_ref, v_ref, bias_ref, subg_ref, o_ref, k_sc, vt_sc,
                      *, tile, n_tiles, lam_init):
    qi = pl.program_id(2)

    @pl.when(qi == 0)
    def _():
        for j in range(n_tiles):
            k_sc[j] = k_ref[j * tile:(j + 1) * tile, :].astype(BF16)
            vt_sc[j] = v_ref[j * tile:(j + 1) * tile, :].T.astype(BF16)

    lam = _lambda_value(lam_ref, lam_init)
    q = q_ref[...]
    lane = lax.broadcasted_iota(jnp.int32, q.shape, 1)
    zero = jnp.zeros_like(q)
    q2 = jnp.concatenate([jnp.where(lane < DH_A, q, zero), jnp.where(lane >= DH_A, q, zero)], axis=0)

    def step(j, carry, bias):
        m, l, acc = carry
        s = _dot_nt(k_sc[j], q2)
        if bias is not None:
            s = s + bias
        m_new = jnp.maximum(m, jnp.max(s, axis=0, keepdims=True))
        alpha = jnp.exp(m - m_new)
        p = jnp.exp(s - m_new)
        l = alpha * l + jnp.sum(p, axis=0, keepdims=True)
        acc = alpha * acc + _dot(vt_sc[j], p.astype(BF16))
        return m_new, l, acc

    carry = (jnp.full((1, 2 * tile), MASK_VALUE, F32), jnp.zeros((1, 2 * tile), F32),
             jnp.zeros((DV_A, 2 * tile), F32))
    carry = step(qi, carry, bias_ref[0])
    near_off = jnp.where(qi == 0, MASK_VALUE, 0.0).astype(F32)
    carry = step(jnp.maximum(qi - 1, 0), carry, bias_ref[1] + near_off)
    carry = lax.fori_loop(0, jnp.maximum(qi - 1, 0), lambda j, c: step(j, c, None), carry)

    _, l, acc = carry
    o_all = acc * (1.0 / l)
    o = o_all[:, :tile] - lam * o_all[:, tile:]
    on = o * lax.rsqrt(jnp.mean(o * o, axis=0, keepdims=True) + EPS) * subg_ref[...]
    o_ref[...] = on.T.astype(o_ref.dtype)


def _attn_prompt(q, k, v, lam_p, bias_tiles, subg_cols, *, lam_init):
    b, l, w = q.shape
    tile = ATTN_TILE
    n_tiles = l // tile
    hd = 2 * DH_A
    return pl.pallas_call(
        functools.partial(_attn_prompt_body, tile=tile, n_tiles=n_tiles, lam_init=lam_init),
        grid=(b, H_A, n_tiles),
        in_specs=[
            pl.BlockSpec(lam_p.shape, lambda bi, h, i: (0, 0)),
            pl.BlockSpec((None, tile, hd), lambda bi, h, i: (bi, i, h)),
            pl.BlockSpec((None, l, hd), lambda bi, h, i: (bi, 0, h)),
            pl.BlockSpec((None, l, DV_A), lambda bi, h, i: (bi, 0, h)),
            pl.BlockSpec((None, 2, tile, 2 * tile), lambda bi, h, i: (h, 0, 0, 0)),
            pl.BlockSpec((DV_A, tile), lambda bi, h, i: (0, 0)),
        ],
        out_specs=pl.BlockSpec((None, tile, DV_A), lambda bi, h, i: (bi, i, h)),
        out_shape=jax.ShapeDtypeStruct((b, l, w), BF16),
        scratch_shapes=[pltpu.VMEM((n_tiles, tile, hd), BF16), pltpu.VMEM((n_tiles, DV_A, tile), BF16)],
        compiler_params=_cparams(3),
        name="attn_prompt",
    )(lam_p, q, k, v, bias_tiles, subg_cols)


def _attn_sample_body(pt_ref, lam_ref, q_ref, kn_ref, vn_ref, mask_ref, bp_ref, bn_ref, subg_ref,
                      ck_hbm, cv_hbm, o_ref, kbuf, vbuf, sem, *, n_pages, page, key_chunk, lam_init):
    b = pl.program_id(0)
    nb = pl.num_programs(0)
    slot = b % 2

    def page_copies(seq, sl):
        cps = []
        for p in range(n_pages):
            pg = pt_ref[seq, p]
            cps.append(pltpu.make_async_copy(ck_hbm.at[pg], kbuf.at[sl, :, pl.ds(p * page, page)],
                                             sem.at[sl, 0]))
            cps.append(pltpu.make_async_copy(cv_hbm.at[pg], vbuf.at[sl, pl.ds(p * page * H_A, page * H_A), :],
                                             sem.at[sl, 1]))
        return cps

    @pl.when(b == 0)
    def _():
        for c in page_copies(0, 0):
            c.start()

    @pl.when(b + 1 < nb)
    def _():
        for c in page_copies(b + 1, 1 - slot):
            c.start()

    lam = _lambda_value(lam_ref, lam_init)
    q = q_ref[...]
    n_rows = mask_ref.shape[0]
    wq = jnp.concatenate([q] * (n_rows // q.shape[0]), axis=0) * mask_ref[...]

    for c in page_copies(b, slot):
        c.wait()

    s_len = q.shape[0]
    n_chunks = (n_pages * page) // key_chunk
    pad_rows = bn_ref.shape[1] - s_len
    kn = jnp.concatenate([kn_ref[...], jnp.zeros((pad_rows, kn_ref.shape[1]), F32)], axis=0).astype(BF16)
    vn = jnp.concatenate([vn_ref[...], jnp.zeros((pad_rows, vn_ref.shape[1]), F32)], axis=0).astype(BF16)
    s_parts = [_dot_nt(wq, kn) + bn_ref[...]]
    for c in range(n_chunks):
        keys = pl.ds(c * key_chunk, key_chunk)
        s_parts.append(_dot(wq, kbuf[slot, :, keys].astype(BF16)) + bp_ref[:, keys])

    m = jnp.max(s_parts[0], axis=1, keepdims=True)
    for s in s_parts[1:]:
        m = jnp.maximum(m, jnp.max(s, axis=1, keepdims=True))
    p_parts = [jnp.exp(s - m) for s in s_parts]
    l = jnp.sum(p_parts[0], axis=1, keepdims=True)
    for p in p_parts[1:]:
        l = l + jnp.sum(p, axis=1, keepdims=True)
    row = lax.broadcasted_iota(jnp.int32, l.shape, 0)
    half = H_A * s_len
    scale = jnp.where(row < half, 1.0, -lam) / l

    heads = []
    for h in range(H_A):
        r0 = h * s_len

        def head_rows(x):
            return jnp.concatenate([x[r0:r0 + s_len], x[half + r0:half + r0 + s_len]], axis=0)

        acc = _dot(head_rows(p_parts[0]).astype(BF16), vn[:, h * DV_A:(h + 1) * DV_A])
        for c in range(n_chunks):
            vh = vbuf[slot, pl.ds(c * key_chunk * H_A + h, key_chunk, stride=H_A), :].astype(BF16)
            acc = acc + _dot(head_rows(p_parts[c + 1]).astype(BF16), vh)
        acc = acc * head_rows(scale)
        o_h = acc[:s_len] + acc[s_len:]
        heads.append(o_h * lax.rsqrt(jnp.mean(o_h * o_h, axis=-1, keepdims=True) + EPS))
    o_ref[...] = (jnp.concatenate(heads, axis=1) * subg_ref[...]).astype(o_ref.dtype)


def _attn_sample(page_table, q, k_new, v_new, cache_k, cache_v, lam_p, wq_mask, bias_past, bias_new,
                 subg_rows, *, lam_init):
    db, s_len, w = q.shape
    n_pages = page_table.shape[1]
    page = cache_k.shape[2]
    past = n_pages * page
    key_chunk = min(512, past)
    seq3 = lambda b, pt: (b, 0, 0)
    full2 = lambda b, pt: (0, 0)
    grid_spec = pltpu.PrefetchScalarGridSpec(
        num_scalar_prefetch=1,
        grid=(db,),
        in_specs=[
            pl.BlockSpec(lam_p.shape, full2),
            pl.BlockSpec((None, s_len, w), seq3),
            pl.BlockSpec((None, s_len, w), seq3),
            pl.BlockSpec((None, s_len, w), seq3),
            pl.BlockSpec(wq_mask.shape, full2),
            pl.BlockSpec(bias_past.shape, full2),
            pl.BlockSpec(bias_new.shape, full2),
            pl.BlockSpec(subg_rows.shape, full2),
            pl.BlockSpec(memory_space=pl.ANY),
            pl.BlockSpec(memory_space=pl.ANY),
        ],
        out_specs=pl.BlockSpec((None, s_len, w), seq3),
        scratch_shapes=[
            pltpu.VMEM((2, w, past), F32),
            pltpu.VMEM((2, past * H_A, DV_A), F32),
            pltpu.SemaphoreType.DMA((2, 2)),
        ],
    )
    return pl.pallas_call(
        functools.partial(_attn_sample_body, n_pages=n_pages, page=page, key_chunk=key_chunk,
                          lam_init=lam_init),
        grid_spec=grid_spec,
        out_shape=jax.ShapeDtypeStruct((db, s_len, w), BF16),
        compiler_params=_cparams(1),
        name="attn_sample",
    )(page_table, lam_p, q, k_new, v_new, wq_mask, bias_past, bias_new, subg_rows, cache_k, cache_v)


def _mix0_body(x_ref, at_ref, ga_ref, ub_ref, vb_ref, wsp_ref, bsp_ref, wo_ref, y_ref):
    tm = x_ref.shape[0]
    wa = at_ref.shape[1]
    vb = vb_ref[...].astype(BF16)
    n_sub = tm // CHUNK_B
    cg = vb.shape[1] // G_B
    cols = []
    for g in range(G_B):
        lanes = slice(g * cg, (g + 1) * cg)
        rows = [_dot(wsp_ref[g], vb[sb * CHUNK_B:(sb + 1) * CHUNK_B, lanes]) for sb in range(n_sub)]
        cols.append(jnp.concatenate(rows, axis=0) if n_sub > 1 else rows[0])
    bsp = bsp_ref[...]
    bias = jnp.concatenate([bsp] * n_sub, axis=0) if n_sub > 1 else bsp
    mixed = jnp.concatenate(cols, axis=1) + bias
    ob = (ub_ref[...].astype(F32) * mixed).astype(BF16)
    oa = at_ref[...] * ga_ref[...]
    y_ref[...] = x_ref[...] + _dot(oa, wo_ref[:wa, :]) + _dot(ob, wo_ref[wa:, :])


def _mix0(x2d, attn, gas, ub, vb, wsp, bsp, wo_bf):
    t, d = x2d.shape
    tm = min(TOKEN_TILE, t)
    row = lambda i: (i, 0)
    full = lambda i: (0, 0)
    return pl.pallas_call(
        _mix0_body,
        grid=(t // tm,),
        in_specs=[
            pl.BlockSpec((tm, d), row),
            pl.BlockSpec((tm, attn.shape[1]), row),
            pl.BlockSpec((tm, gas.shape[1]), row),
            pl.BlockSpec((tm, ub.shape[1]), row),
            pl.BlockSpec((tm, vb.shape[1]), row),
            pl.BlockSpec(wsp.shape, lambda i: (0, 0, 0)),
            pl.BlockSpec(bsp.shape, full),
            pl.BlockSpec(wo_bf.shape, full),
        ],
        out_specs=pl.BlockSpec((tm, d), row),
        out_shape=jax.ShapeDtypeStruct((t, d), F32),
        compiler_params=_cparams(1),
        name="mix0",
    )(x2d, attn, gas, ub, vb, wsp, bsp, wo_bf)


def _proj1_body(x_ref, ng_ref, w_ref, wa_ref, wg_ref, bg_ref, q_ref, k_ref, v_ref, gs_ref, la_ref,
                *, hk, wc, q_scale):
    xb = _rms_rows(x_ref[...], ng_ref[...]).astype(BF16)
    q_ref[...] = _dot(xb, w_ref[:, 0:hk]) * q_scale
    k_ref[...] = _dot(xb, w_ref[:, hk:2 * hk])
    v_ref[...] = _dot(xb, w_ref[:, 2 * hk:2 * hk + wc]).astype(v_ref.dtype)
    gs_ref[...] = _silu(_dot(xb, w_ref[:, 2 * hk + wc:2 * hk + 2 * wc])).astype(gs_ref.dtype)
    za = _dot(xb, wa_ref[...]).astype(BF16)
    xg = _dot(za, wg_ref[...]) + bg_ref[...]
    la_ref[...] = (jnp.minimum(xg, 0.0) - jnp.log1p(jnp.exp(-jnp.abs(xg)))) * np.float32(1.0 / GLA_TAU)


def _proj1(x2d, ng, w_bf, wa_bf, wg_bf, bg, *, hk, wc, q_scale):
    t, d = x2d.shape
    tm = min(TOKEN_TILE, t)
    row = lambda i: (i, 0)
    full = lambda i: (0, 0)
    outs = [
        jax.ShapeDtypeStruct((t, hk), F32),
        jax.ShapeDtypeStruct((t, hk), F32),
        jax.ShapeDtypeStruct((t, wc), BF16),
        jax.ShapeDtypeStruct((t, wc), BF16),
        jax.ShapeDtypeStruct((t, hk), F32),
    ]
    return pl.pallas_call(
        functools.partial(_proj1_body, hk=hk, wc=wc, q_scale=q_scale),
        grid=(t // tm,),
        in_specs=[
            pl.BlockSpec((tm, d), row),
            pl.BlockSpec((1, d), full),
            pl.BlockSpec(w_bf.shape, full),
            pl.BlockSpec(wa_bf.shape, full),
            pl.BlockSpec(wg_bf.shape, full),
            pl.BlockSpec((1, hk), full),
        ],
        out_specs=[pl.BlockSpec((tm, hk), row), pl.BlockSpec((tm, hk), row), pl.BlockSpec((tm, wc), row),
                   pl.BlockSpec((tm, wc), row), pl.BlockSpec((tm, hk), row)],
        out_shape=outs,
        compiler_params=_cparams(1),
        name="proj1",
    )(x2d, ng, w_bf, wa_bf, wg_bf, bg)


def _segment_cumsum(x, seg):
    n = x.shape[0]
    row = lax.broadcasted_iota(jnp.int32, x.shape, 0) % seg
    d = 1
    while d < seg:
        x = x + jnp.where(row >= d, pltpu.roll(x, d, axis=0), 0.0)
        d *= 2
    return x


def _block_reference(bc, hb):
    n, w = bc.shape
    if hb >= SUBLANES:
        pieces = []
        for blk in range(n // (2 * hb)):
            r = blk * 2 * hb + hb - 1
            pieces.append(jnp.broadcast_to(bc[r:r + 1, :], (2 * hb, w)))
        return jnp.concatenate(pieces, axis=0) if len(pieces) > 1 else pieces[0]
    pos = lax.broadcasted_iota(jnp.int32, bc.shape, 0) % (2 * hb)
    out = bc
    for delta in range(-hb, hb):
        if delta == 0:
            continue
        shifted = pltpu.roll(bc, (-delta) % n, axis=0)
        out = jnp.where(pos == hb - 1 - delta, shifted, out)
    return out


def _gla_intra_scores(q, k, bc, levels):
    n = q.shape[0]
    ri = lax.broadcasted_iota(jnp.int32, (n, n), 0)
    ci = lax.broadcasted_iota(jnp.int32, (n, n), 1)
    rowpos = lax.broadcasted_iota(jnp.int32, q.shape, 0)
    att = jnp.where(ri == ci, _dot_nt(q.astype(BF16), k.astype(BF16)), 0.0)
    for hb in levels:
        ref = _block_reference(bc, hb)
        is_q = (rowpos % (2 * hb)) >= hb
        f = jnp.exp(jnp.where(is_q, bc - ref, ref - bc))
        qt = jnp.where(is_q, q * f, 0.0).astype(BF16)
        kt = jnp.where(is_q, 0.0, k * f).astype(BF16)
        same = (ri // (2 * hb)) == (ci // (2 * hb))
        att = att + jnp.where(same, _dot_nt(qt, kt), 0.0)
    return att


def _levels(seg):
    out, hb = [], seg // 2
    while hb >= 1:
        out.append(hb)
        hb //= 2
    return out


def _gla_prompt_body(q_ref, k_ref, v_ref, la_ref, og_ref, s0_ref, o_ref, sout_ref, st_sc, *, dk, dv):
    ci = pl.program_id(1)
    chunk = q_ref.shape[0]

    @pl.when(ci == 0)
    def _():
        for h in range(H_C):
            st_sc[h] = s0_ref[h].T

    for h in range(H_C):
        kl = slice(h * dk, (h + 1) * dk)
        vl = slice(h * dv, (h + 1) * dv)
        q, k, la = q_ref[:, kl], k_ref[:, kl], la_ref[:, kl]
        v = v_ref[:, vl]
        bc = _segment_cumsum(la, chunk)
        st = st_sc[h]
        att = _gla_intra_scores(q, k, bc, _levels(chunk))
        o = _dot_nt((q * jnp.exp(bc)).astype(BF16), st.astype(BF16)) + _dot(att.astype(BF16), v)
        b_last = bc[chunk - 1:chunk, :]
        k_dec = (k * jnp.exp(b_last - bc)).astype(BF16)
        st_sc[h] = st * jnp.exp(b_last) + _dot_tn(v, k_dec)
        o_ref[:, vl] = _rms_rows(o, og_ref[...]).astype(o_ref.dtype)

    @pl.when(ci == pl.num_programs(1) - 1)
    def _():
        for h in range(H_C):
            sout_ref[h] = st_sc[h].T


def _gla_prompt(q, k, v, la, og, s0):
    b, l, hk = q.shape
    wc = v.shape[2]
    dk, dv = hk // H_C, wc // H_C
    chunk = min(GLA_CHUNK_P, l)
    tok = lambda bi, ci: (bi, ci, 0)
    st = lambda bi, ci: (bi, 0, 0, 0)
    return pl.pallas_call(
        functools.partial(_gla_prompt_body, dk=dk, dv=dv),
        grid=(b, l // chunk),
        in_specs=[
            pl.BlockSpec((None, chunk, hk), tok),
            pl.BlockSpec((None, chunk, hk), tok),
            pl.BlockSpec((None, chunk, wc), tok),
            pl.BlockSpec((None, chunk, hk), tok),
            pl.BlockSpec((1, dv), lambda bi, ci: (0, 0)),
            pl.BlockSpec((None, H_C, dk, dv), st),
        ],
        out_specs=[pl.BlockSpec((None, chunk, wc), tok), pl.BlockSpec((None, H_C, dk, dv), st)],
        out_shape=[jax.ShapeDtypeStruct((b, l, wc), BF16), jax.ShapeDtypeStruct((b, H_C, dk, dv), F32)],
        scratch_shapes=[pltpu.VMEM((H_C, dv, dk), F32)],
        compiler_params=_cparams(2),
        name="gla_prompt",
    )(q, k, v, la, og, s0)


def _gla_sample_body(q_ref, k_ref, v_ref, la_ref, og_ref, s0_ref, o_ref, sout_ref, *, dk, dv, seg):
    n_seq = s0_ref.shape[0]
    for h in range(H_C):
        kl = slice(h * dk, (h + 1) * dk)
        vl = slice(h * dv, (h + 1) * dv)
        q, k, la = q_ref[:, kl], k_ref[:, kl], la_ref[:, kl]
        v = v_ref[:, vl]
        bc = _segment_cumsum(la, seg)
        att = _gla_intra_scores(q, k, bc, _levels(seg))
        o_intra = _dot(att.astype(BF16), v)
        q_dec = (q * jnp.exp(bc)).astype(BF16)
        outs = []
        for s in range(n_seq):
            rows = slice(s * seg, (s + 1) * seg)
            st = s0_ref[s, h].T
            bl = bc[s * seg + seg - 1:s * seg + seg, :]
            k_dec = (k[rows] * jnp.exp(bl - bc[rows])).astype(BF16)
            outs.append(_dot_nt(q_dec[rows], st.astype(BF16)))
            sout_ref[s, h] = (st * jnp.exp(bl) + _dot_tn(v[rows], k_dec)).T
        o = jnp.concatenate(outs, axis=0) + o_intra
        o_ref[:, vl] = _rms_rows(o, og_ref[...]).astype(o_ref.dtype)


def _gla_sample(q, k, v, la, og, s0, *, seg):
    t, hk = q.shape
    wc = v.shape[1]
    dk, dv = hk // H_C, wc // H_C
    db = t // seg
    grp = min(GLA_GROUP_S, db)
    rows = grp * seg
    tok = lambda i: (i, 0)
    st = lambda i: (i, 0, 0, 0)
    return pl.pallas_call(
        functools.partial(_gla_sample_body, dk=dk, dv=dv, seg=seg),
        grid=(db // grp,),
        in_specs=[
            pl.BlockSpec((rows, hk), tok),
            pl.BlockSpec((rows, hk), tok),
            pl.BlockSpec((rows, wc), tok),
            pl.BlockSpec((rows, hk), tok),
            pl.BlockSpec((1, dv), lambda i: (0, 0)),
            pl.BlockSpec((grp, H_C, dk, dv), st),
        ],
        out_specs=[pl.BlockSpec((rows, wc), tok), pl.BlockSpec((grp, H_C, dk, dv), st)],
        out_shape=[jax.ShapeDtypeStruct((t, wc), BF16), jax.ShapeDtypeStruct((db, H_C, dk, dv), F32)],
        compiler_params=_cparams(1),
        name="gla_sample",
    )(q, k, v, la, og, s0)


def _out1_body(x_ref, o_ref, gs_ref, w_ref, y_ref):
    y_ref[...] = x_ref[...] + _dot(o_ref[...] * gs_ref[...], w_ref[...])


def _out1(x2d, o, gs, w_bf):
    t, d = x2d.shape
    tm = min(TOKEN_TILE, t)
    row = lambda i: (i, 0)
    return pl.pallas_call(
        _out1_body,
        grid=(t // tm,),
        in_specs=[pl.BlockSpec((tm, d), row), pl.BlockSpec((tm, o.shape[1]), row),
                  pl.BlockSpec((tm, gs.shape[1]), row), pl.BlockSpec(w_bf.shape, lambda i: (0, 0))],
        out_specs=pl.BlockSpec((tm, d), row),
        out_shape=jax.ShapeDtypeStruct((t, d), F32),
        compiler_params=_cparams(1),
        name="out1",
    )(x2d, o, gs, w_bf)


def _t5_bucket_np(dist):
    n = np.maximum(dist, 0)
    max_exact = N_BUCKETS // 2
    nf = np.maximum(n, 1).astype(np.float32)
    ratio = np.log(nf / np.float32(max_exact)) / np.float32(math.log(MAX_DISTANCE / max_exact))
    large = max_exact + (ratio * np.float32(N_BUCKETS - max_exact)).astype(np.int32)
    large = np.minimum(large, N_BUCKETS - 1)
    return np.where(n < max_exact, n, large)


def _prompt_bias_tiles(rel_bias, tile):
    kk = np.arange(tile)[:, None]
    qq = np.arange(tile)[None, :]
    d0 = qq - kk
    d1 = tile + qq - kk
    far = rel_bias[N_BUCKETS - 1]
    assert int(_t5_bucket_np(np.array([tile + 1]))[0]) == N_BUCKETS - 1
    t0 = jnp.transpose(rel_bias[_t5_bucket_np(d0)], (2, 0, 1)) - far[:, None, None]
    t0 = jnp.where(jnp.asarray(d0 >= 0)[None], t0, MASK_VALUE)
    t1 = jnp.transpose(rel_bias[_t5_bucket_np(d1)], (2, 0, 1)) - far[:, None, None]
    tiles = jnp.stack([t0, t1], axis=1)
    return jnp.concatenate([tiles, tiles], axis=-1).astype(F32)


def _sample_bias(rel_bias, past, s_len):
    qq = np.arange(s_len)[:, None]
    d_past = past + qq - np.arange(past)[None, :]
    d_new = qq - np.arange(LANES)[None, :]
    valid_new = (d_new >= 0) & (np.arange(LANES)[None, :] < s_len)

    def rows(d, mask):
        bv = rel_bias[_t5_bucket_np(d)]
        if mask is not None:
            bv = jnp.where(jnp.asarray(mask)[:, :, None], bv, MASK_VALUE)
        bv = jnp.transpose(bv, (2, 0, 1)).reshape(H_A * s_len, d.shape[1])
        return jnp.concatenate([bv, bv], axis=0).astype(F32)

    return rows(d_past, None), rows(d_new, valid_new)


def _sample_query_mask(s_len):
    rows = np.arange(2 * H_A * s_len)
    c = rows // (H_A * s_len)
    h = (rows // s_len) % H_A
    lane_grp = np.arange(H_A * 2 * DH_A) // DH_A
    return jnp.asarray(lane_grp[None, :] == (h * 2 + c)[:, None], dtype=BF16)


def _spatial_weights(sp_w, sp_b, seg, cg):
    g, n, _ = sp_w.shape
    w = sp_w * jnp.tril(jnp.ones((n, n), sp_w.dtype))
    reps = n // seg
    if reps > 1:
        eye = jnp.eye(reps, dtype=sp_w.dtype)
        w = jnp.einsum("ab,gts->gatbs", eye, w[:, :seg, :seg]).reshape(g, n, n)
    bias = jnp.tile(sp_b[:, :seg], (1, reps))
    bias = jnp.repeat(jnp.transpose(bias), cg, axis=1)
    return w.astype(BF16), bias.astype(F32)


def kernel(x_prompt, x_sample, cache_k, cache_v, state_gla, page_table, rel_bias, norm0_g, w_in0, q_norm_g,
           k_norm_g, lam, subln_g, ln_v_g, ln_v_b, spatial_w, spatial_b, w_out0, norm1_g, w_in1, w_gate,
           b_gate, gla_norm_g, w_out1):
    b, l, d = x_prompt.shape
    db, s_len, _ = x_sample.shape
    n_pool, page = cache_k.shape[1], cache_k.shape[2]
    past = page_table.shape[1] * page
    wa = H_A * DV_A
    lam_init = 0.8 - 0.6 * math.exp(-0.3 * 0)
    rank = w_gate.shape[1]
    hk = w_gate.shape[2]
    wc = w_out1.shape[1]
    dk, dv = hk // H_C, wc // H_C

    w0 = w_in0[0].astype(BF16)
    wo0 = w_out0[0].astype(BF16)
    grp = np.arange(wa) // DH_A
    red = jnp.asarray((grp[:, None] == np.arange(LANES)[None, :]) / DH_A, dtype=BF16)
    expd = jnp.asarray(np.arange(LANES)[:, None] == grp[None, :], dtype=BF16)
    qg = (jnp.tile(q_norm_g[0].reshape(-1), H_A) * np.float32(DH_A ** -0.5)).reshape(1, wa).astype(F32)
    kg = jnp.tile(k_norm_g[0].reshape(-1), H_A).reshape(1, wa).astype(F32)
    ng0 = norm0_g[0].reshape(1, d)
    lng = ln_v_g[0].reshape(1, -1)
    lnb = ln_v_b[0].reshape(1, -1)
    sub_scaled = subln_g[0] * np.float32(1.0 - lam_init)
    subg_cols = jnp.broadcast_to(sub_scaled[:, None], (DV_A, ATTN_TILE)).astype(F32)
    subg_rows = jnp.tile(sub_scaled, H_A).reshape(1, wa).astype(F32)
    bias_tiles = _prompt_bias_tiles(rel_bias, ATTN_TILE)
    bias_past, bias_new = _sample_bias(rel_bias, past, s_len)
    wq_mask = _sample_query_mask(s_len)
    cg = ln_v_g.shape[1] // G_B
    wsp_p, bsp_p = _spatial_weights(spatial_w[0], spatial_b[0], CHUNK_B, cg)
    wsp_s, bsp_s = _spatial_weights(spatial_w[0], spatial_b[0], s_len, cg)

    w1 = w_in1[0]
    w1_main = w1[:, :2 * hk + 2 * wc].astype(BF16)
    w1_a = jnp.pad(w1[:, 2 * hk + 2 * wc:], ((0, 0), (0, LANES - rank))).astype(BF16)
    wg = jnp.pad(w_gate[0], ((0, LANES - rank), (0, 0))).astype(BF16)
    bg = b_gate[0].reshape(1, hk)
    ng1 = norm1_g[0].reshape(1, d)
    og = gla_norm_g[0].reshape(1, dv)
    wo1 = w_out1[0].astype(BF16)

    xp = x_prompt.reshape(b * l, d)
    xs = x_sample.reshape(db * s_len, d)
    ck_t = jnp.transpose(cache_k[0], (0, 2, 3, 4, 1)).reshape(n_pool, wa, page)
    cv_rows = cache_v[0].reshape(n_pool, page * H_A, DV_A)

    qp, kp, vp, gap, ubp, vbp = _proj0(xp, ng0, w0, red, expd, qg, kg, lng, lnb, vb_dtype=BF16)
    qs, ks, vs, gas, ubs, vbs = _proj0(xs, ng0, w0, red, expd, qg, kg, lng, lnb, vb_dtype=F32)

    at_p = _attn_prompt(qp.reshape(b, l, wa), kp.reshape(b, l, wa), vp.reshape(b, l, wa), lam[0],
                        bias_tiles, subg_cols, lam_init=lam_init)
    at_s = _attn_sample(page_table, qs.reshape(db, s_len, wa), ks.reshape(db, s_len, wa),
                        vs.reshape(db, s_len, wa), ck_t, cv_rows, lam[0], wq_mask, bias_past, bias_new,
                        subg_rows, lam_init=lam_init)

    yp0 = _mix0(xp, at_p.reshape(b * l, wa), gap, ubp, vbp, wsp_p, bsp_p, wo0)
    ys0 = _mix0(xs, at_s.reshape(db * s_len, wa), gas, ubs, vbs, wsp_s, bsp_s, wo0)

    q_scale = np.float32(dk ** -0.5)
    q1p, k1p, v1p, gsp, lap = _proj1(yp0, ng1, w1_main, w1_a, wg, bg, hk=hk, wc=wc, q_scale=q_scale)
    q1s, k1s, v1s, gss, las = _proj1(ys0, ng1, w1_main, w1_a, wg, bg, hk=hk, wc=wc, q_scale=q_scale)

    o_p, st_p = _gla_prompt(q1p.reshape(b, l, hk), k1p.reshape(b, l, hk), v1p.reshape(b, l, wc),
                            lap.reshape(b, l, hk), og, jnp.zeros((b, H_C, dk, dv), F32))
    o_s, st_s = _gla_sample(q1s, k1s, v1s, las, og, state_gla[0], seg=s_len)

    yp1 = _out1(yp0, o_p.reshape(b * l, wc), gsp, wo1)
    ys1 = _out1(ys0, o_s, gss, wo1)

    return (
        yp1.reshape(b, l, d),
        ys1.reshape(db, s_len, d),
        kp.reshape(1, b, l, H_A, 2, DH_A),
        vp.reshape(1, b, l, H_A, DV_A),
        ks.reshape(1, db, s_len, H_A, 2, DH_A),
        vs.reshape(1, db, s_len, H_A, DV_A),
        vbs.reshape(1, db, s_len, -1),
        st_p.reshape(1, b, H_C, dk, dv),
        st_s.reshape(1, db, H_C, dk, dv),
    )
```

```python
import functools
import math

import numpy as np
import jax
import jax.numpy as jnp
from jax import lax
from jax.experimental import pallas as pl
from jax.experimental.pallas import tpu as pltpu

F32 = jnp.float32
BF16 = jnp.bfloat16

H_A = 8
DH_A = 64
DV_A = 2 * DH_A
G_B = 8
CHUNK_B = 128
H_C = 4
GLA_TAU = 16.0
N_BUCKETS = 32
MAX_DISTANCE = 128
EPS = 1e-6
MASK_VALUE = -1e30
LOG2E = 1.4426950408889634
SAFE_SCORE_BOUND = 60.0

LANES = 128
SUBLANES = 8
VMEM_LIMIT_BYTES = 56 * 1024 * 1024

TOKEN_TILE = 256
ATTN_TILE = TOKEN_TILE
GLA_CHUNK_P = 128
GLA_GROUP_S = 8


def _cparams(n_axes):
    return pltpu.CompilerParams(
        dimension_semantics=("arbitrary",) * n_axes,
        vmem_limit_bytes=VMEM_LIMIT_BYTES,
    )


def _dot(a, b):
    return jnp.dot(a, b, preferred_element_type=F32)


def _dot_nt(a, b):
    return lax.dot_general(a, b, (((1,), (1,)), ((), ())), preferred_element_type=F32)


def _dot_tn(a, b):
    return lax.dot_general(a, b, (((0,), (0,)), ((), ())), preferred_element_type=F32)


def _silu(x):
    return x * jax.nn.sigmoid(x)


def _gelu(x):
    return 0.5 * x * (1.0 + lax.erf(x * np.float32(math.sqrt(0.5))))


def _rms_rows(x, g):
    return x * lax.rsqrt(jnp.mean(x * x, axis=-1, keepdims=True) + EPS) * g


def _split_bf16(x):
    hi = x.astype(BF16)
    lo = (x - hi.astype(F32)).astype(BF16)
    return hi, lo


def _proj0_body(x_ref, ng_ref, w_ref, red_ref, exp_ref, qg_ref, kg_ref, lng_ref, lnb_ref, *out_refs,
                width, prompt):
    if prompt:
        q_ref, kb_ref, kt_ref, v_ref, vt_ref, ga_ref, ub_ref, vb_ref = out_refs
    else:
        q_ref, k_ref, v_ref, ga_ref, ub_ref, vb_ref = out_refs
    xb = _rms_rows(x_ref[...], ng_ref[...]).astype(BF16)

    def seg(i):
        return _dot(xb, w_ref[:, i * width:(i + 1) * width])

    def group_rms(z, g):
        ms = _dot((z * z).astype(BF16), red_ref[...])
        hi, lo = _split_bf16(lax.rsqrt(ms + EPS))
        scale = _dot(hi, exp_ref[...]) + _dot(lo, exp_ref[...])
        return z * scale * g

    q_ref[...] = group_rms(seg(0), qg_ref[...]).astype(q_ref.dtype)
    k = group_rms(seg(1), kg_ref[...])
    v = seg(2)
    v_ref[...] = v
    if prompt:
        kb_ref[...] = k.astype(kb_ref.dtype)
        kt_ref[...] = k.T
        vt_ref[...] = v.T.astype(vt_ref.dtype)
    else:
        k_ref[...] = k
    ga_ref[...] = _silu(seg(3)).astype(ga_ref.dtype)
    ub_ref[...] = (_gelu(seg(4)) * _silu(seg(6))).astype(ub_ref.dtype)
    hv = _gelu(seg(5))
    hc = hv - jnp.mean(hv, axis=-1, keepdims=True)
    vb = hc * lax.rsqrt(jnp.mean(hc * hc, axis=-1, keepdims=True) + EPS)
    vb_ref[...] = (vb * lng_ref[...] + lnb_ref[...]).astype(vb_ref.dtype)


def _proj0(x2d, ng, w_bf, red, expd, qg, kg, lng, lnb, *, vb_dtype, prompt_batch=None):
    t, d = x2d.shape
    n_in = w_bf.shape[1]
    width = n_in // 7
    tm = min(TOKEN_TILE, t)
    row = lambda i: (i, 0)
    full = lambda i: (0, 0)
    tok = lambda dt: jax.ShapeDtypeStruct((t, width), dt)
    row_spec = pl.BlockSpec((tm, width), row)
    prompt = prompt_batch is not None
    if prompt:
        per = t // prompt_batch // tm
        outs = [tok(BF16), tok(BF16), jax.ShapeDtypeStruct((prompt_batch, width, per * tm), F32), tok(F32),
                jax.ShapeDtypeStruct((prompt_batch, per, width, tm), BF16), tok(BF16), tok(BF16), tok(vb_dtype)]
        out_specs = [row_spec, row_spec,
                     pl.BlockSpec((None, width, tm), lambda i: (i // per, 0, i % per)), row_spec,
                     pl.BlockSpec((None, None, width, tm), lambda i: (i // per, i % per, 0, 0)),
                     row_spec, row_spec, row_spec]
    else:
        outs = [tok(BF16), tok(F32), tok(F32), tok(BF16), tok(BF16), tok(vb_dtype)]
        out_specs = [row_spec] * 6
    return pl.pallas_call(
        functools.partial(_proj0_body, width=width, prompt=prompt),
        grid=(t // tm,),
        in_specs=[
            pl.BlockSpec((tm, d), row),
            pl.BlockSpec((1, d), full),
            pl.BlockSpec((d, n_in), full),
            pl.BlockSpec(red.shape, full),
            pl.BlockSpec(expd.shape, full),
            pl.BlockSpec((1, width), full),
            pl.BlockSpec((1, width), full),
            pl.BlockSpec((1, width), full),
            pl.BlockSpec((1, width), full),
        ],
        out_specs=out_specs,
        out_shape=outs,
        compiler_params=_cparams(1),
        name="proj0",
    )(x2d, ng, w_bf, red, expd, qg, kg, lng, lnb)


def _lambda_value(lam_ref, lam_init):
    lf = lam_ref[...]
    s1 = jnp.sum(lf[0:1] * lf[1:2], axis=-1, keepdims=True)
    s2 = jnp.sum(lf[2:3] * lf[3:4], axis=-1, keepdims=True)
    return jnp.exp(s1) - jnp.exp(s2) + lam_init


def _attn_prompt_body(lam_ref, q_ref, k_ref, vt_ref, bias_ref, subg_ref, o_ref, sa_sc, sb_sc, m_sc, l_sc,
                      acc_sc, *, tile, lam_init, stable):
    qi = pl.program_id(2)
    hd = 2 * DH_A
    n_heads = q_ref.shape[1] // hd
    lam = _lambda_value(lam_ref, lam_init)
    q = q_ref[...]
    lane = lax.broadcasted_iota(jnp.int32, (tile, hd), 1)
    q2 = []
    for hh in range(n_heads):
        qh = q[:, hh * hd:(hh + 1) * hd]
        zero = jnp.zeros_like(qh)
        q2.append(jnp.concatenate([jnp.where(lane < DH_A, qh, zero), jnp.where(lane >= DH_A, qh, zero)],
                                  axis=0))

    def sublane_partial(x):
        return x.reshape(x.shape[0] // SUBLANES, SUBLANES, x.shape[1]).sum(axis=0)

    def scores(t, buf):
        j = qi - t
        for hh in range(n_heads):
            buf[hh] = _dot_nt(k_ref[j, :, hh * hd:(hh + 1) * hd], q2[hh])

    def consume(t, buf):
        j = qi - t
        kind = jnp.minimum(t, 2)
        ps = []
        for hh in range(n_heads):
            s = buf[hh] + bias_ref[hh, kind]
            if stable:
                m = m_sc[hh]
                m_new = jnp.maximum(m, jnp.max(s, axis=0, keepdims=True))
                alpha = jnp.exp2(m - m_new)
                m_sc[hh] = m_new
                p = jnp.exp2(s - m_new)
                l_sc[hh] = alpha * l_sc[hh] + sublane_partial(p)
                acc_sc[hh] = alpha * acc_sc[hh]
            else:
                p = jnp.exp2(s)
                l_sc[hh] = l_sc[hh] + sublane_partial(p)
            ps.append(p.astype(BF16))
        for hh in range(n_heads):
            acc_sc[hh] = acc_sc[hh] + _dot(vt_ref[j, hh * DV_A:(hh + 1) * DV_A, :], ps[hh])

    m_sc[...] = jnp.full(m_sc.shape, MASK_VALUE, F32)
    l_sc[...] = jnp.zeros(l_sc.shape, F32)
    acc_sc[...] = jnp.zeros(acc_sc.shape, F32)
    n_keys = qi + 1
    n_pairs = (n_keys - 1) // 2
    scores(0, sa_sc)

    def body(pair, c):
        t = 2 * pair
        scores(t + 1, sb_sc)
        consume(t, sa_sc)
        scores(t + 2, sa_sc)
        consume(t + 1, sb_sc)
        return c

    lax.fori_loop(0, n_pairs, body, 0)
    t_last = 2 * n_pairs

    @pl.when(n_keys - t_last == 1)
    def _():
        consume(t_last, sa_sc)

    @pl.when(n_keys - t_last == 2)
    def _():
        scores(t_last + 1, sb_sc)
        consume(t_last, sa_sc)
        consume(t_last + 1, sb_sc)

    outs = []
    for hh in range(n_heads):
        l8, acc = l_sc[hh], acc_sc[hh]
        o_all = acc * (1.0 / jnp.sum(l8, axis=0, keepdims=True))
        o = o_all[:, :tile] - lam * o_all[:, tile:]
        on = o * lax.rsqrt(jnp.mean(o * o, axis=0, keepdims=True) + EPS) * subg_ref[...]
        outs.append(on.T)
    o_ref[...] = jnp.concatenate(outs, axis=1).astype(o_ref.dtype)


def _attn_prompt(q, k, vt, lam_p, bias_tiles, subg_cols, *, lam_init, stable):
    b, l, w = q.shape
    tile = ATTN_TILE
    n_tiles = l // tile
    hp = 2
    wb = hp * 2 * DH_A
    return pl.pallas_call(
        functools.partial(_attn_prompt_body, tile=tile, lam_init=lam_init, stable=stable),
        grid=(b, H_A // hp, n_tiles),
        in_specs=[
            pl.BlockSpec(lam_p.shape, lambda bi, h, i: (0, 0)),
            pl.BlockSpec((None, tile, wb), lambda bi, h, i: (bi, i, h)),
            pl.BlockSpec((None, n_tiles, tile, wb), lambda bi, h, i: (bi, 0, 0, h)),
            pl.BlockSpec((None, n_tiles, wb, tile), lambda bi, h, i: (bi, 0, h, 0)),
            pl.BlockSpec((hp, 3, tile, 2 * tile), lambda bi, h, i: (h, 0, 0, 0)),
            pl.BlockSpec((DV_A, tile), lambda bi, h, i: (0, 0)),
        ],
        out_specs=pl.BlockSpec((None, tile, wb), lambda bi, h, i: (bi, i, h)),
        out_shape=jax.ShapeDtypeStruct((b, l, w), BF16),
        scratch_shapes=[
            pltpu.VMEM((hp, tile, 2 * tile), F32),
            pltpu.VMEM((hp, tile, 2 * tile), F32),
            pltpu.VMEM((hp, 1, 2 * tile), F32),
            pltpu.VMEM((hp, SUBLANES, 2 * tile), F32),
            pltpu.VMEM((hp, DV_A, 2 * tile), F32),
        ],
        compiler_params=_cparams(3),
        name="attn_prompt_stable" if stable else "attn_prompt",
    )(lam_p, q, k, vt, bias_tiles, subg_cols)


def _attn_sample_body(pt_ref, lam_ref, q_ref, kn_ref, vn_ref, mask_ref, bp_ref, bn_ref, subg_ref,
                      ck_hbm, cv_hbm, o_ref, kbuf, vbuf, sem, *, n_pages, page, key_chunk, lam_init):
    b = pl.program_id(0)
    nb = pl.num_programs(0)
    slot = b % 2

    def page_copies(seq, sl):
        cps = []
        for p in range(n_pages):
            pg = pt_ref[seq, p]
            cps.append(pltpu.make_async_copy(ck_hbm.at[pg], kbuf.at[sl, :, pl.ds(p * page, page)],
                                             sem.at[sl, 0]))
            cps.append(pltpu.make_async_copy(cv_hbm.at[pg], vbuf.at[sl, pl.ds(p * page * H_A, page * H_A), :],
                                             sem.at[sl, 1]))
        return cps

    @pl.when(b == 0)
    def _():
        for c in page_copies(0, 0):
            c.start()

    @pl.when(b + 1 < nb)
    def _():
        for c in page_copies(b + 1, 1 - slot):
            c.start()

    lam = _lambda_value(lam_ref, lam_init)
    q = q_ref[...]
    n_rows = mask_ref.shape[0]
    wq = jnp.concatenate([q] * (n_rows // q.shape[0]), axis=0) * mask_ref[...]

    for c in page_copies(b, slot):
        c.wait()

    s_len = q.shape[0]
    n_chunks = (n_pages * page) // key_chunk
    pad_rows = bn_ref.shape[1] - s_len
    kn = jnp.concatenate([kn_ref[...], jnp.zeros((pad_rows, kn_ref.shape[1]), F32)], axis=0).astype(BF16)
    vn = jnp.concatenate([vn_ref[...], jnp.zeros((pad_rows, vn_ref.shape[1]), F32)], axis=0).astype(BF16)
    s_parts = [_dot_nt(wq, kn) + bn_ref[...]]
    for c in range(n_chunks):
        keys = pl.ds(c * key_chunk, key_chunk)
        s_parts.append(_dot(wq, kbuf[slot, :, keys].astype(BF16)) + bp_ref[:, keys])

    m = jnp.max(s_parts[0], axis=1, keepdims=True)
    for s in s_parts[1:]:
        m = jnp.maximum(m, jnp.max(s, axis=1, keepdims=True))
    p_parts = [jnp.exp2(s - m) for s in s_parts]
    l = jnp.sum(p_parts[0], axis=1, keepdims=True)
    for p in p_parts[1:]:
        l = l + jnp.sum(p, axis=1, keepdims=True)
    row = lax.broadcasted_iota(jnp.int32, l.shape, 0)
    half = H_A * s_len
    scale = jnp.where(row < half, 1.0, -lam) / l

    heads = []
    for h in range(H_A):
        r0 = h * s_len

        def head_rows(x):
            return jnp.concatenate([x[r0:r0 + s_len], x[half + r0:half + r0 + s_len]], axis=0)

        acc = _dot(head_rows(p_parts[0]).astype(BF16), vn[:, h * DV_A:(h + 1) * DV_A])
        for c in range(n_chunks):
            vh = vbuf[slot, pl.ds(c * key_chunk * H_A + h, key_chunk, stride=H_A), :].astype(BF16)
            acc = acc + _dot(head_rows(p_parts[c + 1]).astype(BF16), vh)
        acc = acc * head_rows(scale)
        o_h = acc[:s_len] + acc[s_len:]
        heads.append(o_h * lax.rsqrt(jnp.mean(o_h * o_h, axis=-1, keepdims=True) + EPS))
    o_ref[...] = (jnp.concatenate(heads, axis=1) * subg_ref[...]).astype(o_ref.dtype)


def _attn_sample(page_table, q, k_new, v_new, cache_k, cache_v, lam_p, wq_mask, bias_past, bias_new,
                 subg_rows, *, lam_init):
    db, s_len, w = q.shape
    n_pages = page_table.shape[1]
    page = cache_k.shape[2]
    past = n_pages * page
    key_chunk = min(512, past)
    seq3 = lambda b, pt: (b, 0, 0)
    full2 = lambda b, pt: (0, 0)
    grid_spec = pltpu.PrefetchScalarGridSpec(
        num_scalar_prefetch=1,
        grid=(db,),
        in_specs=[
            pl.BlockSpec(lam_p.shape, full2),
            pl.BlockSpec((None, s_len, w), seq3),
            pl.BlockSpec((None, s_len, w), seq3),
            pl.BlockSpec((None, s_len, w), seq3),
            pl.BlockSpec(wq_mask.shape, full2),
            pl.BlockSpec(bias_past.shape, full2),
            pl.BlockSpec(bias_new.shape, full2),
            pl.BlockSpec(subg_rows.shape, full2),
            pl.BlockSpec(memory_space=pl.ANY),
            pl.BlockSpec(memory_space=pl.ANY),
        ],
        out_specs=pl.BlockSpec((None, s_len, w), seq3),
        scratch_shapes=[
            pltpu.VMEM((2, w, past), F32),
            pltpu.VMEM((2, past * H_A, DV_A), F32),
            pltpu.SemaphoreType.DMA((2, 2)),
        ],
    )
    return pl.pallas_call(
        functools.partial(_attn_sample_body, n_pages=n_pages, page=page, key_chunk=key_chunk,
                          lam_init=lam_init),
        grid_spec=grid_spec,
        out_shape=jax.ShapeDtypeStruct((db, s_len, w), BF16),
        compiler_params=_cparams(1),
        name="attn_sample",
    )(page_table, lam_p, q, k_new, v_new, wq_mask, bias_past, bias_new, subg_rows, cache_k, cache_v)


def _mix0_body(x_ref, at_ref, ga_ref, ub_ref, vb_ref, wsp_ref, bsp_ref, wo_ref, y_ref):
    tm = x_ref.shape[0]
    wa = at_ref.shape[1]
    vb = vb_ref[...].astype(BF16)
    n_sub = tm // CHUNK_B
    cg = vb.shape[1] // G_B
    cols = []
    for g in range(G_B):
        lanes = slice(g * cg, (g + 1) * cg)
        rows = [_dot(wsp_ref[g], vb[sb * CHUNK_B:(sb + 1) * CHUNK_B, lanes]) for sb in range(n_sub)]
        cols.append(jnp.concatenate(rows, axis=0) if n_sub > 1 else rows[0])
    bsp = bsp_ref[...]
    bias = jnp.concatenate([bsp] * n_sub, axis=0) if n_sub > 1 else bsp
    mixed = jnp.concatenate(cols, axis=1) + bias
    ob = (ub_ref[...].astype(F32) * mixed).astype(BF16)
    oa = at_ref[...] * ga_ref[...]
    y_ref[...] = x_ref[...] + _dot(oa, wo_ref[:wa, :]) + _dot(ob, wo_ref[wa:, :])


def _mix0(x2d, attn, gas, ub, vb, wsp, bsp, wo_bf):
    t, d = x2d.shape
    tm = min(TOKEN_TILE, t)
    row = lambda i: (i, 0)
    full = lambda i: (0, 0)
    return pl.pallas_call(
        _mix0_body,
        grid=(t // tm,),
        in_specs=[
            pl.BlockSpec((tm, d), row),
            pl.BlockSpec((tm, attn.shape[1]), row),
            pl.BlockSpec((tm, gas.shape[1]), row),
            pl.BlockSpec((tm, ub.shape[1]), row),
            pl.BlockSpec((tm, vb.shape[1]), row),
            pl.BlockSpec(wsp.shape, lambda i: (0, 0, 0)),
            pl.BlockSpec(bsp.shape, full),
            pl.BlockSpec(wo_bf.shape, full),
        ],
        out_specs=pl.BlockSpec((tm, d), row),
        out_shape=jax.ShapeDtypeStruct((t, d), F32),
        compiler_params=_cparams(1),
        name="mix0",
    )(x2d, attn, gas, ub, vb, wsp, bsp, wo_bf)


def _proj1_body(x_ref, ng_ref, w_ref, wa_ref, wg_ref, bg_ref, q_ref, k_ref, v_ref, gs_ref, la_ref,
                *, hk, wc, q_scale):
    xb = _rms_rows(x_ref[...], ng_ref[...]).astype(BF16)
    q_ref[...] = _dot(xb, w_ref[:, 0:hk]) * q_scale
    k_ref[...] = _dot(xb, w_ref[:, hk:2 * hk])
    v_ref[...] = _dot(xb, w_ref[:, 2 * hk:2 * hk + wc]).astype(v_ref.dtype)
    gs_ref[...] = _silu(_dot(xb, w_ref[:, 2 * hk + wc:2 * hk + 2 * wc])).astype(gs_ref.dtype)
    za = _dot(xb, wa_ref[...]).astype(BF16)
    xg = _dot(za, wg_ref[...]) + bg_ref[...]
    la_ref[...] = (jnp.minimum(xg, 0.0) - jnp.log1p(jnp.exp(-jnp.abs(xg)))) * np.float32(1.0 / GLA_TAU)


def _proj1(x2d, ng, w_bf, wa_bf, wg_bf, bg, *, hk, wc, q_scale):
    t, d = x2d.shape
    tm = min(TOKEN_TILE, t)
    row = lambda i: (i, 0)
    full = lambda i: (0, 0)
    outs = [
        jax.ShapeDtypeStruct((t, hk), F32),
        jax.ShapeDtypeStruct((t, hk), F32),
        jax.ShapeDtypeStruct((t, wc), BF16),
        jax.ShapeDtypeStruct((t, wc), BF16),
        jax.ShapeDtypeStruct((t, hk), F32),
    ]
    return pl.pallas_call(
        functools.partial(_proj1_body, hk=hk, wc=wc, q_scale=q_scale),
        grid=(t // tm,),
        in_specs=[
            pl.BlockSpec((tm, d), row),
            pl.BlockSpec((1, d), full),
            pl.BlockSpec(w_bf.shape, full),
            pl.BlockSpec(wa_bf.shape, full),
            pl.BlockSpec(wg_bf.shape, full),
            pl.BlockSpec((1, hk), full),
        ],
        out_specs=[pl.BlockSpec((tm, hk), row), pl.BlockSpec((tm, hk), row), pl.BlockSpec((tm, wc), row),
                   pl.BlockSpec((tm, wc), row), pl.BlockSpec((tm, hk), row)],
        out_shape=outs,
        compiler_params=_cparams(1),
        name="proj1",
    )(x2d, ng, w_bf, wa_bf, wg_bf, bg)


def _segment_cumsum(x, seg):
    n = x.shape[0]
    row = lax.broadcasted_iota(jnp.int32, x.shape, 0) % seg
    d = 1
    while d < seg:
        x = x + jnp.where(row >= d, pltpu.roll(x, d, axis=0), 0.0)
        d *= 2
    return x


def _block_reference(bc, hb):
    n, w = bc.shape
    if hb >= SUBLANES:
        pieces = []
        for blk in range(n // (2 * hb)):
            r = blk * 2 * hb + hb - 1
            pieces.append(jnp.broadcast_to(bc[r:r + 1, :], (2 * hb, w)))
        return jnp.concatenate(pieces, axis=0) if len(pieces) > 1 else pieces[0]
    pos = lax.broadcasted_iota(jnp.int32, bc.shape, 0) % (2 * hb)
    out = bc
    for delta in range(-hb, hb):
        if delta == 0:
            continue
        shifted = pltpu.roll(bc, (-delta) % n, axis=0)
        out = jnp.where(pos == hb - 1 - delta, shifted, out)
    return out


def _gla_intra_scores(q, k, bc, levels):
    n = q.shape[0]
    ri = lax.broadcasted_iota(jnp.int32, (n, n), 0)
    ci = lax.broadcasted_iota(jnp.int32, (n, n), 1)
    rowpos = lax.broadcasted_iota(jnp.int32, q.shape, 0)
    att = jnp.where(ri == ci, _dot_nt(q.astype(BF16), k.astype(BF16)), 0.0)
    for hb in levels:
        ref = _block_reference(bc, hb)
        is_q = (rowpos % (2 * hb)) >= hb
        f = jnp.exp(jnp.where(is_q, bc - ref, ref - bc))
        qt = jnp.where(is_q, q * f, 0.0).astype(BF16)
        kt = jnp.where(is_q, 0.0, k * f).astype(BF16)
        same = (ri // (2 * hb)) == (ci // (2 * hb))
        att = att + jnp.where(same, _dot_nt(qt, kt), 0.0)
    return att


def _levels(seg):
    out, hb = [], seg // 2
    while hb >= 1:
        out.append(hb)
        hb //= 2
    return out


def _gla_prompt_body(q_ref, k_ref, v_ref, la_ref, og_ref, s0_ref, o_ref, sout_ref, st_sc, *, dk, dv):
    ci = pl.program_id(1)
    chunk = q_ref.shape[0]

    @pl.when(ci == 0)
    def _():
        for h in range(H_C):
            st_sc[h] = s0_ref[h].T

    for h in range(H_C):
        kl = slice(h * dk, (h + 1) * dk)
        vl = slice(h * dv, (h + 1) * dv)
        q, k, la = q_ref[:, kl], k_ref[:, kl], la_ref[:, kl]
        v = v_ref[:, vl]
        bc = _segment_cumsum(la, chunk)
        st = st_sc[h]
        att = _gla_intra_scores(q, k, bc, _levels(chunk))
        o = _dot_nt((q * jnp.exp(bc)).astype(BF16), st.astype(BF16)) + _dot(att.astype(BF16), v)
        b_last = bc[chunk - 1:chunk, :]
        k_dec = (k * jnp.exp(b_last - bc)).astype(BF16)
        st_sc[h] = st * jnp.exp(b_last) + _dot_tn(v, k_dec)
        o_ref[:, vl] = _rms_rows(o, og_ref[...]).astype(o_ref.dtype)

    @pl.when(ci == pl.num_programs(1) - 1)
    def _():
        for h in range(H_C):
            sout_ref[h] = st_sc[h].T


def _gla_prompt(q, k, v, la, og, s0):
    b, l, hk = q.shape
    wc = v.shape[2]
    dk, dv = hk // H_C, wc // H_C
    chunk = min(GLA_CHUNK_P, l)
    tok = lambda bi, ci: (bi, ci, 0)
    st = lambda bi, ci: (bi, 0, 0, 0)
    return pl.pallas_call(
        functools.partial(_gla_prompt_body, dk=dk, dv=dv),
        grid=(b, l // chunk),
        in_specs=[
            pl.BlockSpec((None, chunk, hk), tok),
            pl.BlockSpec((None, chunk, hk), tok),
            pl.BlockSpec((None, chunk, wc), tok),
            pl.BlockSpec((None, chunk, hk), tok),
            pl.BlockSpec((1, dv), lambda bi, ci: (0, 0)),
            pl.BlockSpec((None, H_C, dk, dv), st),
        ],
        out_specs=[pl.BlockSpec((None, chunk, wc), tok), pl.BlockSpec((None, H_C, dk, dv), st)],
        out_shape=[jax.ShapeDtypeStruct((b, l, wc), BF16), jax.ShapeDtypeStruct((b, H_C, dk, dv), F32)],
        scratch_shapes=[pltpu.VMEM((H_C, dv, dk), F32)],
        compiler_params=_cparams(2),
        name="gla_prompt",
    )(q, k, v, la, og, s0)


def _gla_sample_body(q_ref, k_ref, v_ref, la_ref, og_ref, s0_ref, o_ref, sout_ref, *, dk, dv, seg):
    n_seq = s0_ref.shape[0]
    for h in range(H_C):
        kl = slice(h * dk, (h + 1) * dk)
        vl = slice(h * dv, (h + 1) * dv)
        q, k, la = q_ref[:, kl], k_ref[:, kl], la_ref[:, kl]
        v = v_ref[:, vl]
        bc = _segment_cumsum(la, seg)
        att = _gla_intra_scores(q, k, bc, _levels(seg))
        o_intra = _dot(att.astype(BF16), v)
        q_dec = (q * jnp.exp(bc)).astype(BF16)
        outs = []
        for s in range(n_seq):
            rows = slice(s * seg, (s + 1) * seg)
            st = s0_ref[s, h].T
            bl = bc[s * seg + seg - 1:s * seg + seg, :]
            k_dec = (k[rows] * jnp.exp(bl - bc[rows])).astype(BF16)
            outs.append(_dot_nt(q_dec[rows], st.astype(BF16)))
            sout_ref[s, h] = (st * jnp.exp(bl) + _dot_tn(v[rows], k_dec)).T
        o = jnp.concatenate(outs, axis=0) + o_intra
        o_ref[:, vl] = _rms_rows(o, og_ref[...]).astype(o_ref.dtype)


def _gla_sample(q, k, v, la, og, s0, *, seg):
    t, hk = q.shape
    wc = v.shape[1]
    dk, dv = hk // H_C, wc // H_C
    db = t // seg
    grp = min(GLA_GROUP_S, db)
    rows = grp * seg
    tok = lambda i: (i, 0)
    st = lambda i: (i, 0, 0, 0)
    return pl.pallas_call(
        functools.partial(_gla_sample_body, dk=dk, dv=dv, seg=seg),
        grid=(db // grp,),
        in_specs=[
            pl.BlockSpec((rows, hk), tok),
            pl.BlockSpec((rows, hk), tok),
            pl.BlockSpec((rows, wc), tok),
            pl.BlockSpec((rows, hk), tok),
            pl.BlockSpec((1, dv), lambda i: (0, 0)),
            pl.BlockSpec((grp, H_C, dk, dv), st),
        ],
        out_specs=[pl.BlockSpec((rows, wc), tok), pl.BlockSpec((grp, H_C, dk, dv), st)],
        out_shape=[jax.ShapeDtypeStruct((t, wc), BF16), jax.ShapeDtypeStruct((db, H_C, dk, dv), F32)],
        compiler_params=_cparams(1),
        name="gla_sample",
    )(q, k, v, la, og, s0)


def _out1_body(x_ref, o_ref, gs_ref, w_ref, y_ref):
    y_ref[...] = x_ref[...] + _dot(o_ref[...] * gs_ref[...], w_ref[...])


def _out1(x2d, o, gs, w_bf):
    t, d = x2d.shape
    tm = min(TOKEN_TILE, t)
    row = lambda i: (i, 0)
    return pl.pallas_call(
        _out1_body,
        grid=(t // tm,),
        in_specs=[pl.BlockSpec((tm, d), row), pl.BlockSpec((tm, o.shape[1]), row),
                  pl.BlockSpec((tm, gs.shape[1]), row), pl.BlockSpec(w_bf.shape, lambda i: (0, 0))],
        out_specs=pl.BlockSpec((tm, d), row),
        out_shape=jax.ShapeDtypeStruct((t, d), F32),
        compiler_params=_cparams(1),
        name="out1",
    )(x2d, o, gs, w_bf)


def _t5_bucket_np(dist):
    n = np.maximum(dist, 0)
    max_exact = N_BUCKETS // 2
    nf = np.maximum(n, 1).astype(np.float32)
    ratio = np.log(nf / np.float32(max_exact)) / np.float32(math.log(MAX_DISTANCE / max_exact))
    large = max_exact + (ratio * np.float32(N_BUCKETS - max_exact)).astype(np.int32)
    large = np.minimum(large, N_BUCKETS - 1)
    return np.where(n < max_exact, n, large)


def _bias_by_distance(rel_bias, n):
    buckets = _t5_bucket_np(np.arange(n))
    assert np.all(np.diff(buckets) >= 0)
    cuts = [0] + [int(i) + 1 for i in np.nonzero(np.diff(buckets))[0]] + [n]
    pieces = [jnp.broadcast_to(rel_bias[int(buckets[a])][:, None], (rel_bias.shape[1], e - a))
              for a, e in zip(cuts[:-1], cuts[1:])]
    return jnp.concatenate(pieces, axis=1).astype(F32) * np.float32(LOG2E)


def _toeplitz(w, n):
    h = w.shape[0]
    flat = jnp.tile(w, (1, n))[:, :n * (2 * n - 1)]
    return flat.reshape(h, n, 2 * n - 1)[:, :, n - 1:]


def _prompt_bias_tiles(rel_bias, tile):
    assert int(_t5_bucket_np(np.array([tile + 1]))[0]) == N_BUCKETS - 1
    tab = _bias_by_distance(rel_bias, 2 * tile)
    tab = tab - tab[:, 2 * tile - 1:]
    h = tab.shape[0]
    masked = jnp.full((h, tile - 1), MASK_VALUE, F32)
    pad = jnp.zeros((h, 1), F32)
    w0 = jnp.concatenate([masked, tab[:, :tile], pad], axis=1)
    w1 = jnp.concatenate([tab[:, 1:], pad], axis=1)
    tiles = jnp.stack([_toeplitz(w0, tile), _toeplitz(w1, tile), jnp.zeros((h, tile, tile), F32)], axis=1)
    return jnp.concatenate([tiles, tiles], axis=-1)


def _sample_bias(rel_bias, past, s_len):
    n = past + s_len
    rev = _bias_by_distance(rel_bias, n)[:, ::-1]
    h = rev.shape[0]
    bp = jnp.stack([rev[:, s_len - 1 - q:s_len - 1 - q + past] for q in range(s_len)], axis=1)
    small = jnp.concatenate([rev[:, past:], jnp.full((h, LANES), MASK_VALUE, F32)], axis=1)
    bn = jnp.stack([small[:, s_len - 1 - q:s_len - 1 - q + LANES] for q in range(s_len)], axis=1)
    bp = bp.reshape(h * s_len, past)
    bn = bn.reshape(h * s_len, LANES)
    return jnp.concatenate([bp, bp], axis=0), jnp.concatenate([bn, bn], axis=0)


def _sample_query_mask(s_len):
    rows = np.arange(2 * H_A * s_len)
    c = rows // (H_A * s_len)
    h = (rows // s_len) % H_A
    lane_grp = np.arange(H_A * 2 * DH_A) // DH_A
    return jnp.asarray(lane_grp[None, :] == (h * 2 + c)[:, None], dtype=BF16)


def _spatial_weights(sp_w, sp_b, seg, cg):
    g, n, _ = sp_w.shape
    w = sp_w * jnp.tril(jnp.ones((n, n), sp_w.dtype))
    reps = n // seg
    if reps > 1:
        eye = jnp.eye(reps, dtype=sp_w.dtype)
        w = jnp.einsum("ab,gts->gatbs", eye, w[:, :seg, :seg]).reshape(g, n, n)
    bias = jnp.tile(sp_b[:, :seg], (1, reps))
    bias = jnp.repeat(jnp.transpose(bias), cg, axis=1)
    return w.astype(BF16), bias.astype(F32)


def kernel(x_prompt, x_sample, cache_k, cache_v, state_gla, page_table, rel_bias, norm0_g, w_in0, q_norm_g,
           k_norm_g, lam, subln_g, ln_v_g, ln_v_b, spatial_w, spatial_b, w_out0, norm1_g, w_in1, w_gate,
           b_gate, gla_norm_g, w_out1):
    b, l, d = x_prompt.shape
    db, s_len, _ = x_sample.shape
    n_pool, page = cache_k.shape[1], cache_k.shape[2]
    past = page_table.shape[1] * page
    wa = H_A * DV_A
    lam_init = 0.8 - 0.6 * math.exp(-0.3 * 0)
    rank = w_gate.shape[1]
    hk = w_gate.shape[2]
    wc = w_out1.shape[1]
    dk, dv = hk // H_C, wc // H_C

    w0 = w_in0[0].astype(BF16)
    wo0 = w_out0[0].astype(BF16)
    grp = np.arange(wa) // DH_A
    red = jnp.asarray((grp[:, None] == np.arange(LANES)[None, :]) / DH_A, dtype=BF16)
    expd = jnp.asarray(np.arange(LANES)[:, None] == grp[None, :], dtype=BF16)
    qg = (jnp.tile(q_norm_g[0].reshape(-1), H_A) * np.float32(DH_A ** -0.5 * LOG2E)).reshape(1, wa).astype(F32)
    kg = jnp.tile(k_norm_g[0].reshape(-1), H_A).reshape(1, wa).astype(F32)
    ng0 = norm0_g[0].reshape(1, d)
    lng = ln_v_g[0].reshape(1, -1)
    lnb = ln_v_b[0].reshape(1, -1)
    sub_scaled = subln_g[0] * np.float32(1.0 - lam_init)
    subg_cols = jnp.broadcast_to(sub_scaled[:, None], (DV_A, ATTN_TILE)).astype(F32)
    subg_rows = jnp.tile(sub_scaled, H_A).reshape(1, wa).astype(F32)
    bias_tiles = _prompt_bias_tiles(rel_bias, ATTN_TILE)
    bias_past, bias_new = _sample_bias(rel_bias, past, s_len)
    wq_mask = _sample_query_mask(s_len)
    cg = ln_v_g.shape[1] // G_B
    wsp_p, bsp_p = _spatial_weights(spatial_w[0], spatial_b[0], CHUNK_B, cg)
    wsp_s, bsp_s = _spatial_weights(spatial_w[0], spatial_b[0], s_len, cg)

    w1 = w_in1[0]
    w1_main = w1[:, :2 * hk + 2 * wc].astype(BF16)
    w1_a = jnp.pad(w1[:, 2 * hk + 2 * wc:], ((0, 0), (0, LANES - rank))).astype(BF16)
    wg = jnp.pad(w_gate[0], ((0, LANES - rank), (0, 0))).astype(BF16)
    bg = b_gate[0].reshape(1, hk)
    ng1 = norm1_g[0].reshape(1, d)
    og = gla_norm_g[0].reshape(1, dv)
    wo1 = w_out1[0].astype(BF16)

    xp = x_prompt.reshape(b * l, d)
    xs = x_sample.reshape(db * s_len, d)
    ck_t = jnp.transpose(cache_k[0], (0, 2, 3, 4, 1)).reshape(n_pool, wa, page)
    cv_rows = cache_v[0].reshape(n_pool, page * H_A, DV_A)

    qp, kbp, ktp, vp, vtp, gap, ubp, vbp = _proj0(xp, ng0, w0, red, expd, qg, kg, lng, lnb, vb_dtype=BF16,
                                                  prompt_batch=b)
    qs, ks, vs, gas, ubs, vbs = _proj0(xs, ng0, w0, red, expd, qg, kg, lng, lnb, vb_dtype=F32)

    score_bound = (np.float32(1.02 * math.sqrt(DH_A) * LOG2E)
                   * jnp.max(jnp.abs(q_norm_g[0])) * jnp.max(jnp.abs(k_norm_g[0]))
                   + np.float32(LOG2E) * jnp.max(jnp.abs(rel_bias - rel_bias[N_BUCKETS - 1])))
    attn_args = (qp.reshape(b, l, wa), kbp.reshape(b, l // ATTN_TILE, ATTN_TILE, wa), vtp, lam[0],
                 bias_tiles, subg_cols)
    at_p = lax.cond(score_bound <= SAFE_SCORE_BOUND,
                    lambda args: _attn_prompt(*args, lam_init=lam_init, stable=False),
                    lambda args: _attn_prompt(*args, lam_init=lam_init, stable=True),
                    attn_args)
    at_s = _attn_sample(page_table, qs.reshape(db, s_len, wa), ks.reshape(db, s_len, wa),
                        vs.reshape(db, s_len, wa), ck_t, cv_rows, lam[0], wq_mask, bias_past, bias_new,
                        subg_rows, lam_init=lam_init)

    yp0 = _mix0(xp, at_p.reshape(b * l, wa), gap, ubp, vbp, wsp_p, bsp_p, wo0)
    ys0 = _mix0(xs, at_s.reshape(db * s_len, wa), gas, ubs, vbs, wsp_s, bsp_s, wo0)

    q_scale = np.float32(dk ** -0.5)
    q1p, k1p, v1p, gsp, lap = _proj1(yp0, ng1, w1_main, w1_a, wg, bg, hk=hk, wc=wc, q_scale=q_scale)
    q1s, k1s, v1s, gss, las = _proj1(ys0, ng1, w1_main, w1_a, wg, bg, hk=hk, wc=wc, q_scale=q_scale)

    o_p, st_p = _gla_prompt(q1p.reshape(b, l, hk), k1p.reshape(b, l, hk), v1p.reshape(b, l, wc),
                            lap.reshape(b, l, hk), og, jnp.zeros((b, H_C, dk, dv), F32))
    o_s, st_s = _gla_sample(q1s, k1s, v1s, las, og, state_gla[0], seg=s_len)

    yp1 = _out1(yp0, o_p.reshape(b * l, wc), gsp, wo1)
    ys1 = _out1(ys0, o_s, gss, wo1)

    return (
        yp1.reshape(b, l, d),
        ys1.reshape(db, s_len, d),
        jnp.transpose(ktp.reshape(b, H_A, 2, DH_A, l), (0, 4, 1, 2, 3))[None],
        vp.reshape(1, b, l, H_A, DV_A),
        ks.reshape(1, db, s_len, H_A, 2, DH_A),
        vs.reshape(1, db, s_len, H_A, DV_A),
        vbs.reshape(1, db, s_len, -1),
        st_p.reshape(1, b, H_C, dk, dv),
        st_s.reshape(1, db, H_C, dk, dv),
    )
```

```python
import functools
import math

import numpy as np
import jax
import jax.numpy as jnp
from jax import lax
from jax.experimental import pallas as pl
from jax.experimental.pallas import tpu as pltpu

F32 = jnp.float32
BF16 = jnp.bfloat16

H_A = 8
DH_A = 64
DV_A = 2 * DH_A
G_B = 8
CHUNK_B = 128
H_C = 4
GLA_TAU = 16.0
N_BUCKETS = 32
MAX_DISTANCE = 128
EPS = 1e-6
MASK_VALUE = -1e30
LOG2E = 1.4426950408889634
SAFE_SCORE_BOUND = 60.0
SAFE_DECAY_BOUND = 60.0

LANES = 128
SUBLANES = 8
VMEM_LIMIT_BYTES = 56 * 1024 * 1024

TOKEN_TILE = 256
STREAM_TILE = 512
ATTN_TILE = TOKEN_TILE
GLA_CHUNK_P = 128
GLA_GROUP_S = 8


def _cparams(n_axes):
    return pltpu.CompilerParams(
        dimension_semantics=("arbitrary",) * n_axes,
        vmem_limit_bytes=VMEM_LIMIT_BYTES,
    )


def _row_tile(t, preferred):
    tm = min(preferred, t)
    while t % tm:
        tm -= CHUNK_B
    assert tm > 0 and t % tm == 0
    return tm


def _dot(a, b):
    return jnp.dot(a, b, preferred_element_type=F32)


def _dot_nt(a, b):
    return lax.dot_general(a, b, (((1,), (1,)), ((), ())), preferred_element_type=F32)


def _dot_tn(a, b):
    return lax.dot_general(a, b, (((0,), (0,)), ((), ())), preferred_element_type=F32)


def _silu(x):
    return x * jax.nn.sigmoid(x)


def _gelu(x):
    return 0.5 * x * (1.0 + lax.erf(x * np.float32(math.sqrt(0.5))))


def _rms_rows(x, g):
    return x * lax.rsqrt(jnp.mean(x * x, axis=-1, keepdims=True) + EPS) * g


def _split_bf16(x):
    hi = x.astype(BF16)
    lo = (x - hi.astype(F32)).astype(BF16)
    return hi, lo


def _proj0_body(x_ref, ng_ref, w_ref, red_ref, exp_ref, qg_ref, kg_ref, lng_ref, lnb_ref, *out_refs,
                width, prompt):
    if prompt:
        q_ref, kb_ref, kt_ref, v_ref, vt_ref, ga_ref, ub_ref, vb_ref = out_refs
    else:
        q_ref, k_ref, v_ref, ga_ref, ub_ref, vb_ref = out_refs
    xb = _rms_rows(x_ref[...], ng_ref[...]).astype(BF16)

    def seg(i):
        return _dot(xb, w_ref[:, i * width:(i + 1) * width])

    def group_rms(z, g):
        ms = _dot((z * z).astype(BF16), red_ref[...])
        hi, lo = _split_bf16(lax.rsqrt(ms + EPS))
        scale = _dot(hi, exp_ref[...]) + _dot(lo, exp_ref[...])
        return z * scale * g

    q_ref[...] = group_rms(seg(0), qg_ref[...]).astype(q_ref.dtype)
    k = group_rms(seg(1), kg_ref[...])
    v = seg(2)
    v_ref[...] = v
    if prompt:
        kb_ref[...] = k.astype(kb_ref.dtype)
        kt_ref[...] = k.T
        vt_ref[...] = v.T.astype(vt_ref.dtype)
    else:
        k_ref[...] = k
    ga_ref[...] = _silu(seg(3)).astype(ga_ref.dtype)
    ub_ref[...] = (_gelu(seg(4)) * _silu(seg(6))).astype(ub_ref.dtype)
    hv = _gelu(seg(5))
    hc = hv - jnp.mean(hv, axis=-1, keepdims=True)
    vb = hc * lax.rsqrt(jnp.mean(hc * hc, axis=-1, keepdims=True) + EPS)
    vb_ref[...] = (vb * lng_ref[...] + lnb_ref[...]).astype(vb_ref.dtype)


def _proj0(x2d, ng, w_bf, red, expd, qg, kg, lng, lnb, *, vb_dtype, prompt_batch=None):
    t, d = x2d.shape
    n_in = w_bf.shape[1]
    width = n_in // 7
    tm = _row_tile(t, TOKEN_TILE)
    row = lambda i: (i, 0)
    full = lambda i: (0, 0)
    tok = lambda dt: jax.ShapeDtypeStruct((t, width), dt)
    row_spec = pl.BlockSpec((tm, width), row)
    prompt = prompt_batch is not None
    if prompt:
        per = t // prompt_batch // tm
        outs = [tok(BF16), tok(BF16), jax.ShapeDtypeStruct((prompt_batch, width, per * tm), F32), tok(F32),
                jax.ShapeDtypeStruct((prompt_batch, per, width, tm), BF16), tok(BF16), tok(BF16), tok(vb_dtype)]
        out_specs = [row_spec, row_spec,
                     pl.BlockSpec((None, width, tm), lambda i: (i // per, 0, i % per)), row_spec,
                     pl.BlockSpec((None, None, width, tm), lambda i: (i // per, i % per, 0, 0)),
                     row_spec, row_spec, row_spec]
    else:
        outs = [tok(BF16), tok(F32), tok(F32), tok(BF16), tok(BF16), tok(vb_dtype)]
        out_specs = [row_spec] * 6
    return pl.pallas_call(
        functools.partial(_proj0_body, width=width, prompt=prompt),
        grid=(t // tm,),
        in_specs=[
            pl.BlockSpec((tm, d), row),
            pl.BlockSpec((1, d), full),
            pl.BlockSpec((d, n_in), full),
            pl.BlockSpec(red.shape, full),
            pl.BlockSpec(expd.shape, full),
            pl.BlockSpec((1, width), full),
            pl.BlockSpec((1, width), full),
            pl.BlockSpec((1, width), full),
            pl.BlockSpec((1, width), full),
        ],
        out_specs=out_specs,
        out_shape=outs,
        compiler_params=_cparams(1),
        name="proj0",
    )(x2d, ng, w_bf, red, expd, qg, kg, lng, lnb)


def _lambda_value(lam_ref, lam_init):
    lf = lam_ref[...]
    s1 = jnp.sum(lf[0:1] * lf[1:2], axis=-1, keepdims=True)
    s2 = jnp.sum(lf[2:3] * lf[3:4], axis=-1, keepdims=True)
    return jnp.exp(s1) - jnp.exp(s2) + lam_init


def _attn_prompt_body(lam_ref, q_ref, k_ref, vt_ref, bias_ref, subg_ref, o_ref, sa_sc, sb_sc, pa_sc, pb_sc,
                      m_sc, l_sc, acc_sc, *, tile, lam_init, stable):
    qi = pl.program_id(2)
    hd = 2 * DH_A
    n_heads = q_ref.shape[1] // hd
    lam = _lambda_value(lam_ref, lam_init)
    q = q_ref[...]
    lane = lax.broadcasted_iota(jnp.int32, (tile, hd), 1)
    q2t = []
    for hh in range(n_heads):
        qh = q[:, hh * hd:(hh + 1) * hd].astype(F32)
        q2 = jnp.concatenate([jnp.where(lane < DH_A, qh, 0.0), jnp.where(lane >= DH_A, qh, 0.0)], axis=0)
        q2t.append(q2.T.astype(BF16))

    def sublane_partial(x):
        return x.reshape(x.shape[0] // SUBLANES, SUBLANES, x.shape[1]).sum(axis=0)

    def scores(t, sbuf):
        j = jnp.maximum(qi - t, 0)
        for hh in range(n_heads):
            sbuf[hh] = _dot(k_ref[j, :, hh * hd:(hh + 1) * hd], q2t[hh])

    def weights(t, sbuf, pbuf):
        kind = jnp.where(t > qi, 3, jnp.minimum(t, 2))
        for hh in range(n_heads):
            s = sbuf[hh] + bias_ref[hh, kind]
            if stable:
                m = m_sc[hh]
                m_new = jnp.maximum(m, jnp.max(s, axis=0, keepdims=True))
                alpha = jnp.exp2(m - m_new)
                m_sc[hh] = m_new
                p = jnp.exp2(s - m_new)
                l_sc[hh] = alpha * l_sc[hh] + sublane_partial(p)
                acc_sc[hh] = alpha * acc_sc[hh]
            else:
                p = jnp.exp2(s)
                l_sc[hh] = l_sc[hh] + sublane_partial(p)
            pbuf[hh] = p.astype(BF16)

    def values(t, pbuf):
        j = jnp.maximum(qi - t, 0)
        for hh in range(n_heads):
            acc_sc[hh] = acc_sc[hh] + _dot(vt_ref[j, hh * DV_A:(hh + 1) * DV_A, :], pbuf[hh])

    m_sc[...] = jnp.full(m_sc.shape, MASK_VALUE, F32)
    l_sc[...] = jnp.zeros(l_sc.shape, F32)
    acc_sc[...] = jnp.zeros(acc_sc.shape, F32)
    n_keys = qi + 1
    if stable:
        def body(t, c):
            scores(t, sa_sc)
            weights(t, sa_sc, pa_sc)
            values(t, pa_sc)
            return c

        lax.fori_loop(0, n_keys, body, 0)
    else:
        n_even = n_keys + n_keys % 2
        scores(0, sa_sc)
        scores(1, sb_sc)
        weights(0, sa_sc, pa_sc)
        scores(2, sa_sc)

        def body(pair, c):
            u = 2 * pair + 1
            weights(u, sb_sc, pb_sc)
            values(u - 1, pa_sc)
            scores(u + 2, sb_sc)
            weights(u + 1, sa_sc, pa_sc)
            values(u, pb_sc)
            scores(u + 3, sa_sc)
            return c

        lax.fori_loop(0, n_even // 2 - 1, body, 0)
        weights(n_even - 1, sb_sc, pb_sc)
        values(n_even - 2, pa_sc)
        values(n_even - 1, pb_sc)

    outs = []
    for hh in range(n_heads):
        l8, acc = l_sc[hh], acc_sc[hh]
        o_all = acc * (1.0 / jnp.sum(l8, axis=0, keepdims=True))
        o = o_all[:, :tile] - lam * o_all[:, tile:]
        on = o * lax.rsqrt(jnp.mean(o * o, axis=0, keepdims=True) + EPS) * subg_ref[...]
        outs.append(on.T)
    o_ref[...] = jnp.concatenate(outs, axis=1).astype(o_ref.dtype)


def _attn_prompt(q, k, vt, lam_p, bias_tiles, subg_cols, *, lam_init, stable):
    b, l, w = q.shape
    tile = ATTN_TILE
    n_tiles = l // tile
    hp = 2
    wb = hp * 2 * DH_A
    return pl.pallas_call(
        functools.partial(_attn_prompt_body, tile=tile, lam_init=lam_init, stable=stable),
        grid=(b, H_A // hp, n_tiles),
        in_specs=[
            pl.BlockSpec(lam_p.shape, lambda bi, h, i: (0, 0)),
            pl.BlockSpec((None, tile, wb), lambda bi, h, i: (bi, i, h)),
            pl.BlockSpec((None, n_tiles, tile, wb), lambda bi, h, i: (bi, 0, 0, h)),
            pl.BlockSpec((None, n_tiles, wb, tile), lambda bi, h, i: (bi, 0, h, 0)),
            pl.BlockSpec((hp, 4, tile, 2 * tile), lambda bi, h, i: (h, 0, 0, 0)),
            pl.BlockSpec((DV_A, tile), lambda bi, h, i: (0, 0)),
        ],
        out_specs=pl.BlockSpec((None, tile, wb), lambda bi, h, i: (bi, i, h)),
        out_shape=jax.ShapeDtypeStruct((b, l, w), BF16),
        scratch_shapes=[
            pltpu.VMEM((hp, tile, 2 * tile), F32),
            pltpu.VMEM((hp, tile, 2 * tile), F32),
            pltpu.VMEM((hp, tile, 2 * tile), BF16),
            pltpu.VMEM((hp, tile, 2 * tile), BF16),
            pltpu.VMEM((hp, 1, 2 * tile), F32),
            pltpu.VMEM((hp, SUBLANES, 2 * tile), F32),
            pltpu.VMEM((hp, DV_A, 2 * tile), F32),
        ],
        compiler_params=_cparams(3),
        name="attn_prompt_stable" if stable else "attn_prompt",
    )(lam_p, q, k, vt, bias_tiles, subg_cols)


def _attn_sample_body(pt_ref, lam_ref, q_ref, kn_ref, vn_ref, mask_ref, bp_ref, bn_ref, subg_ref,
                      ck_hbm, cv_hbm, o_ref, kbuf, vbuf, sem, *, n_pages, page, key_chunk, lam_init):
    b = pl.program_id(0)
    nb = pl.num_programs(0)
    slot = b % 2

    def page_copies(seq, sl):
        cps = []
        for p in range(n_pages):
            pg = pt_ref[seq, p]
            cps.append(pltpu.make_async_copy(ck_hbm.at[pg], kbuf.at[sl, :, pl.ds(p * page, page)],
                                             sem.at[sl, 0]))
            cps.append(pltpu.make_async_copy(cv_hbm.at[pg], vbuf.at[sl, pl.ds(p * page * H_A, page * H_A), :],
                                             sem.at[sl, 1]))
        return cps

    @pl.when(b == 0)
    def _():
        for c in page_copies(0, 0):
            c.start()

    @pl.when(b + 1 < nb)
    def _():
        for c in page_copies(b + 1, 1 - slot):
            c.start()

    lam = _lambda_value(lam_ref, lam_init)
    q = q_ref[...]
    n_rows = mask_ref.shape[0]
    wq = jnp.concatenate([q] * (n_rows // q.shape[0]), axis=0) * mask_ref[...]

    for c in page_copies(b, slot):
        c.wait()

    s_len = q.shape[0]
    n_chunks = (n_pages * page) // key_chunk
    pad_rows = bn_ref.shape[1] - s_len
    kn = jnp.concatenate([kn_ref[...], jnp.zeros((pad_rows, kn_ref.shape[1]), F32)], axis=0).astype(BF16)
    vn = jnp.concatenate([vn_ref[...], jnp.zeros((pad_rows, vn_ref.shape[1]), F32)], axis=0).astype(BF16)
    s_parts = [_dot_nt(wq, kn) + bn_ref[...]]
    for c in range(n_chunks):
        keys = pl.ds(c * key_chunk, key_chunk)
        s_parts.append(_dot(wq, kbuf[slot, :, keys].astype(BF16)) + bp_ref[:, keys])

    m = jnp.max(s_parts[0], axis=1, keepdims=True)
    for s in s_parts[1:]:
        m = jnp.maximum(m, jnp.max(s, axis=1, keepdims=True))
    p_parts = [jnp.exp2(s - m) for s in s_parts]
    l = jnp.sum(p_parts[0], axis=1, keepdims=True)
    for p in p_parts[1:]:
        l = l + jnp.sum(p, axis=1, keepdims=True)
    row = lax.broadcasted_iota(jnp.int32, l.shape, 0)
    half = H_A * s_len
    scale = jnp.where(row < half, 1.0, -lam) / l

    heads = []
    for h in range(H_A):
        r0 = h * s_len

        def head_rows(x):
            return jnp.concatenate([x[r0:r0 + s_len], x[half + r0:half + r0 + s_len]], axis=0)

        acc = _dot(head_rows(p_parts[0]).astype(BF16), vn[:, h * DV_A:(h + 1) * DV_A])
        for c in range(n_chunks):
            vh = vbuf[slot, pl.ds(c * key_chunk * H_A + h, key_chunk, stride=H_A), :].astype(BF16)
            acc = acc + _dot(head_rows(p_parts[c + 1]).astype(BF16), vh)
        acc = acc * head_rows(scale)
        o_h = acc[:s_len] + acc[s_len:]
        heads.append(o_h * lax.rsqrt(jnp.mean(o_h * o_h, axis=-1, keepdims=True) + EPS))
    o_ref[...] = (jnp.concatenate(heads, axis=1) * subg_ref[...]).astype(o_ref.dtype)


def _attn_sample(page_table, q, k_new, v_new, cache_k, cache_v, lam_p, wq_mask, bias_past, bias_new,
                 subg_rows, *, lam_init):
    db, s_len, w = q.shape
    n_pages = page_table.shape[1]
    page = cache_k.shape[2]
    past = n_pages * page
    key_chunk = min(512, past)
    seq3 = lambda b, pt: (b, 0, 0)
    full2 = lambda b, pt: (0, 0)
    grid_spec = pltpu.PrefetchScalarGridSpec(
        num_scalar_prefetch=1,
        grid=(db,),
        in_specs=[
            pl.BlockSpec(lam_p.shape, full2),
            pl.BlockSpec((None, s_len, w), seq3),
            pl.BlockSpec((None, s_len, w), seq3),
            pl.BlockSpec((None, s_len, w), seq3),
            pl.BlockSpec(wq_mask.shape, full2),
            pl.BlockSpec(bias_past.shape, full2),
            pl.BlockSpec(bias_new.shape, full2),
            pl.BlockSpec(subg_rows.shape, full2),
            pl.BlockSpec(memory_space=pl.ANY),
            pl.BlockSpec(memory_space=pl.ANY),
        ],
        out_specs=pl.BlockSpec((None, s_len, w), seq3),
        scratch_shapes=[
            pltpu.VMEM((2, w, past), F32),
            pltpu.VMEM((2, past * H_A, DV_A), F32),
            pltpu.SemaphoreType.DMA((2, 2)),
        ],
    )
    return pl.pallas_call(
        functools.partial(_attn_sample_body, n_pages=n_pages, page=page, key_chunk=key_chunk,
                          lam_init=lam_init),
        grid_spec=grid_spec,
        out_shape=jax.ShapeDtypeStruct((db, s_len, w), BF16),
        compiler_params=_cparams(1),
        name="attn_sample",
    )(page_table, lam_p, q, k_new, v_new, wq_mask, bias_past, bias_new, subg_rows, cache_k, cache_v)


def _mix0_body(x_ref, at_ref, ga_ref, ub_ref, vb_ref, wsp_ref, bsp_ref, wo_ref, y_ref):
    tm = x_ref.shape[0]
    wa = at_ref.shape[1]
    vb = vb_ref[...].astype(BF16)
    n_sub = tm // CHUNK_B
    cg = vb.shape[1] // G_B
    cols = []
    for g in range(G_B):
        lanes = slice(g * cg, (g + 1) * cg)
        rows = [_dot(wsp_ref[g], vb[sb * CHUNK_B:(sb + 1) * CHUNK_B, lanes]) for sb in range(n_sub)]
        cols.append(jnp.concatenate(rows, axis=0) if n_sub > 1 else rows[0])
    bsp = bsp_ref[...]
    bias = jnp.concatenate([bsp] * n_sub, axis=0) if n_sub > 1 else bsp
    mixed = jnp.concatenate(cols, axis=1) + bias
    ob = (ub_ref[...].astype(F32) * mixed).astype(BF16)
    oa = at_ref[...] * ga_ref[...]
    y_ref[...] = x_ref[...] + _dot(oa, wo_ref[:wa, :]) + _dot(ob, wo_ref[wa:, :])


def _mix0(x2d, attn, gas, ub, vb, wsp, bsp, wo_bf):
    t, d = x2d.shape
    tm = _row_tile(t, STREAM_TILE)
    row = lambda i: (i, 0)
    full = lambda i: (0, 0)
    return pl.pallas_call(
        _mix0_body,
        grid=(t // tm,),
        in_specs=[
            pl.BlockSpec((tm, d), row),
            pl.BlockSpec((tm, attn.shape[1]), row),
            pl.BlockSpec((tm, gas.shape[1]), row),
            pl.BlockSpec((tm, ub.shape[1]), row),
            pl.BlockSpec((tm, vb.shape[1]), row),
            pl.BlockSpec(wsp.shape, lambda i: (0, 0, 0)),
            pl.BlockSpec(bsp.shape, full),
            pl.BlockSpec(wo_bf.shape, full),
        ],
        out_specs=pl.BlockSpec((tm, d), row),
        out_shape=jax.ShapeDtypeStruct((t, d), F32),
        compiler_params=_cparams(1),
        name="mix0",
    )(x2d, attn, gas, ub, vb, wsp, bsp, wo_bf)


def _proj1_body(x_ref, ng_ref, w_ref, wg_ref, bg_ref, q_ref, k_ref, v_ref, gs_ref, la_ref, lamin_ref,
                *, hk, wc, q_scale):
    xb = _rms_rows(x_ref[...], ng_ref[...]).astype(BF16)
    q_ref[...] = _dot(xb, w_ref[:, 0:hk]) * q_scale
    k_ref[...] = _dot(xb, w_ref[:, hk:2 * hk])
    v_ref[...] = _dot(xb, w_ref[:, 2 * hk:2 * hk + wc]).astype(v_ref.dtype)
    gs_ref[...] = _silu(_dot(xb, w_ref[:, 2 * hk + wc:2 * hk + 2 * wc])).astype(gs_ref.dtype)
    za = _dot(xb, w_ref[:, 2 * hk + 2 * wc:]).astype(BF16)
    xg = _dot(za, wg_ref[...]) + bg_ref[...]
    la = (jnp.minimum(xg, 0.0) - jnp.log1p(jnp.exp(-jnp.abs(xg)))) * np.float32(1.0 / GLA_TAU)
    la_ref[...] = la
    lamin_ref[...] = jnp.min(la, axis=0, keepdims=True)


def _proj1(x2d, ng, w_bf, wg_bf, bg, *, hk, wc, q_scale):
    t, d = x2d.shape
    tm = _row_tile(t, STREAM_TILE)
    row = lambda i: (i, 0)
    full = lambda i: (0, 0)
    outs = [
        jax.ShapeDtypeStruct((t, hk), F32),
        jax.ShapeDtypeStruct((t, hk), F32),
        jax.ShapeDtypeStruct((t, wc), BF16),
        jax.ShapeDtypeStruct((t, wc), BF16),
        jax.ShapeDtypeStruct((t, hk), F32),
        jax.ShapeDtypeStruct((t // tm, 1, hk), F32),
    ]
    return pl.pallas_call(
        functools.partial(_proj1_body, hk=hk, wc=wc, q_scale=q_scale),
        grid=(t // tm,),
        in_specs=[
            pl.BlockSpec((tm, d), row),
            pl.BlockSpec((1, d), full),
            pl.BlockSpec(w_bf.shape, full),
            pl.BlockSpec(wg_bf.shape, full),
            pl.BlockSpec((1, hk), full),
        ],
        out_specs=[pl.BlockSpec((tm, hk), row), pl.BlockSpec((tm, hk), row), pl.BlockSpec((tm, wc), row),
                   pl.BlockSpec((tm, wc), row), pl.BlockSpec((tm, hk), row),
                   pl.BlockSpec((None, 1, hk), lambda i: (i, 0, 0))],
        out_shape=outs,
        compiler_params=_cparams(1),
        name="proj1",
    )(x2d, ng, w_bf, wg_bf, bg)


def _segment_cumsum(x, seg):
    n = x.shape[0]
    row = lax.broadcasted_iota(jnp.int32, x.shape, 0) % seg
    d = 1
    while d < seg:
        x = x + jnp.where(row >= d, pltpu.roll(x, d, axis=0), 0.0)
        d *= 2
    return x


def _block_reference(bc, hb):
    n, w = bc.shape
    if hb >= SUBLANES:
        pieces = []
        for blk in range(n // (2 * hb)):
            r = blk * 2 * hb + hb - 1
            pieces.append(jnp.broadcast_to(bc[r:r + 1, :], (2 * hb, w)))
        return jnp.concatenate(pieces, axis=0) if len(pieces) > 1 else pieces[0]
    pos = lax.broadcasted_iota(jnp.int32, bc.shape, 0) % (2 * hb)
    out = bc
    for delta in range(-hb, hb):
        if delta == 0:
            continue
        shifted = pltpu.roll(bc, (-delta) % n, axis=0)
        out = jnp.where(pos == hb - 1 - delta, shifted, out)
    return out


def _gla_intra_scores(q, k, bc, seg, bounded):
    n = q.shape[0]
    ri = lax.broadcasted_iota(jnp.int32, (n, n), 0)
    ci = lax.broadcasted_iota(jnp.int32, (n, n), 1)
    if bounded:
        att = _dot_nt((q * jnp.exp(bc)).astype(BF16), (k * jnp.exp(-bc)).astype(BF16))
        att = jnp.where(ri >= ci, att, 0.0)
        if seg < n:
            att = jnp.where((ri // seg) == (ci // seg), att, 0.0)
        return att
    rowpos = lax.broadcasted_iota(jnp.int32, q.shape, 0)
    att = jnp.where(ri == ci, _dot_nt(q.astype(BF16), k.astype(BF16)), 0.0)
    for hb in _levels(seg):
        ref = _block_reference(bc, hb)
        is_q = (rowpos % (2 * hb)) >= hb
        f = jnp.exp(jnp.where(is_q, bc - ref, ref - bc))
        qt = jnp.where(is_q, q * f, 0.0).astype(BF16)
        kt = jnp.where(is_q, 0.0, k * f).astype(BF16)
        same = (ri // (2 * hb)) == (ci // (2 * hb))
        att = att + jnp.where(same, _dot_nt(qt, kt), 0.0)
    return att


def _levels(seg):
    out, hb = [], seg // 2
    while hb >= 1:
        out.append(hb)
        hb //= 2
    return out


def _rows_scale_matrix(row, n_cols):
    hi = row.astype(BF16).astype(F32)
    r1 = row - hi
    mid = r1.astype(BF16).astype(F32)
    lo = (r1 - mid).astype(BF16).astype(F32)
    pad = jnp.zeros((2 * SUBLANES - 3, row.shape[1]), F32)
    parts = jnp.concatenate([hi, mid, lo, pad], axis=0).astype(BF16)
    return _dot_tn(parts, jnp.ones((2 * SUBLANES, n_cols), BF16))


def _gla_prompt_body(q_ref, k_ref, v_ref, la_ref, og_ref, s0_ref, o_ref, sout_ref, st_sc, *, dk, dv,
                     bounded):
    ci = pl.program_id(1)
    chunk = q_ref.shape[0]

    @pl.when(ci == 0)
    def _():
        st_sc[...] = s0_ref[...]

    for h in range(H_C):
        kl = slice(h * dk, (h + 1) * dk)
        vl = slice(h * dv, (h + 1) * dv)
        q, k, la = q_ref[:, kl], k_ref[:, kl], la_ref[:, kl]
        v = v_ref[:, vl]
        bc = _segment_cumsum(la, chunk)
        st = st_sc[h]
        att = _gla_intra_scores(q, k, bc, chunk, bounded)
        o = _dot((q * jnp.exp(bc)).astype(BF16), st.astype(BF16)) + _dot(att.astype(BF16), v)
        b_last = bc[chunk - 1:chunk, :]
        k_dec = (k * jnp.exp(b_last - bc)).astype(BF16)
        st_sc[h] = st * _rows_scale_matrix(jnp.exp(b_last), dv) + _dot_tn(k_dec, v)
        o_ref[:, vl] = _rms_rows(o, og_ref[...]).astype(o_ref.dtype)

    @pl.when(ci == pl.num_programs(1) - 1)
    def _():
        sout_ref[...] = st_sc[...]


def _gla_prompt(q, k, v, la, og, s0, *, bounded):
    b, l, hk = q.shape
    wc = v.shape[2]
    dk, dv = hk // H_C, wc // H_C
    chunk = min(GLA_CHUNK_P, l)
    tok = lambda bi, ci: (bi, ci, 0)
    st = lambda bi, ci: (bi, 0, 0, 0)
    return pl.pallas_call(
        functools.partial(_gla_prompt_body, dk=dk, dv=dv, bounded=bounded),
        grid=(b, l // chunk),
        in_specs=[
            pl.BlockSpec((None, chunk, hk), tok),
            pl.BlockSpec((None, chunk, hk), tok),
            pl.BlockSpec((None, chunk, wc), tok),
            pl.BlockSpec((None, chunk, hk), tok),
            pl.BlockSpec((1, dv), lambda bi, ci: (0, 0)),
            pl.BlockSpec((None, H_C, dk, dv), st),
        ],
        out_specs=[pl.BlockSpec((None, chunk, wc), tok), pl.BlockSpec((None, H_C, dk, dv), st)],
        out_shape=[jax.ShapeDtypeStruct((b, l, wc), BF16), jax.ShapeDtypeStruct((b, H_C, dk, dv), F32)],
        scratch_shapes=[pltpu.VMEM((H_C, dk, dv), F32)],
        compiler_params=_cparams(2),
        name="gla_prompt" if bounded else "gla_prompt_any_decay",
    )(q, k, v, la, og, s0)


def _gla_sample_body(q_ref, k_ref, v_ref, la_ref, og_ref, s0_ref, o_ref, sout_ref, *, dk, dv, seg, bounded):
    n_seq = s0_ref.shape[0]
    for h in range(H_C):
        kl = slice(h * dk, (h + 1) * dk)
        vl = slice(h * dv, (h + 1) * dv)
        q, k, la = q_ref[:, kl], k_ref[:, kl], la_ref[:, kl]
        v = v_ref[:, vl]
        bc = _segment_cumsum(la, seg)
        att = _gla_intra_scores(q, k, bc, seg, bounded)
        o_intra = _dot(att.astype(BF16), v)
        q_dec = (q * jnp.exp(bc)).astype(BF16)
        outs = []
        for s in range(n_seq):
            rows = slice(s * seg, (s + 1) * seg)
            st = s0_ref[s, h]
            bl = bc[s * seg + seg - 1:s * seg + seg, :]
            k_dec = (k[rows] * jnp.exp(bl - bc[rows])).astype(BF16)
            outs.append(_dot(q_dec[rows], st.astype(BF16)))
            sout_ref[s, h] = st * _rows_scale_matrix(jnp.exp(bl), dv) + _dot_tn(k_dec, v[rows])
        o = jnp.concatenate(outs, axis=0) + o_intra
        o_ref[:, vl] = _rms_rows(o, og_ref[...]).astype(o_ref.dtype)


def _gla_sample(q, k, v, la, og, s0, *, seg, bounded):
    t, hk = q.shape
    wc = v.shape[1]
    dk, dv = hk // H_C, wc // H_C
    db = t // seg
    grp = min(GLA_GROUP_S, db)
    rows = grp * seg
    tok = lambda i: (i, 0)
    st = lambda i: (i, 0, 0, 0)
    return pl.pallas_call(
        functools.partial(_gla_sample_body, dk=dk, dv=dv, seg=seg, bounded=bounded),
        grid=(db // grp,),
        in_specs=[
            pl.BlockSpec((rows, hk), tok),
            pl.BlockSpec((rows, hk), tok),
            pl.BlockSpec((rows, wc), tok),
            pl.BlockSpec((rows, hk), tok),
            pl.BlockSpec((1, dv), lambda i: (0, 0)),
            pl.BlockSpec((grp, H_C, dk, dv), st),
        ],
        out_specs=[pl.BlockSpec((rows, wc), tok), pl.BlockSpec((grp, H_C, dk, dv), st)],
        out_shape=[jax.ShapeDtypeStruct((t, wc), BF16), jax.ShapeDtypeStruct((db, H_C, dk, dv), F32)],
        compiler_params=_cparams(1),
        name="gla_sample" if bounded else "gla_sample_any_decay",
    )(q, k, v, la, og, s0)


def _out1_body(x_ref, o_ref, gs_ref, w_ref, y_ref):
    y_ref[...] = x_ref[...] + _dot(o_ref[...] * gs_ref[...], w_ref[...])


def _out1(x2d, o, gs, w_bf):
    t, d = x2d.shape
    tm = _row_tile(t, STREAM_TILE)
    row = lambda i: (i, 0)
    return pl.pallas_call(
        _out1_body,
        grid=(t // tm,),
        in_specs=[pl.BlockSpec((tm, d), row), pl.BlockSpec((tm, o.shape[1]), row),
                  pl.BlockSpec((tm, gs.shape[1]), row), pl.BlockSpec(w_bf.shape, lambda i: (0, 0))],
        out_specs=pl.BlockSpec((tm, d), row),
        out_shape=jax.ShapeDtypeStruct((t, d), F32),
        compiler_params=_cparams(1),
        name="out1",
    )(x2d, o, gs, w_bf)


def _t5_bucket_np(dist):
    n = np.maximum(dist, 0)
    max_exact = N_BUCKETS // 2
    nf = np.maximum(n, 1).astype(np.float32)
    ratio = np.log(nf / np.float32(max_exact)) / np.float32(math.log(MAX_DISTANCE / max_exact))
    large = max_exact + (ratio * np.float32(N_BUCKETS - max_exact)).astype(np.int32)
    large = np.minimum(large, N_BUCKETS - 1)
    return np.where(n < max_exact, n, large)


def _bias_by_distance(rel_bias, n):
    buckets = _t5_bucket_np(np.arange(n))
    assert np.all(np.diff(buckets) >= 0)
    cuts = [0] + [int(i) + 1 for i in np.nonzero(np.diff(buckets))[0]] + [n]
    pieces = [jnp.broadcast_to(rel_bias[int(buckets[a])][:, None], (rel_bias.shape[1], e - a))
              for a, e in zip(cuts[:-1], cuts[1:])]
    return jnp.concatenate(pieces, axis=1).astype(F32) * np.float32(LOG2E)


def _toeplitz(w, n):
    h = w.shape[0]
    flat = jnp.tile(w, (1, n))[:, :n * (2 * n - 1)]
    return flat.reshape(h, n, 2 * n - 1)[:, :, n - 1:]


def _prompt_bias_tiles(rel_bias, tile):
    assert int(_t5_bucket_np(np.array([tile + 1]))[0]) == N_BUCKETS - 1
    tab = _bias_by_distance(rel_bias, 2 * tile)
    tab = tab - tab[:, 2 * tile - 1:]
    h = tab.shape[0]
    masked = jnp.full((h, tile - 1), MASK_VALUE, F32)
    pad = jnp.zeros((h, 1), F32)
    w0 = jnp.concatenate([masked, tab[:, :tile], pad], axis=1)
    w1 = jnp.concatenate([tab[:, 1:], pad], axis=1)
    tiles = jnp.stack([_toeplitz(w0, tile), _toeplitz(w1, tile), jnp.zeros((h, tile, tile), F32),
                       jnp.full((h, tile, tile), MASK_VALUE, F32)], axis=1)
    return jnp.concatenate([tiles, tiles], axis=-1)


def _sample_bias(rel_bias, past, s_len):
    n = past + s_len
    rev = _bias_by_distance(rel_bias, n)[:, ::-1]
    h = rev.shape[0]
    bp = jnp.stack([rev[:, s_len - 1 - q:s_len - 1 - q + past] for q in range(s_len)], axis=1)
    small = jnp.concatenate([rev[:, past:], jnp.full((h, LANES), MASK_VALUE, F32)], axis=1)
    bn = jnp.stack([small[:, s_len - 1 - q:s_len - 1 - q + LANES] for q in range(s_len)], axis=1)
    bp = bp.reshape(h * s_len, past)
    bn = bn.reshape(h * s_len, LANES)
    return jnp.concatenate([bp, bp], axis=0), jnp.concatenate([bn, bn], axis=0)


def _sample_query_mask(s_len):
    rows = np.arange(2 * H_A * s_len)
    c = rows // (H_A * s_len)
    h = (rows // s_len) % H_A
    lane_grp = np.arange(H_A * 2 * DH_A) // DH_A
    return jnp.asarray(lane_grp[None, :] == (h * 2 + c)[:, None], dtype=BF16)


def _spatial_weights(sp_w, sp_b, seg, cg):
    g, n, _ = sp_w.shape
    w = sp_w * jnp.tril(jnp.ones((n, n), sp_w.dtype))
    reps = n // seg
    if reps > 1:
        eye = jnp.eye(reps, dtype=sp_w.dtype)
        w = jnp.einsum("ab,gts->gatbs", eye, w[:, :seg, :seg]).reshape(g, n, n)
    bias = jnp.tile(sp_b[:, :seg], (1, reps))
    bias = jnp.repeat(jnp.transpose(bias), cg, axis=1)
    return w.astype(BF16), bias.astype(F32)


def kernel(x_prompt, x_sample, cache_k, cache_v, state_gla, page_table, rel_bias, norm0_g, w_in0, q_norm_g,
           k_norm_g, lam, subln_g, ln_v_g, ln_v_b, spatial_w, spatial_b, w_out0, norm1_g, w_in1, w_gate,
           b_gate, gla_norm_g, w_out1):
    b, l, d = x_prompt.shape
    db, s_len, _ = x_sample.shape
    n_pool, page = cache_k.shape[1], cache_k.shape[2]
    past = page_table.shape[1] * page
    wa = H_A * DV_A
    lam_init = 0.8 - 0.6 * math.exp(-0.3 * 0)
    rank = w_gate.shape[1]
    hk = w_gate.shape[2]
    wc = w_out1.shape[1]
    dk, dv = hk // H_C, wc // H_C

    w0 = w_in0[0].astype(BF16)
    wo0 = w_out0[0].astype(BF16)
    grp = np.arange(wa) // DH_A
    red = jnp.asarray((grp[:, None] == np.arange(LANES)[None, :]) / DH_A, dtype=BF16)
    expd = jnp.asarray(np.arange(LANES)[:, None] == grp[None, :], dtype=BF16)
    qg = (jnp.tile(q_norm_g[0].reshape(-1), H_A) * np.float32(DH_A ** -0.5 * LOG2E)).reshape(1, wa).astype(F32)
    kg = jnp.tile(k_norm_g[0].reshape(-1), H_A).reshape(1, wa).astype(F32)
    ng0 = norm0_g[0].reshape(1, d)
    lng = ln_v_g[0].reshape(1, -1)
    lnb = ln_v_b[0].reshape(1, -1)
    sub_scaled = subln_g[0] * np.float32(1.0 - lam_init)
    subg_cols = jnp.broadcast_to(sub_scaled[:, None], (DV_A, ATTN_TILE)).astype(F32)
    subg_rows = jnp.tile(sub_scaled, H_A).reshape(1, wa).astype(F32)
    bias_tiles = _prompt_bias_tiles(rel_bias, ATTN_TILE)
    bias_past, bias_new = _sample_bias(rel_bias, past, s_len)
    wq_mask = _sample_query_mask(s_len)
    cg = ln_v_g.shape[1] // G_B
    wsp_p, bsp_p = _spatial_weights(spatial_w[0], spatial_b[0], CHUNK_B, cg)
    wsp_s, bsp_s = _spatial_weights(spatial_w[0], spatial_b[0], s_len, cg)

    w1 = w_in1[0]
    w1_bf = jnp.pad(w1, ((0, 0), (0, LANES - rank))).astype(BF16)
    wg = jnp.pad(w_gate[0], ((0, LANES - rank), (0, 0))).astype(BF16)
    bg = b_gate[0].reshape(1, hk)
    ng1 = norm1_g[0].reshape(1, d)
    og = gla_norm_g[0].reshape(1, dv)
    wo1 = w_out1[0].astype(BF16)

    xp = x_prompt.reshape(b * l, d)
    xs = x_sample.reshape(db * s_len, d)
    ck_t = jnp.transpose(cache_k[0], (0, 2, 3, 4, 1)).reshape(n_pool, wa, page)
    cv_rows = cache_v[0].reshape(n_pool, page * H_A, DV_A)

    qp, kbp, ktp, vp, vtp, gap, ubp, vbp = _proj0(xp, ng0, w0, red, expd, qg, kg, lng, lnb, vb_dtype=BF16,
                                                  prompt_batch=b)
    qs, ks, vs, gas, ubs, vbs = _proj0(xs, ng0, w0, red, expd, qg, kg, lng, lnb, vb_dtype=F32)

    score_bound = (np.float32(1.02 * math.sqrt(DH_A) * LOG2E)
                   * jnp.max(jnp.abs(q_norm_g[0])) * jnp.max(jnp.abs(k_norm_g[0]))
                   + np.float32(LOG2E) * jnp.max(jnp.abs(rel_bias - rel_bias[N_BUCKETS - 1])))
    attn_args = (qp.reshape(b, l, wa), kbp.reshape(b, l // ATTN_TILE, ATTN_TILE, wa), vtp, lam[0],
                 bias_tiles, subg_cols)
    at_p = lax.cond(score_bound <= SAFE_SCORE_BOUND,
                    lambda args: _attn_prompt(*args, lam_init=lam_init, stable=False),
                    lambda args: _attn_prompt(*args, lam_init=lam_init, stable=True),
                    attn_args)
    at_s = _attn_sample(page_table, qs.reshape(db, s_len, wa), ks.reshape(db, s_len, wa),
                        vs.reshape(db, s_len, wa), ck_t, cv_rows, lam[0], wq_mask, bias_past, bias_new,
                        subg_rows, lam_init=lam_init)

    yp0 = _mix0(xp, at_p.reshape(b * l, wa), gap, ubp, vbp, wsp_p, bsp_p, wo0)
    ys0 = _mix0(xs, at_s.reshape(db * s_len, wa), gas, ubs, vbs, wsp_s, bsp_s, wo0)

    q_scale = np.float32(dk ** -0.5)
    q1p, k1p, v1p, gsp, lap, lamin_p = _proj1(yp0, ng1, w1_bf, wg, bg, hk=hk, wc=wc, q_scale=q_scale)
    q1s, k1s, v1s, gss, las, lamin_s = _proj1(ys0, ng1, w1_bf, wg, bg, hk=hk, wc=wc, q_scale=q_scale)

    gla_p_args = (q1p.reshape(b, l, hk), k1p.reshape(b, l, hk), v1p.reshape(b, l, wc),
                  lap.reshape(b, l, hk), og, jnp.zeros((b, H_C, dk, dv), F32))
    o_p, st_p = lax.cond(-jnp.min(lamin_p) * min(GLA_CHUNK_P, l) <= SAFE_DECAY_BOUND,
                         lambda a: _gla_prompt(*a, bounded=True),
                         lambda a: _gla_prompt(*a, bounded=False), gla_p_args)
    gla_s_args = (q1s, k1s, v1s, las, og, state_gla[0])
    o_s, st_s = lax.cond(-jnp.min(lamin_s) * s_len <= SAFE_DECAY_BOUND,
                         lambda a: _gla_sample(*a, seg=s_len, bounded=True),
                         lambda a: _gla_sample(*a, seg=s_len, bounded=False), gla_s_args)

    yp1 = _out1(yp0, o_p.reshape(b * l, wc), gsp, wo1)
    ys1 = _out1(ys0, o_s, gss, wo1)

    return (
        yp1.reshape(b, l, d),
        ys1.reshape(db, s_len, d),
        jnp.transpose(ktp.reshape(b, H_A, 2, DH_A, l), (0, 4, 1, 2, 3))[None],
        vp.reshape(1, b, l, H_A, DV_A),
        ks.reshape(1, db, s_len, H_A, 2, DH_A),
        vs.reshape(1, db, s_len, H_A, DV_A),
        vbs.reshape(1, db, s_len, -1),
        st_p.reshape(1, b, H_C, dk, dv),
        st_s.reshape(1, db, H_C, dk, dv),
    )
```

```python
import functools
import math

import numpy as np
import jax
import jax.numpy as jnp
from jax import lax
from jax.experimental import pallas as pl
from jax.experimental.pallas import tpu as pltpu

F32 = jnp.float32
BF16 = jnp.bfloat16

H_A = 8
DH_A = 64
DV_A = 2 * DH_A
G_B = 8
CHUNK_B = 128
H_C = 4
GLA_TAU = 16.0
N_BUCKETS = 32
MAX_DISTANCE = 128
EPS = 1e-6
MASK_VALUE = -1e30
LOG2E = 1.4426950408889634
SAFE_SCORE_BOUND = 60.0
SAFE_DECAY_BOUND = 60.0

LANES = 128
SUBLANES = 8
VMEM_LIMIT_BYTES = 56 * 1024 * 1024

TOKEN_TILE = 256
STREAM_TILE = 512
ATTN_TILE = TOKEN_TILE
GLA_CHUNK_P = 128
GLA_GROUP_S = 8


def _cparams(n_axes):
    return pltpu.CompilerParams(
        dimension_semantics=("arbitrary",) * n_axes,
        vmem_limit_bytes=VMEM_LIMIT_BYTES,
    )


def _row_tile(t, preferred):
    tm = min(preferred, t)
    while t % tm:
        tm -= CHUNK_B
    assert tm > 0 and t % tm == 0
    return tm


def _dot(a, b):
    return jnp.dot(a, b, preferred_element_type=F32)


def _dot_nt(a, b):
    return lax.dot_general(a, b, (((1,), (1,)), ((), ())), preferred_element_type=F32)


def _dot_tn(a, b):
    return lax.dot_general(a, b, (((0,), (0,)), ((), ())), preferred_element_type=F32)


def _silu(x):
    return x * jax.nn.sigmoid(x)


def _gelu(x):
    return 0.5 * x * (1.0 + lax.erf(x * np.float32(math.sqrt(0.5))))


def _rms_rows(x, g):
    return x * lax.rsqrt(jnp.mean(x * x, axis=-1, keepdims=True) + EPS) * g


def _split_bf16(x):
    hi = x.astype(BF16)
    lo = (x - hi.astype(F32)).astype(BF16)
    return hi, lo


def _proj0_body(x_ref, ng_ref, w_ref, red_ref, exp_ref, qg_ref, kg_ref, lng_ref, lnb_ref, *out_refs,
                width, prompt):
    if prompt:
        q_ref, kb_ref, kt_ref, v_ref, vt_ref, ga_ref, ub_ref, vb_ref = out_refs
    else:
        q_ref, k_ref, v_ref, ga_ref, ub_ref, vb_ref = out_refs
    xb = _rms_rows(x_ref[...], ng_ref[...]).astype(BF16)

    def seg(i):
        return _dot(xb, w_ref[:, i * width:(i + 1) * width])

    def group_rms(z, g):
        ms = _dot((z * z).astype(BF16), red_ref[...])
        hi, lo = _split_bf16(lax.rsqrt(ms + EPS))
        scale = _dot(hi, exp_ref[...]) + _dot(lo, exp_ref[...])
        return z * scale * g

    q_ref[...] = group_rms(seg(0), qg_ref[...]).astype(q_ref.dtype)
    k = group_rms(seg(1), kg_ref[...])
    v = seg(2)
    v_ref[...] = v
    if prompt:
        kb_ref[...] = k.astype(kb_ref.dtype)
        kt_ref[...] = k.T
        vt_ref[...] = v.T.astype(vt_ref.dtype)
    else:
        k_ref[...] = k
    ga_ref[...] = _silu(seg(3)).astype(ga_ref.dtype)
    ub_ref[...] = (_gelu(seg(4)) * _silu(seg(6))).astype(ub_ref.dtype)
    hv = _gelu(seg(5))
    hc = hv - jnp.mean(hv, axis=-1, keepdims=True)
    vb = hc * lax.rsqrt(jnp.mean(hc * hc, axis=-1, keepdims=True) + EPS)
    vb_ref[...] = (vb * lng_ref[...] + lnb_ref[...]).astype(vb_ref.dtype)


def _proj0(x2d, ng, w_bf, red, expd, qg, kg, lng, lnb, *, vb_dtype, prompt_batch=None):
    t, d = x2d.shape
    n_in = w_bf.shape[1]
    width = n_in // 7
    tm = _row_tile(t, TOKEN_TILE)
    row = lambda i: (i, 0)
    full = lambda i: (0, 0)
    tok = lambda dt: jax.ShapeDtypeStruct((t, width), dt)
    row_spec = pl.BlockSpec((tm, width), row)
    prompt = prompt_batch is not None
    if prompt:
        per = t // prompt_batch // tm
        outs = [tok(BF16), tok(BF16), jax.ShapeDtypeStruct((prompt_batch, width, per * tm), F32), tok(F32),
                jax.ShapeDtypeStruct((prompt_batch, per, width, tm), BF16), tok(BF16), tok(BF16), tok(vb_dtype)]
        out_specs = [row_spec, row_spec,
                     pl.BlockSpec((None, width, tm), lambda i: (i // per, 0, i % per)), row_spec,
                     pl.BlockSpec((None, None, width, tm), lambda i: (i // per, i % per, 0, 0)),
                     row_spec, row_spec, row_spec]
    else:
        outs = [tok(BF16), tok(F32), tok(F32), tok(BF16), tok(BF16), tok(vb_dtype)]
        out_specs = [row_spec] * 6
    return pl.pallas_call(
        functools.partial(_proj0_body, width=width, prompt=prompt),
        grid=(t // tm,),
        in_specs=[
            pl.BlockSpec((tm, d), row),
            pl.BlockSpec((1, d), full),
            pl.BlockSpec((d, n_in), full),
            pl.BlockSpec(red.shape, full),
            pl.BlockSpec(expd.shape, full),
            pl.BlockSpec((1, width), full),
            pl.BlockSpec((1, width), full),
            pl.BlockSpec((1, width), full),
            pl.BlockSpec((1, width), full),
        ],
        out_specs=out_specs,
        out_shape=outs,
        compiler_params=_cparams(1),
        name="proj0",
    )(x2d, ng, w_bf, red, expd, qg, kg, lng, lnb)


def _lambda_value(lam_ref, lam_init):
    lf = lam_ref[...]
    s1 = jnp.sum(lf[0:1] * lf[1:2], axis=-1, keepdims=True)
    s2 = jnp.sum(lf[2:3] * lf[3:4], axis=-1, keepdims=True)
    return jnp.exp(s1) - jnp.exp(s2) + lam_init


def _prompt_attention_step(qi, lam, q_ref, k_ref, vt_ref, bias_ref, subg_ref, o_ref, sa_sc, sb_sc, m_sc, l_sc,
                           acc_sc, *, tile, stable):
    hd = 2 * DH_A
    n_heads = q_ref.shape[1] // hd
    q = q_ref[...]
    lane = lax.broadcasted_iota(jnp.int32, (tile, hd), 1)
    q2t = []
    for hh in range(n_heads):
        qh = q[:, hh * hd:(hh + 1) * hd].astype(F32)
        q2 = jnp.concatenate([jnp.where(lane < DH_A, qh, 0.0), jnp.where(lane >= DH_A, qh, 0.0)], axis=0)
        q2t.append(q2.T.astype(BF16))

    def sublane_partial(x):
        return x.reshape(x.shape[0] // SUBLANES, SUBLANES, x.shape[1]).sum(axis=0)

    def scores(t, sbuf):
        j = qi - t
        for hh in range(n_heads):
            sbuf[hh] = _dot(k_ref[j, :, hh * hd:(hh + 1) * hd], q2t[hh])

    def consume(t, sbuf):
        j = qi - t
        kind = jnp.minimum(t, 2)
        ps = []
        for hh in range(n_heads):
            s = sbuf[hh] + bias_ref[hh, kind]
            if stable:
                m = m_sc[hh]
                m_new = jnp.maximum(m, jnp.max(s, axis=0, keepdims=True))
                alpha = jnp.exp2(m - m_new)
                m_sc[hh] = m_new
                p = jnp.exp2(s - m_new)
                l_sc[hh] = alpha * l_sc[hh] + sublane_partial(p)
                acc_sc[hh] = alpha * acc_sc[hh]
            else:
                p = jnp.exp2(s)
                l_sc[hh] = l_sc[hh] + sublane_partial(p)
            ps.append(p.astype(BF16))
        for hh in range(n_heads):
            acc_sc[hh] = acc_sc[hh] + _dot(vt_ref[j, hh * DV_A:(hh + 1) * DV_A, :], ps[hh])

    m_sc[...] = jnp.full(m_sc.shape, MASK_VALUE, F32)
    l_sc[...] = jnp.zeros(l_sc.shape, F32)
    acc_sc[...] = jnp.zeros(acc_sc.shape, F32)
    n_keys = qi + 1
    n_pairs = (n_keys - 1) // 2
    scores(0, sa_sc)

    def body(pair, c):
        t = 2 * pair
        scores(t + 1, sb_sc)
        consume(t, sa_sc)
        scores(t + 2, sa_sc)
        consume(t + 1, sb_sc)
        return c

    lax.fori_loop(0, n_pairs, body, 0)
    t_last = 2 * n_pairs

    @pl.when(n_keys - t_last == 1)
    def _():
        consume(t_last, sa_sc)

    @pl.when(n_keys - t_last == 2)
    def _():
        scores(t_last + 1, sb_sc)
        consume(t_last, sa_sc)
        consume(t_last + 1, sb_sc)

    outs = []
    for hh in range(n_heads):
        l8, acc = l_sc[hh], acc_sc[hh]
        o_all = acc * (1.0 / jnp.sum(l8, axis=0, keepdims=True))
        o = o_all[:, :tile] - lam * o_all[:, tile:]
        on = o * lax.rsqrt(jnp.mean(o * o, axis=0, keepdims=True) + EPS) * subg_ref[...]
        outs.append(on.T)
    o_ref[...] = jnp.concatenate(outs, axis=1).astype(o_ref.dtype)


def _sample_attention_item(half, slot, sb, lam, qs_ref, kn_ref, vn_ref, mask_ref, bp_ref, bn_ref, subg_ref, os_ref,
                           kbuf, vbuf, sm_sc, sl_sc, sacc_sc, *, key_chunk):
    q = qs_ref[sb]
    s_len = q.shape[0]
    n_rows = mask_ref.shape[0]
    half_rows = H_A * s_len
    wq = jnp.concatenate([q] * (n_rows // s_len), axis=0) * mask_ref[...]
    first = half == 0
    n_chunks = kbuf.shape[2] // key_chunk

    pad_rows = bn_ref.shape[1] - s_len
    kn = jnp.concatenate([kn_ref[sb], jnp.zeros((pad_rows, kn_ref.shape[2]), F32)], axis=0).astype(BF16)
    vn = jnp.concatenate([vn_ref[sb], jnp.zeros((pad_rows, vn_ref.shape[2]), F32)], axis=0).astype(BF16)
    s_parts = [_dot_nt(wq, kn) + bn_ref[...] + jnp.where(first, 0.0, MASK_VALUE).astype(F32)]
    for c in range(n_chunks):
        keys = pl.ds(c * key_chunk, key_chunk)
        s_parts.append(_dot(wq, kbuf[slot, :, keys].astype(BF16)) + bp_ref[half, :, keys])

    m_old = jnp.where(first, MASK_VALUE, sm_sc[sb])
    m = m_old
    for s in s_parts:
        m = jnp.maximum(m, jnp.max(s, axis=1, keepdims=True))
    alpha = jnp.exp2(m_old - m)
    p_parts = [jnp.exp2(s - m) for s in s_parts]
    l = alpha * jnp.where(first, 0.0, sl_sc[sb])
    for p in p_parts:
        l = l + jnp.sum(p, axis=1, keepdims=True)
    sm_sc[sb] = m
    sl_sc[sb] = l
    row = lax.broadcasted_iota(jnp.int32, l.shape, 0)
    scale = jnp.where(row < half_rows, 1.0, -lam) / l

    heads = []
    for h in range(H_A):
        r0 = h * s_len

        def head_rows(x):
            return jnp.concatenate([x[r0:r0 + s_len], x[half_rows + r0:half_rows + r0 + s_len]], axis=0)

        acc = head_rows(alpha) * jnp.where(first, 0.0, sacc_sc[sb, h])
        acc = acc + _dot(head_rows(p_parts[0]).astype(BF16), vn[:, h * DV_A:(h + 1) * DV_A])
        for c in range(n_chunks):
            vh = vbuf[slot, pl.ds(c * key_chunk * H_A + h, key_chunk, stride=H_A), :].astype(BF16)
            acc = acc + _dot(head_rows(p_parts[c + 1]).astype(BF16), vh)
        sacc_sc[sb, h] = acc
        w = acc * head_rows(scale)
        o_h = w[:s_len] + w[s_len:]
        heads.append(o_h * lax.rsqrt(jnp.mean(o_h * o_h, axis=-1, keepdims=True) + EPS))
    os_ref[sb] = (jnp.concatenate(heads, axis=1) * subg_ref[...]).astype(os_ref.dtype)


def _attention_body(pt_ref, lam_ref, q_ref, k_ref, vt_ref, bias_ref, subg_ref,
                    qs_ref, kn_ref, vn_ref, mask_ref, bp_ref, bn_ref, subgr_ref, ck_hbm, cv_hbm,
                    o_ref, os_ref,
                    sa_sc, sb_sc, m_sc, l_sc, acc_sc, kbuf, vbuf, sem, sm_sc, sl_sc, sacc_sc,
                    *, tile, lam_init, stable, items_per_step, pages_per_item, page, key_chunk):
    bi, hp, qi = pl.program_id(0), pl.program_id(1), pl.program_id(2)
    step = (bi * pl.num_programs(1) + hp) * pl.num_programs(2) + qi
    n_steps = pl.num_programs(0) * pl.num_programs(1) * pl.num_programs(2)
    ipg = items_per_step

    def item_copies(item, slot):
        seq, half = item // 2, item % 2
        cps = []
        for p in range(pages_per_item):
            pg = pt_ref[seq, half * pages_per_item + p]
            cps.append(pltpu.make_async_copy(ck_hbm.at[pg], kbuf.at[slot, :, pl.ds(p * page, page)],
                                             sem.at[slot, 0]))
            cps.append(pltpu.make_async_copy(cv_hbm.at[pg], vbuf.at[slot, pl.ds(p * page * H_A, page * H_A), :],
                                             sem.at[slot, 1]))
        return cps

    @pl.when(step == 0)
    def _():
        for r in range(ipg):
            for c in item_copies(r, r):
                c.start()

    @pl.when(step + 1 < n_steps)
    def _():
        for r in range(ipg):
            for c in item_copies((step + 1) * ipg + r, ((step + 1) % 2) * ipg + r):
                c.start()

    lam = _lambda_value(lam_ref, lam_init)
    for r in range(ipg):
        item = step * ipg + r
        slot = (step % 2) * ipg + r
        for c in item_copies(item, slot):
            c.wait()
        if ipg == 1:
            half, sb = item % 2, 0
        else:
            half, sb = jnp.int32(r % 2), r // 2
        _sample_attention_item(half, slot, sb, lam, qs_ref, kn_ref, vn_ref, mask_ref, bp_ref, bn_ref, subgr_ref,
                               os_ref, kbuf, vbuf, sm_sc, sl_sc, sacc_sc, key_chunk=key_chunk)

    _prompt_attention_step(qi, lam, q_ref, k_ref, vt_ref, bias_ref, subg_ref, o_ref, sa_sc, sb_sc, m_sc, l_sc,
                           acc_sc, tile=tile, stable=stable)


def _attention(page_table, q, k, vt, lam_p, bias_tiles, subg_cols, qs, k_new, v_new, cache_k, cache_v, wq_mask,
               bias_past, bias_new, subg_rows, *, lam_init, stable):
    b, l, w = q.shape
    db, s_len, _ = qs.shape
    tile = ATTN_TILE
    n_tiles = l // tile
    hp = 2
    wb = hp * 2 * DH_A
    nhp = H_A // hp
    n_steps = b * nhp * n_tiles
    n_pages, page = page_table.shape[1], cache_k.shape[2]
    pages_per_item = n_pages // 2
    ipg = (2 * db) // n_steps
    assert n_pages % 2 == 0 and ipg * n_steps == 2 * db and (ipg == 1 or ipg % 2 == 0)
    spb = max(1, ipg // 2)
    hk = pages_per_item * page
    key_chunk = min(512, hk)
    n_rows = wq_mask.shape[0]

    def sample_blk(bi, h, i, pt):
        step = (bi * nhp + h) * n_tiles + i
        return ((step * ipg) // 2 // spb, 0, 0)

    full2 = lambda bi, h, i, pt: (0, 0)
    grid_spec = pltpu.PrefetchScalarGridSpec(
        num_scalar_prefetch=1,
        grid=(b, nhp, n_tiles),
        in_specs=[
            pl.BlockSpec(lam_p.shape, full2),
            pl.BlockSpec((None, tile, wb), lambda bi, h, i, pt: (bi, i, h)),
            pl.BlockSpec((None, n_tiles, tile, wb), lambda bi, h, i, pt: (bi, 0, 0, h)),
            pl.BlockSpec((None, n_tiles, wb, tile), lambda bi, h, i, pt: (bi, 0, h, 0)),
            pl.BlockSpec((hp, 3, tile, 2 * tile), lambda bi, h, i, pt: (h, 0, 0, 0)),
            pl.BlockSpec((DV_A, tile), full2),
            pl.BlockSpec((spb, s_len, w), sample_blk),
            pl.BlockSpec((spb, s_len, w), sample_blk),
            pl.BlockSpec((spb, s_len, w), sample_blk),
            pl.BlockSpec(wq_mask.shape, full2),
            pl.BlockSpec(bias_past.shape, lambda bi, h, i, pt: (0, 0, 0)),
            pl.BlockSpec(bias_new.shape, full2),
            pl.BlockSpec(subg_rows.shape, full2),
            pl.BlockSpec(memory_space=pl.ANY),
            pl.BlockSpec(memory_space=pl.ANY),
        ],
        out_specs=[pl.BlockSpec((None, tile, wb), lambda bi, h, i, pt: (bi, i, h)),
                   pl.BlockSpec((spb, s_len, w), sample_blk)],
        scratch_shapes=[
            pltpu.VMEM((hp, tile, 2 * tile), F32),
            pltpu.VMEM((hp, tile, 2 * tile), F32),
            pltpu.VMEM((hp, 1, 2 * tile), F32),
            pltpu.VMEM((hp, SUBLANES, 2 * tile), F32),
            pltpu.VMEM((hp, DV_A, 2 * tile), F32),
            pltpu.VMEM((2 * ipg, w, hk), F32),
            pltpu.VMEM((2 * ipg, hk * H_A, DV_A), F32),
            pltpu.SemaphoreType.DMA((2 * ipg, 2)),
            pltpu.VMEM((spb, n_rows, 1), F32),
            pltpu.VMEM((spb, n_rows, 1), F32),
            pltpu.VMEM((spb, H_A, 2 * s_len, DV_A), F32),
        ],
    )
    return pl.pallas_call(
        functools.partial(_attention_body, tile=tile, lam_init=lam_init, stable=stable, items_per_step=ipg,
                          pages_per_item=pages_per_item, page=page, key_chunk=key_chunk),
        grid_spec=grid_spec,
        out_shape=[jax.ShapeDtypeStruct((b, l, w), BF16), jax.ShapeDtypeStruct((db, s_len, w), BF16)],
        compiler_params=_cparams(3),
        name="attention_stable" if stable else "attention",
    )(page_table, lam_p, q, k, vt, bias_tiles, subg_cols, qs, k_new, v_new, wq_mask, bias_past, bias_new,
      subg_rows, cache_k, cache_v)


def _mix0_body(x_ref, at_ref, ga_ref, ub_ref, vb_ref, wsp_ref, bsp_ref, wo_ref, y_ref):
    tm = x_ref.shape[0]
    wa = at_ref.shape[1]
    vb = vb_ref[...].astype(BF16)
    n_sub = tm // CHUNK_B
    cg = vb.shape[1] // G_B
    cols = []
    for g in range(G_B):
        lanes = slice(g * cg, (g + 1) * cg)
        rows = [_dot(wsp_ref[g], vb[sb * CHUNK_B:(sb + 1) * CHUNK_B, lanes]) for sb in range(n_sub)]
        cols.append(jnp.concatenate(rows, axis=0) if n_sub > 1 else rows[0])
    bsp = bsp_ref[...]
    bias = jnp.concatenate([bsp] * n_sub, axis=0) if n_sub > 1 else bsp
    mixed = jnp.concatenate(cols, axis=1) + bias
    ob = (ub_ref[...].astype(F32) * mixed).astype(BF16)
    oa = at_ref[...] * ga_ref[...]
    y_ref[...] = x_ref[...] + _dot(oa, wo_ref[:wa, :]) + _dot(ob, wo_ref[wa:, :])


def _mix0(x2d, attn, gas, ub, vb, wsp, bsp, wo_bf):
    t, d = x2d.shape
    tm = _row_tile(t, STREAM_TILE)
    row = lambda i: (i, 0)
    full = lambda i: (0, 0)
    return pl.pallas_call(
        _mix0_body,
        grid=(t // tm,),
        in_specs=[
            pl.BlockSpec((tm, d), row),
            pl.BlockSpec((tm, attn.shape[1]), row),
            pl.BlockSpec((tm, gas.shape[1]), row),
            pl.BlockSpec((tm, ub.shape[1]), row),
            pl.BlockSpec((tm, vb.shape[1]), row),
            pl.BlockSpec(wsp.shape, lambda i: (0, 0, 0)),
            pl.BlockSpec(bsp.shape, full),
            pl.BlockSpec(wo_bf.shape, full),
        ],
        out_specs=pl.BlockSpec((tm, d), row),
        out_shape=jax.ShapeDtypeStruct((t, d), F32),
        compiler_params=_cparams(1),
        name="mix0",
    )(x2d, attn, gas, ub, vb, wsp, bsp, wo_bf)


def _proj1_body(x_ref, ng_ref, w_ref, wg_ref, bg_ref, q_ref, k_ref, v_ref, gs_ref, la_ref, lamin_ref,
                *, hk, wc, q_scale):
    xb = _rms_rows(x_ref[...], ng_ref[...]).astype(BF16)
    q_ref[...] = _dot(xb, w_ref[:, 0:hk]) * q_scale
    k_ref[...] = _dot(xb, w_ref[:, hk:2 * hk])
    v_ref[...] = _dot(xb, w_ref[:, 2 * hk:2 * hk + wc]).astype(v_ref.dtype)
    gs_ref[...] = _silu(_dot(xb, w_ref[:, 2 * hk + wc:2 * hk + 2 * wc])).astype(gs_ref.dtype)
    za = _dot(xb, w_ref[:, 2 * hk + 2 * wc:]).astype(BF16)
    xg = _dot(za, wg_ref[...]) + bg_ref[...]
    la = (jnp.minimum(xg, 0.0) - jnp.log1p(jnp.exp(-jnp.abs(xg)))) * np.float32(1.0 / GLA_TAU)
    la_ref[...] = la
    lamin_ref[...] = jnp.min(la, axis=0, keepdims=True)


def _proj1(x2d, ng, w_bf, wg_bf, bg, *, hk, wc, q_scale):
    t, d = x2d.shape
    tm = _row_tile(t, STREAM_TILE)
    row = lambda i: (i, 0)
    full = lambda i: (0, 0)
    outs = [
        jax.ShapeDtypeStruct((t, hk), F32),
        jax.ShapeDtypeStruct((t, hk), F32),
        jax.ShapeDtypeStruct((t, wc), BF16),
        jax.ShapeDtypeStruct((t, wc), BF16),
        jax.ShapeDtypeStruct((t, hk), F32),
        jax.ShapeDtypeStruct((t // tm, 1, hk), F32),
    ]
    return pl.pallas_call(
        functools.partial(_proj1_body, hk=hk, wc=wc, q_scale=q_scale),
        grid=(t // tm,),
        in_specs=[
            pl.BlockSpec((tm, d), row),
            pl.BlockSpec((1, d), full),
            pl.BlockSpec(w_bf.shape, full),
            pl.BlockSpec(wg_bf.shape, full),
            pl.BlockSpec((1, hk), full),
        ],
        out_specs=[pl.BlockSpec((tm, hk), row), pl.BlockSpec((tm, hk), row), pl.BlockSpec((tm, wc), row),
                   pl.BlockSpec((tm, wc), row), pl.BlockSpec((tm, hk), row),
                   pl.BlockSpec((None, 1, hk), lambda i: (i, 0, 0))],
        out_shape=outs,
        compiler_params=_cparams(1),
        name="proj1",
    )(x2d, ng, w_bf, wg_bf, bg)


def _segment_cumsum(x, seg):
    n = x.shape[0]
    row = lax.broadcasted_iota(jnp.int32, x.shape, 0) % seg
    d = 1
    while d < seg:
        x = x + jnp.where(row >= d, pltpu.roll(x, d, axis=0), 0.0)
        d *= 2
    return x


def _block_reference(bc, hb):
    n, w = bc.shape
    if hb >= SUBLANES:
        pieces = []
        for blk in range(n // (2 * hb)):
            r = blk * 2 * hb + hb - 1
            pieces.append(jnp.broadcast_to(bc[r:r + 1, :], (2 * hb, w)))
        return jnp.concatenate(pieces, axis=0) if len(pieces) > 1 else pieces[0]
    pos = lax.broadcasted_iota(jnp.int32, bc.shape, 0) % (2 * hb)
    out = bc
    for delta in range(-hb, hb):
        if delta == 0:
            continue
        shifted = pltpu.roll(bc, (-delta) % n, axis=0)
        out = jnp.where(pos == hb - 1 - delta, shifted, out)
    return out


def _gla_intra_scores(q, k, bc, seg, bounded):
    n = q.shape[0]
    ri = lax.broadcasted_iota(jnp.int32, (n, n), 0)
    ci = lax.broadcasted_iota(jnp.int32, (n, n), 1)
    if bounded:
        att = _dot_nt((q * jnp.exp(bc)).astype(BF16), (k * jnp.exp(-bc)).astype(BF16))
        att = jnp.where(ri >= ci, att, 0.0)
        if seg < n:
            att = jnp.where((ri // seg) == (ci // seg), att, 0.0)
        return att
    rowpos = lax.broadcasted_iota(jnp.int32, q.shape, 0)
    att = jnp.where(ri == ci, _dot_nt(q.astype(BF16), k.astype(BF16)), 0.0)
    for hb in _levels(seg):
        ref = _block_reference(bc, hb)
        is_q = (rowpos % (2 * hb)) >= hb
        f = jnp.exp(jnp.where(is_q, bc - ref, ref - bc))
        qt = jnp.where(is_q, q * f, 0.0).astype(BF16)
        kt = jnp.where(is_q, 0.0, k * f).astype(BF16)
        same = (ri // (2 * hb)) == (ci // (2 * hb))
        att = att + jnp.where(same, _dot_nt(qt, kt), 0.0)
    return att


def _levels(seg):
    out, hb = [], seg // 2
    while hb >= 1:
        out.append(hb)
        hb //= 2
    return out


def _rows_scale_matrix(row, n_cols):
    hi = row.astype(BF16).astype(F32)
    r1 = row - hi
    mid = r1.astype(BF16).astype(F32)
    lo = (r1 - mid).astype(BF16).astype(F32)
    pad = jnp.zeros((2 * SUBLANES - 3, row.shape[1]), F32)
    parts = jnp.concatenate([hi, mid, lo, pad], axis=0).astype(BF16)
    return _dot_tn(parts, jnp.ones((2 * SUBLANES, n_cols), BF16))


def _gla_prompt_body(q_ref, k_ref, v_ref, la_ref, og_ref, s0_ref, o_ref, sout_ref, st_sc, *, dk, dv,
                     bounded):
    ci = pl.program_id(1)
    chunk = q_ref.shape[0]

    @pl.when(ci == 0)
    def _():
        st_sc[...] = s0_ref[...]

    for h in range(H_C):
        kl = slice(h * dk, (h + 1) * dk)
        vl = slice(h * dv, (h + 1) * dv)
        q, k, la = q_ref[:, kl], k_ref[:, kl], la_ref[:, kl]
        v = v_ref[:, vl]
        bc = _segment_cumsum(la, chunk)
        st = st_sc[h]
        att = _gla_intra_scores(q, k, bc, chunk, bounded)
        o = _dot((q * jnp.exp(bc)).astype(BF16), st.astype(BF16)) + _dot(att.astype(BF16), v)
        b_last = bc[chunk - 1:chunk, :]
        k_dec = (k * jnp.exp(b_last - bc)).astype(BF16)
        st_sc[h] = st * _rows_scale_matrix(jnp.exp(b_last), dv) + _dot_tn(k_dec, v)
        o_ref[:, vl] = _rms_rows(o, og_ref[...]).astype(o_ref.dtype)

    @pl.when(ci == pl.num_programs(1) - 1)
    def _():
        sout_ref[...] = st_sc[...]


def _gla_prompt(q, k, v, la, og, s0, *, bounded):
    b, l, hk = q.shape
    wc = v.shape[2]
    dk, dv = hk // H_C, wc // H_C
    chunk = min(GLA_CHUNK_P, l)
    tok = lambda bi, ci: (bi, ci, 0)
    st = lambda bi, ci: (bi, 0, 0, 0)
    return pl.pallas_call(
        functools.partial(_gla_prompt_body, dk=dk, dv=dv, bounded=bounded),
        grid=(b, l // chunk),
        in_specs=[
            pl.BlockSpec((None, chunk, hk), tok),
            pl.BlockSpec((None, chunk, hk), tok),
            pl.BlockSpec((None, chunk, wc), tok),
            pl.BlockSpec((None, chunk, hk), tok),
            pl.BlockSpec((1, dv), lambda bi, ci: (0, 0)),
            pl.BlockSpec((None, H_C, dk, dv), st),
        ],
        out_specs=[pl.BlockSpec((None, chunk, wc), tok), pl.BlockSpec((None, H_C, dk, dv), st)],
        out_shape=[jax.ShapeDtypeStruct((b, l, wc), BF16), jax.ShapeDtypeStruct((b, H_C, dk, dv), F32)],
        scratch_shapes=[pltpu.VMEM((H_C, dk, dv), F32)],
        compiler_params=_cparams(2),
        name="gla_prompt" if bounded else "gla_prompt_any_decay",
    )(q, k, v, la, og, s0)


def _gla_sample_body(q_ref, k_ref, v_ref, la_ref, og_ref, s0_ref, o_ref, sout_ref, *, dk, dv, seg, bounded):
    n_seq = s0_ref.shape[0]
    for h in range(H_C):
        kl = slice(h * dk, (h + 1) * dk)
        vl = slice(h * dv, (h + 1) * dv)
        q, k, la = q_ref[:, kl], k_ref[:, kl], la_ref[:, kl]
        v = v_ref[:, vl]
        bc = _segment_cumsum(la, seg)
        att = _gla_intra_scores(q, k, bc, seg, bounded)
        o_intra = _dot(att.astype(BF16), v)
        q_dec = (q * jnp.exp(bc)).astype(BF16)
        outs = []
        for s in range(n_seq):
            rows = slice(s * seg, (s + 1) * seg)
            st = s0_ref[s, h]
            bl = bc[s * seg + seg - 1:s * seg + seg, :]
            k_dec = (k[rows] * jnp.exp(bl - bc[rows])).astype(BF16)
            outs.append(_dot(q_dec[rows], st.astype(BF16)))
            sout_ref[s, h] = st * _rows_scale_matrix(jnp.exp(bl), dv) + _dot_tn(k_dec, v[rows])
        o = jnp.concatenate(outs, axis=0) + o_intra
        o_ref[:, vl] = _rms_rows(o, og_ref[...]).astype(o_ref.dtype)


def _gla_sample(q, k, v, la, og, s0, *, seg, bounded):
    t, hk = q.shape
    wc = v.shape[1]
    dk, dv = hk // H_C, wc // H_C
    db = t // seg
    grp = min(GLA_GROUP_S, db)
    rows = grp * seg
    tok = lambda i: (i, 0)
    st = lambda i: (i, 0, 0, 0)
    return pl.pallas_call(
        functools.partial(_gla_sample_body, dk=dk, dv=dv, seg=seg, bounded=bounded),
        grid=(db // grp,),
        in_specs=[
            pl.BlockSpec((rows, hk), tok),
            pl.BlockSpec((rows, hk), tok),
            pl.BlockSpec((rows, wc), tok),
            pl.BlockSpec((rows, hk), tok),
            pl.BlockSpec((1, dv), lambda i: (0, 0)),
            pl.BlockSpec((grp, H_C, dk, dv), st),
        ],
        out_specs=[pl.BlockSpec((rows, wc), tok), pl.BlockSpec((grp, H_C, dk, dv), st)],
        out_shape=[jax.ShapeDtypeStruct((t, wc), BF16), jax.ShapeDtypeStruct((db, H_C, dk, dv), F32)],
        compiler_params=_cparams(1),
        name="gla_sample" if bounded else "gla_sample_any_decay",
    )(q, k, v, la, og, s0)


def _out1_body(x_ref, o_ref, gs_ref, w_ref, y_ref):
    y_ref[...] = x_ref[...] + _dot(o_ref[...] * gs_ref[...], w_ref[...])


def _out1(x2d, o, gs, w_bf):
    t, d = x2d.shape
    tm = _row_tile(t, STREAM_TILE)
    row = lambda i: (i, 0)
    return pl.pallas_call(
        _out1_body,
        grid=(t // tm,),
        in_specs=[pl.BlockSpec((tm, d), row), pl.BlockSpec((tm, o.shape[1]), row),
                  pl.BlockSpec((tm, gs.shape[1]), row), pl.BlockSpec(w_bf.shape, lambda i: (0, 0))],
        out_specs=pl.BlockSpec((tm, d), row),
        out_shape=jax.ShapeDtypeStruct((t, d), F32),
        compiler_params=_cparams(1),
        name="out1",
    )(x2d, o, gs, w_bf)


def _t5_bucket_np(dist):
    n = np.maximum(dist, 0)
    max_exact = N_BUCKETS // 2
    nf = np.maximum(n, 1).astype(np.float32)
    ratio = np.log(nf / np.float32(max_exact)) / np.float32(math.log(MAX_DISTANCE / max_exact))
    large = max_exact + (ratio * np.float32(N_BUCKETS - max_exact)).astype(np.int32)
    large = np.minimum(large, N_BUCKETS - 1)
    return np.where(n < max_exact, n, large)


def _bias_by_distance(rel_bias, n):
    buckets = _t5_bucket_np(np.arange(n))
    assert np.all(np.diff(buckets) >= 0)
    cuts = [0] + [int(i) + 1 for i in np.nonzero(np.diff(buckets))[0]] + [n]
    pieces = [jnp.broadcast_to(rel_bias[int(buckets[a])][:, None], (rel_bias.shape[1], e - a))
              for a, e in zip(cuts[:-1], cuts[1:])]
    return jnp.concatenate(pieces, axis=1).astype(F32) * np.float32(LOG2E)


def _toeplitz(w, n):
    h = w.shape[0]
    flat = jnp.tile(w, (1, n))[:, :n * (2 * n - 1)]
    return flat.reshape(h, n, 2 * n - 1)[:, :, n - 1:]


def _prompt_bias_tiles(rel_bias, tile):
    assert int(_t5_bucket_np(np.array([tile + 1]))[0]) == N_BUCKETS - 1
    tab = _bias_by_distance(rel_bias, 2 * tile)
    tab = tab - tab[:, 2 * tile - 1:]
    h = tab.shape[0]
    masked = jnp.full((h, tile - 1), MASK_VALUE, F32)
    pad = jnp.zeros((h, 1), F32)
    w0 = jnp.concatenate([masked, tab[:, :tile], pad], axis=1)
    w1 = jnp.concatenate([tab[:, 1:], pad], axis=1)
    tiles = jnp.stack([_toeplitz(w0, tile), _toeplitz(w1, tile), jnp.zeros((h, tile, tile), F32)], axis=1)
    return jnp.concatenate([tiles, tiles], axis=-1)


def _sample_bias(rel_bias, past, s_len):
    n = past + s_len
    rev = _bias_by_distance(rel_bias, n)[:, ::-1]
    h = rev.shape[0]
    bp = jnp.stack([rev[:, s_len - 1 - q:s_len - 1 - q + past] for q in range(s_len)], axis=1)
    small = jnp.concatenate([rev[:, past:], jnp.full((h, LANES), MASK_VALUE, F32)], axis=1)
    bn = jnp.stack([small[:, s_len - 1 - q:s_len - 1 - q + LANES] for q in range(s_len)], axis=1)
    bp = bp.reshape(h * s_len, past)
    bn = bn.reshape(h * s_len, LANES)
    return jnp.concatenate([bp, bp], axis=0), jnp.concatenate([bn, bn], axis=0)


def _sample_query_mask(s_len):
    rows = np.arange(2 * H_A * s_len)
    c = rows // (H_A * s_len)
    h = (rows // s_len) % H_A
    lane_grp = np.arange(H_A * 2 * DH_A) // DH_A
    return jnp.asarray(lane_grp[None, :] == (h * 2 + c)[:, None], dtype=BF16)


def _spatial_weights(sp_w, sp_b, seg, cg):
    g, n, _ = sp_w.shape
    w = sp_w * jnp.tril(jnp.ones((n, n), sp_w.dtype))
    reps = n // seg
    if reps > 1:
        eye = jnp.eye(reps, dtype=sp_w.dtype)
        w = jnp.einsum("ab,gts->gatbs", eye, w[:, :seg, :seg]).reshape(g, n, n)
    bias = jnp.tile(sp_b[:, :seg], (1, reps))
    bias = jnp.repeat(jnp.transpose(bias), cg, axis=1)
    return w.astype(BF16), bias.astype(F32)


def kernel(x_prompt, x_sample, cache_k, cache_v, state_gla, page_table, rel_bias, norm0_g, w_in0, q_norm_g,
           k_norm_g, lam, subln_g, ln_v_g, ln_v_b, spatial_w, spatial_b, w_out0, norm1_g, w_in1, w_gate,
           b_gate, gla_norm_g, w_out1):
    b, l, d = x_prompt.shape
    db, s_len, _ = x_sample.shape
    n_pool, page = cache_k.shape[1], cache_k.shape[2]
    past = page_table.shape[1] * page
    wa = H_A * DV_A
    lam_init = 0.8 - 0.6 * math.exp(-0.3 * 0)
    rank = w_gate.shape[1]
    hk = w_gate.shape[2]
    wc = w_out1.shape[1]
    dk, dv = hk // H_C, wc // H_C

    w0 = w_in0[0].astype(BF16)
    wo0 = w_out0[0].astype(BF16)
    grp = np.arange(wa) // DH_A
    red = jnp.asarray((grp[:, None] == np.arange(LANES)[None, :]) / DH_A, dtype=BF16)
    expd = jnp.asarray(np.arange(LANES)[:, None] == grp[None, :], dtype=BF16)
    qg = (jnp.tile(q_norm_g[0].reshape(-1), H_A) * np.float32(DH_A ** -0.5 * LOG2E)).reshape(1, wa).astype(F32)
    kg = jnp.tile(k_norm_g[0].reshape(-1), H_A).reshape(1, wa).astype(F32)
    ng0 = norm0_g[0].reshape(1, d)
    lng = ln_v_g[0].reshape(1, -1)
    lnb = ln_v_b[0].reshape(1, -1)
    sub_scaled = subln_g[0] * np.float32(1.0 - lam_init)
    subg_cols = jnp.broadcast_to(sub_scaled[:, None], (DV_A, ATTN_TILE)).astype(F32)
    subg_rows = jnp.tile(sub_scaled, H_A).reshape(1, wa).astype(F32)
    bias_tiles = _prompt_bias_tiles(rel_bias, ATTN_TILE)
    bias_past, bias_new = _sample_bias(rel_bias, past, s_len)
    wq_mask = _sample_query_mask(s_len)
    cg = ln_v_g.shape[1] // G_B
    wsp_p, bsp_p = _spatial_weights(spatial_w[0], spatial_b[0], CHUNK_B, cg)
    wsp_s, bsp_s = _spatial_weights(spatial_w[0], spatial_b[0], s_len, cg)

    w1 = w_in1[0]
    w1_bf = jnp.pad(w1, ((0, 0), (0, LANES - rank))).astype(BF16)
    wg = jnp.pad(w_gate[0], ((0, LANES - rank), (0, 0))).astype(BF16)
    bg = b_gate[0].reshape(1, hk)
    ng1 = norm1_g[0].reshape(1, d)
    og = gla_norm_g[0].reshape(1, dv)
    wo1 = w_out1[0].astype(BF16)

    xp = x_prompt.reshape(b * l, d)
    xs = x_sample.reshape(db * s_len, d)
    ck_t = jnp.transpose(cache_k[0], (0, 2, 3, 4, 1)).reshape(n_pool, wa, page)
    cv_rows = cache_v[0].reshape(n_pool, page * H_A, DV_A)

    qp, kbp, ktp, vp, vtp, gap, ubp, vbp = _proj0(xp, ng0, w0, red, expd, qg, kg, lng, lnb, vb_dtype=BF16,
                                                  prompt_batch=b)
    qs, ks, vs, gas, ubs, vbs = _proj0(xs, ng0, w0, red, expd, qg, kg, lng, lnb, vb_dtype=F32)

    score_bound = (np.float32(1.02 * math.sqrt(DH_A) * LOG2E)
                   * jnp.max(jnp.abs(q_norm_g[0])) * jnp.max(jnp.abs(k_norm_g[0]))
                   + np.float32(LOG2E) * jnp.max(jnp.abs(rel_bias - rel_bias[N_BUCKETS - 1])))
    n_rows = bias_past.shape[0]
    bias_past_halves = jnp.transpose(bias_past.reshape(n_rows, 2, past // 2), (1, 0, 2))
    attn_args = (page_table, qp.reshape(b, l, wa), kbp.reshape(b, l // ATTN_TILE, ATTN_TILE, wa), vtp, lam[0],
                 bias_tiles, subg_cols, qs.reshape(db, s_len, wa), ks.reshape(db, s_len, wa),
                 vs.reshape(db, s_len, wa), ck_t, cv_rows, wq_mask, bias_past_halves, bias_new, subg_rows)
    at_p, at_s = lax.cond(score_bound <= SAFE_SCORE_BOUND,
                          lambda args: _attention(*args, lam_init=lam_init, stable=False),
                          lambda args: _attention(*args, lam_init=lam_init, stable=True),
                          attn_args)

    yp0 = _mix0(xp, at_p.reshape(b * l, wa), gap, ubp, vbp, wsp_p, bsp_p, wo0)
    ys0 = _mix0(xs, at_s.reshape(db * s_len, wa), gas, ubs, vbs, wsp_s, bsp_s, wo0)

    q_scale = np.float32(dk ** -0.5)
    q1p, k1p, v1p, gsp, lap, lamin_p = _proj1(yp0, ng1, w1_bf, wg, bg, hk=hk, wc=wc, q_scale=q_scale)
    q1s, k1s, v1s, gss, las, lamin_s = _proj1(ys0, ng1, w1_bf, wg, bg, hk=hk, wc=wc, q_scale=q_scale)

    gla_p_args = (q1p.reshape(b, l, hk), k1p.reshape(b, l, hk), v1p.reshape(b, l, wc),
                  lap.reshape(b, l, hk), og, jnp.zeros((b, H_C, dk, dv), F32))
    o_p, st_p = lax.cond(-jnp.min(lamin_p) * min(GLA_CHUNK_P, l) <= SAFE_DECAY_BOUND,
                         lambda a: _gla_prompt(*a, bounded=True),
                         lambda a: _gla_prompt(*a, bounded=False), gla_p_args)
    gla_s_args = (q1s, k1s, v1s, las, og, state_gla[0])
    o_s, st_s = lax.cond(-jnp.min(lamin_s) * s_len <= SAFE_DECAY_BOUND,
                         lambda a: _gla_sample(*a, seg=s_len, bounded=True),
                         lambda a: _gla_sample(*a, seg=s_len, bounded=False), gla_s_args)

    yp1 = _out1(yp0, o_p.reshape(b * l, wc), gsp, wo1)
    ys1 = _out1(ys0, o_s, gss, wo1)

    return (
        yp1.reshape(b, l, d),
        ys1.reshape(db, s_len, d),
        jnp.transpose(ktp.reshape(b, H_A, 2, DH_A, l), (0, 4, 1, 2, 3))[None],
        vp.reshape(1, b, l, H_A, DV_A),
        ks.reshape(1, db, s_len, H_A, 2, DH_A),
        vs.reshape(1, db, s_len, H_A, DV_A),
        vbs.reshape(1, db, s_len, -1),
        st_p.reshape(1, b, H_C, dk, dv),
        st_s.reshape(1, db, H_C, dk, dv),
    )
```

```python
import functools
import math

import numpy as np
import jax
import jax.numpy as jnp
from jax import lax
from jax.experimental import pallas as pl
from jax.experimental.pallas import tpu as pltpu

F32 = jnp.float32
BF16 = jnp.bfloat16

H_A = 8
DH_A = 64
DV_A = 2 * DH_A
G_B = 8
CHUNK_B = 128
H_C = 4
GLA_TAU = 16.0
N_BUCKETS = 32
MAX_DISTANCE = 128
EPS = 1e-6
MASK_VALUE = -1e30
LOG2E = 1.4426950408889634
SAFE_SCORE_BOUND = 60.0
SAFE_DECAY_BOUND = 60.0

LANES = 128
SUBLANES = 8
VMEM_LIMIT_BYTES = 56 * 1024 * 1024

TOKEN_TILE = 256
STREAM_TILE = 512
ATTN_TILE = TOKEN_TILE
GLA_CHUNK_P = 128
GLA_GROUP_S = 8


def _cparams(n_axes):
    return pltpu.CompilerParams(
        dimension_semantics=("arbitrary",) * n_axes,
        vmem_limit_bytes=VMEM_LIMIT_BYTES,
    )


def _row_tile(t, preferred):
    tm = min(preferred, t)
    while t % tm:
        tm -= CHUNK_B
    assert tm > 0 and t % tm == 0
    return tm


def _dot(a, b):
    return jnp.dot(a, b, preferred_element_type=F32)


def _dot_nt(a, b):
    return lax.dot_general(a, b, (((1,), (1,)), ((), ())), preferred_element_type=F32)


def _dot_tn(a, b):
    return lax.dot_general(a, b, (((0,), (0,)), ((), ())), preferred_element_type=F32)


def _silu(x):
    return x * jax.nn.sigmoid(x)


def _gelu(x):
    return 0.5 * x * (1.0 + lax.erf(x * np.float32(math.sqrt(0.5))))


def _rms_rows(x, g):
    return x * lax.rsqrt(jnp.mean(x * x, axis=-1, keepdims=True) + EPS) * g


def _split_bf16(x):
    hi = x.astype(BF16)
    lo = (x - hi.astype(F32)).astype(BF16)
    return hi, lo


def _proj0_body(x_ref, ng_ref, w_ref, red_ref, exp_ref, qg_ref, kg_ref, lng_ref, lnb_ref, *out_refs,
                width, prompt):
    if prompt:
        q_ref, kb_ref, kt_ref, v_ref, vt_ref, ga_ref, ub_ref, vb_ref = out_refs
    else:
        q_ref, k_ref, v_ref, ga_ref, ub_ref, vb_ref = out_refs
    xb = _rms_rows(x_ref[...], ng_ref[...]).astype(BF16)

    def seg(i):
        return _dot(xb, w_ref[:, i * width:(i + 1) * width])

    def group_rms(z, g):
        ms = _dot((z * z).astype(BF16), red_ref[...])
        hi, lo = _split_bf16(lax.rsqrt(ms + EPS))
        scale = _dot(hi, exp_ref[...]) + _dot(lo, exp_ref[...])
        return z * scale * g

    q_ref[...] = group_rms(seg(0), qg_ref[...]).astype(q_ref.dtype)
    k = group_rms(seg(1), kg_ref[...])
    v = seg(2)
    v_ref[...] = v
    if prompt:
        kb_ref[...] = k.astype(kb_ref.dtype)
        kt_ref[...] = k.T
        vt_ref[...] = v.T.astype(vt_ref.dtype)
    else:
        k_ref[...] = k
    ga_ref[...] = _silu(seg(3)).astype(ga_ref.dtype)
    ub_ref[...] = (_gelu(seg(4)) * _silu(seg(6))).astype(ub_ref.dtype)
    hv = _gelu(seg(5))
    hc = hv - jnp.mean(hv, axis=-1, keepdims=True)
    vb = hc * lax.rsqrt(jnp.mean(hc * hc, axis=-1, keepdims=True) + EPS)
    vb_ref[...] = (vb * lng_ref[...] + lnb_ref[...]).astype(vb_ref.dtype)


def _proj0(x2d, ng, w_bf, red, expd, qg, kg, lng, lnb, *, vb_dtype, prompt_batch=None):
    t, d = x2d.shape
    n_in = w_bf.shape[1]
    width = n_in // 7
    tm = _row_tile(t, TOKEN_TILE)
    row = lambda i: (i, 0)
    full = lambda i: (0, 0)
    tok = lambda dt: jax.ShapeDtypeStruct((t, width), dt)
    row_spec = pl.BlockSpec((tm, width), row)
    prompt = prompt_batch is not None
    if prompt:
        per = t // prompt_batch // tm
        outs = [tok(BF16), tok(BF16), jax.ShapeDtypeStruct((prompt_batch, width, per * tm), F32), tok(F32),
                jax.ShapeDtypeStruct((prompt_batch, per, width, tm), BF16), tok(BF16), tok(BF16), tok(vb_dtype)]
        out_specs = [row_spec, row_spec,
                     pl.BlockSpec((None, width, tm), lambda i: (i // per, 0, i % per)), row_spec,
                     pl.BlockSpec((None, None, width, tm), lambda i: (i // per, i % per, 0, 0)),
                     row_spec, row_spec, row_spec]
    else:
        outs = [tok(BF16), tok(F32), tok(F32), tok(BF16), tok(BF16), tok(vb_dtype)]
        out_specs = [row_spec] * 6
    return pl.pallas_call(
        functools.partial(_proj0_body, width=width, prompt=prompt),
        grid=(t // tm,),
        in_specs=[
            pl.BlockSpec((tm, d), row),
            pl.BlockSpec((1, d), full),
            pl.BlockSpec((d, n_in), full),
            pl.BlockSpec(red.shape, full),
            pl.BlockSpec(expd.shape, full),
            pl.BlockSpec((1, width), full),
            pl.BlockSpec((1, width), full),
            pl.BlockSpec((1, width), full),
            pl.BlockSpec((1, width), full),
        ],
        out_specs=out_specs,
        out_shape=outs,
        compiler_params=_cparams(1),
        name="proj0",
    )(x2d, ng, w_bf, red, expd, qg, kg, lng, lnb)


def _lambda_value(lam_ref, lam_init):
    lf = lam_ref[...]
    s1 = jnp.sum(lf[0:1] * lf[1:2], axis=-1, keepdims=True)
    s2 = jnp.sum(lf[2:3] * lf[3:4], axis=-1, keepdims=True)
    return jnp.exp(s1) - jnp.exp(s2) + lam_init


def _prompt_attention_step(qi, lam, q_ref, k_ref, vt_ref, bias_ref, subg_ref, o_ref, sa_sc, sb_sc, m_sc, l_sc,
                           acc_sc, *, tile, stable):
    hd = 2 * DH_A
    n_heads = q_ref.shape[1] // hd
    q = q_ref[...]
    lane = lax.broadcasted_iota(jnp.int32, (tile, hd), 1)
    q2t = []
    for hh in range(n_heads):
        qh = q[:, hh * hd:(hh + 1) * hd].astype(F32)
        q2 = jnp.concatenate([jnp.where(lane < DH_A, qh, 0.0), jnp.where(lane >= DH_A, qh, 0.0)], axis=0)
        q2t.append(q2.T.astype(BF16))

    def sublane_partial(x):
        return x.reshape(x.shape[0] // SUBLANES, SUBLANES, x.shape[1]).sum(axis=0)

    def scores(t, sbuf):
        j = qi - t
        for hh in range(n_heads):
            sbuf[hh] = _dot(k_ref[j, :, hh * hd:(hh + 1) * hd], q2t[hh])

    def consume(t, sbuf):
        j = qi - t
        kind = jnp.minimum(t, 2)
        ps = []
        for hh in range(n_heads):
            s = sbuf[hh] + bias_ref[hh, kind]
            if stable:
                m = m_sc[hh]
                m_new = jnp.maximum(m, jnp.max(s, axis=0, keepdims=True))
                alpha = jnp.exp2(m - m_new)
                m_sc[hh] = m_new
                p = jnp.exp2(s - m_new)
                l_sc[hh] = alpha * l_sc[hh] + sublane_partial(p)
                acc_sc[hh] = alpha * acc_sc[hh]
            else:
                p = jnp.exp2(s)
                l_sc[hh] = l_sc[hh] + sublane_partial(p)
            ps.append(p.astype(BF16))
        for hh in range(n_heads):
            acc_sc[hh] = acc_sc[hh] + _dot(vt_ref[j, hh * DV_A:(hh + 1) * DV_A, :], ps[hh])

    m_sc[...] = jnp.full(m_sc.shape, MASK_VALUE, F32)
    l_sc[...] = jnp.zeros(l_sc.shape, F32)
    acc_sc[...] = jnp.zeros(acc_sc.shape, F32)
    n_keys = qi + 1
    n_pairs = (n_keys - 1) // 2
    scores(0, sa_sc)

    def body(pair, c):
        t = 2 * pair
        scores(t + 1, sb_sc)
        consume(t, sa_sc)
        scores(t + 2, sa_sc)
        consume(t + 1, sb_sc)
        return c

    lax.fori_loop(0, n_pairs, body, 0)
    t_last = 2 * n_pairs

    @pl.when(n_keys - t_last == 1)
    def _():
        consume(t_last, sa_sc)

    @pl.when(n_keys - t_last == 2)
    def _():
        scores(t_last + 1, sb_sc)
        consume(t_last, sa_sc)
        consume(t_last + 1, sb_sc)

    outs = []
    for hh in range(n_heads):
        l8, acc = l_sc[hh], acc_sc[hh]
        o_all = acc * (1.0 / jnp.sum(l8, axis=0, keepdims=True))
        o = o_all[:, :tile] - lam * o_all[:, tile:]
        on = o * lax.rsqrt(jnp.mean(o * o, axis=0, keepdims=True) + EPS) * subg_ref[...]
        outs.append(on.T)
    o_ref[...] = jnp.concatenate(outs, axis=1).astype(o_ref.dtype)


def _sample_attention_item(half, slot, sb, lam, qs_ref, kn_ref, vn_ref, mask_ref, bp_ref, bn_ref, subg_ref, os_ref,
                           kbuf, vbuf, sm_sc, sl_sc, sacc_sc, *, key_chunk):
    q = qs_ref[sb]
    s_len = q.shape[0]
    n_rows = mask_ref.shape[0]
    half_rows = H_A * s_len
    wq = jnp.concatenate([q] * (n_rows // s_len), axis=0) * mask_ref[...]
    first = half == 0
    n_chunks = kbuf.shape[2] // key_chunk

    pad_rows = bn_ref.shape[1] - s_len
    kn = jnp.concatenate([kn_ref[sb], jnp.zeros((pad_rows, kn_ref.shape[2]), F32)], axis=0).astype(BF16)
    vn = jnp.concatenate([vn_ref[sb], jnp.zeros((pad_rows, vn_ref.shape[2]), F32)], axis=0).astype(BF16)
    s_parts = [_dot_nt(wq, kn) + bn_ref[...] + jnp.where(first, 0.0, MASK_VALUE).astype(F32)]
    for c in range(n_chunks):
        keys = pl.ds(c * key_chunk, key_chunk)
        s_parts.append(_dot(wq, kbuf[slot, :, keys].astype(BF16)) + bp_ref[half, :, keys])

    m_old = jnp.where(first, MASK_VALUE, sm_sc[sb])
    m = m_old
    for s in s_parts:
        m = jnp.maximum(m, jnp.max(s, axis=1, keepdims=True))
    alpha = jnp.exp2(m_old - m)
    p_parts = [jnp.exp2(s - m) for s in s_parts]
    l = alpha * jnp.where(first, 0.0, sl_sc[sb])
    for p in p_parts:
        l = l + jnp.sum(p, axis=1, keepdims=True)
    sm_sc[sb] = m
    sl_sc[sb] = l
    row = lax.broadcasted_iota(jnp.int32, l.shape, 0)
    scale = jnp.where(row < half_rows, 1.0, -lam) / l

    def head_rows(x, h):
        r0 = h * s_len
        return jnp.concatenate([x[r0:r0 + s_len], x[half_rows + r0:half_rows + r0 + s_len]], axis=0)

    vhs = [vbuf[slot, h].astype(BF16) for h in range(H_A)]
    p_past = jnp.concatenate(p_parts[1:], axis=1) if n_chunks > 1 else p_parts[1]
    pvs = []
    for h in range(H_A):
        pv = _dot(head_rows(p_past, h).astype(BF16), vhs[h])
        pvs.append(pv + _dot(head_rows(p_parts[0], h).astype(BF16), vn[:, h * DV_A:(h + 1) * DV_A]))
    heads = []
    for h in range(H_A):
        acc = head_rows(alpha, h) * jnp.where(first, 0.0, sacc_sc[sb, h]) + pvs[h]
        sacc_sc[sb, h] = acc
        w = acc * head_rows(scale, h)
        o_h = w[:s_len] + w[s_len:]
        heads.append(o_h * lax.rsqrt(jnp.mean(o_h * o_h, axis=-1, keepdims=True) + EPS))
    os_ref[sb] = (jnp.concatenate(heads, axis=1) * subg_ref[...]).astype(os_ref.dtype)


def _attention_body(pt_ref, lam_ref, q_ref, k_ref, vt_ref, bias_ref, subg_ref,
                    qs_ref, kn_ref, vn_ref, mask_ref, bp_ref, bn_ref, subgr_ref, ck_hbm, cv_hbm,
                    o_ref, os_ref,
                    sa_sc, sb_sc, m_sc, l_sc, acc_sc, kbuf, vbuf, sem, sm_sc, sl_sc, sacc_sc,
                    *, tile, lam_init, stable, items_per_step, pages_per_item, page, key_chunk):
    bi, hp, qi = pl.program_id(0), pl.program_id(1), pl.program_id(2)
    step = (bi * pl.num_programs(1) + hp) * pl.num_programs(2) + qi
    n_steps = pl.num_programs(0) * pl.num_programs(1) * pl.num_programs(2)
    ipg = items_per_step

    def item_copies(item, slot):
        seq, half = item // 2, item % 2
        cps = []
        for p in range(pages_per_item):
            pg = pt_ref[seq, half * pages_per_item + p]
            cps.append(pltpu.make_async_copy(ck_hbm.at[pg], kbuf.at[slot, :, pl.ds(p * page, page)],
                                             sem.at[slot, 0]))
            for h in range(H_A):
                cps.append(pltpu.make_async_copy(cv_hbm.at[pg, :, h, :],
                                                 vbuf.at[slot, h, pl.ds(p * page, page), :], sem.at[slot, 1]))
        return cps

    @pl.when(step == 0)
    def _():
        for r in range(ipg):
            for c in item_copies(r, r):
                c.start()
        sm_sc[...] = jnp.zeros(sm_sc.shape, F32)
        sl_sc[...] = jnp.zeros(sl_sc.shape, F32)
        sacc_sc[...] = jnp.zeros(sacc_sc.shape, F32)

    @pl.when(step + 1 < n_steps)
    def _():
        for r in range(ipg):
            for c in item_copies((step + 1) * ipg + r, ((step + 1) % 2) * ipg + r):
                c.start()

    lam = _lambda_value(lam_ref, lam_init)
    for r in range(ipg):
        item = step * ipg + r
        slot = (step % 2) * ipg + r
        for c in item_copies(item, slot):
            c.wait()
        if ipg == 1:
            half, sb = item % 2, 0
        else:
            half, sb = jnp.int32(r % 2), r // 2
        _sample_attention_item(half, slot, sb, lam, qs_ref, kn_ref, vn_ref, mask_ref, bp_ref, bn_ref, subgr_ref,
                               os_ref, kbuf, vbuf, sm_sc, sl_sc, sacc_sc, key_chunk=key_chunk)

    _prompt_attention_step(qi, lam, q_ref, k_ref, vt_ref, bias_ref, subg_ref, o_ref, sa_sc, sb_sc, m_sc, l_sc,
                           acc_sc, tile=tile, stable=stable)


def _attention(page_table, q, k, vt, lam_p, bias_tiles, subg_cols, qs, k_new, v_new, cache_k, cache_v, wq_mask,
               bias_past, bias_new, subg_rows, *, lam_init, stable):
    b, l, w = q.shape
    db, s_len, _ = qs.shape
    tile = ATTN_TILE
    n_tiles = l // tile
    hp = 2
    wb = hp * 2 * DH_A
    nhp = H_A // hp
    n_steps = b * nhp * n_tiles
    n_pages, page = page_table.shape[1], cache_k.shape[2]
    pages_per_item = n_pages // 2
    ipg = (2 * db) // n_steps
    assert n_pages % 2 == 0 and ipg * n_steps == 2 * db and (ipg == 1 or ipg % 2 == 0)
    spb = max(1, ipg // 2)
    hk = pages_per_item * page
    key_chunk = min(512, hk)
    n_rows = wq_mask.shape[0]

    def sample_blk(bi, h, i, pt):
        step = (bi * nhp + h) * n_tiles + i
        return ((step * ipg) // 2 // spb, 0, 0)

    full2 = lambda bi, h, i, pt: (0, 0)
    grid_spec = pltpu.PrefetchScalarGridSpec(
        num_scalar_prefetch=1,
        grid=(b, nhp, n_tiles),
        in_specs=[
            pl.BlockSpec(lam_p.shape, full2),
            pl.BlockSpec((None, tile, wb), lambda bi, h, i, pt: (bi, i, h)),
            pl.BlockSpec((None, n_tiles, tile, wb), lambda bi, h, i, pt: (bi, 0, 0, h)),
            pl.BlockSpec((None, n_tiles, wb, tile), lambda bi, h, i, pt: (bi, 0, h, 0)),
            pl.BlockSpec((hp, 3, tile, 2 * tile), lambda bi, h, i, pt: (h, 0, 0, 0)),
            pl.BlockSpec((DV_A, tile), full2),
            pl.BlockSpec((spb, s_len, w), sample_blk),
            pl.BlockSpec((spb, s_len, w), sample_blk),
            pl.BlockSpec((spb, s_len, w), sample_blk),
            pl.BlockSpec(wq_mask.shape, full2),
            pl.BlockSpec(bias_past.shape, lambda bi, h, i, pt: (0, 0, 0)),
            pl.BlockSpec(bias_new.shape, full2),
            pl.BlockSpec(subg_rows.shape, full2),
            pl.BlockSpec(memory_space=pl.ANY),
            pl.BlockSpec(memory_space=pl.ANY),
        ],
        out_specs=[pl.BlockSpec((None, tile, wb), lambda bi, h, i, pt: (bi, i, h)),
                   pl.BlockSpec((spb, s_len, w), sample_blk)],
        scratch_shapes=[
            pltpu.VMEM((hp, tile, 2 * tile), F32),
            pltpu.VMEM((hp, tile, 2 * tile), F32),
            pltpu.VMEM((hp, 1, 2 * tile), F32),
            pltpu.VMEM((hp, SUBLANES, 2 * tile), F32),
            pltpu.VMEM((hp, DV_A, 2 * tile), F32),
            pltpu.VMEM((2 * ipg, w, hk), F32),
            pltpu.VMEM((2 * ipg, H_A, hk, DV_A), F32),
            pltpu.SemaphoreType.DMA((2 * ipg, 2)),
            pltpu.VMEM((spb, n_rows, 1), F32),
            pltpu.VMEM((spb, n_rows, 1), F32),
            pltpu.VMEM((spb, H_A, 2 * s_len, DV_A), F32),
        ],
    )
    return pl.pallas_call(
        functools.partial(_attention_body, tile=tile, lam_init=lam_init, stable=stable, items_per_step=ipg,
                          pages_per_item=pages_per_item, page=page, key_chunk=key_chunk),
        grid_spec=grid_spec,
        out_shape=[jax.ShapeDtypeStruct((b, l, w), BF16), jax.ShapeDtypeStruct((db, s_len, w), BF16)],
        compiler_params=_cparams(3),
        name="attention_stable" if stable else "attention",
    )(page_table, lam_p, q, k, vt, bias_tiles, subg_cols, qs, k_new, v_new, wq_mask, bias_past, bias_new,
      subg_rows, cache_k, cache_v)


def _mix0_body(x_ref, at_ref, ga_ref, ub_ref, vb_ref, wsp_ref, bsp_ref, wo_ref, y_ref):
    tm = x_ref.shape[0]
    wa = at_ref.shape[1]
    vb = vb_ref[...].astype(BF16)
    n_sub = tm // CHUNK_B
    cg = vb.shape[1] // G_B
    cols = []
    for g in range(G_B):
        lanes = slice(g * cg, (g + 1) * cg)
        rows = [_dot(wsp_ref[g], vb[sb * CHUNK_B:(sb + 1) * CHUNK_B, lanes]) for sb in range(n_sub)]
        cols.append(jnp.concatenate(rows, axis=0) if n_sub > 1 else rows[0])
    bsp = bsp_ref[...]
    bias = jnp.concatenate([bsp] * n_sub, axis=0) if n_sub > 1 else bsp
    mixed = jnp.concatenate(cols, axis=1) + bias
    ob = (ub_ref[...].astype(F32) * mixed).astype(BF16)
    oa = at_ref[...] * ga_ref[...]
    y_ref[...] = x_ref[...] + _dot(oa, wo_ref[:wa, :]) + _dot(ob, wo_ref[wa:, :])


def _mix0(x2d, attn, gas, ub, vb, wsp, bsp, wo_bf):
    t, d = x2d.shape
    tm = _row_tile(t, STREAM_TILE)
    row = lambda i: (i, 0)
    full = lambda i: (0, 0)
    return pl.pallas_call(
        _mix0_body,
        grid=(t // tm,),
        in_specs=[
            pl.BlockSpec((tm, d), row),
            pl.BlockSpec((tm, attn.shape[1]), row),
            pl.BlockSpec((tm, gas.shape[1]), row),
            pl.BlockSpec((tm, ub.shape[1]), row),
            pl.BlockSpec((tm, vb.shape[1]), row),
            pl.BlockSpec(wsp.shape, lambda i: (0, 0, 0)),
            pl.BlockSpec(bsp.shape, full),
            pl.BlockSpec(wo_bf.shape, full),
        ],
        out_specs=pl.BlockSpec((tm, d), row),
        out_shape=jax.ShapeDtypeStruct((t, d), F32),
        compiler_params=_cparams(1),
        name="mix0",
    )(x2d, attn, gas, ub, vb, wsp, bsp, wo_bf)


def _proj1_body(x_ref, ng_ref, w_ref, wg_ref, bg_ref, q_ref, k_ref, v_ref, gs_ref, la_ref, lamin_ref,
                *, hk, wc, q_scale):
    xb = _rms_rows(x_ref[...], ng_ref[...]).astype(BF16)
    q_ref[...] = _dot(xb, w_ref[:, 0:hk]) * q_scale
    k_ref[...] = _dot(xb, w_ref[:, hk:2 * hk])
    v_ref[...] = _dot(xb, w_ref[:, 2 * hk:2 * hk + wc]).astype(v_ref.dtype)
    gs_ref[...] = _silu(_dot(xb, w_ref[:, 2 * hk + wc:2 * hk + 2 * wc])).astype(gs_ref.dtype)
    za = _dot(xb, w_ref[:, 2 * hk + 2 * wc:]).astype(BF16)
    xg = _dot(za, wg_ref[...]) + bg_ref[...]
    la = (jnp.minimum(xg, 0.0) - jnp.log1p(jnp.exp(-jnp.abs(xg)))) * np.float32(1.0 / GLA_TAU)
    la_ref[...] = la
    lamin_ref[...] = jnp.min(la, axis=0, keepdims=True)


def _proj1(x2d, ng, w_bf, wg_bf, bg, *, hk, wc, q_scale):
    t, d = x2d.shape
    tm = _row_tile(t, STREAM_TILE)
    row = lambda i: (i, 0)
    full = lambda i: (0, 0)
    outs = [
        jax.ShapeDtypeStruct((t, hk), F32),
        jax.ShapeDtypeStruct((t, hk), F32),
        jax.ShapeDtypeStruct((t, wc), BF16),
        jax.ShapeDtypeStruct((t, wc), BF16),
        jax.ShapeDtypeStruct((t, hk), F32),
        jax.ShapeDtypeStruct((t // tm, 1, hk), F32),
    ]
    return pl.pallas_call(
        functools.partial(_proj1_body, hk=hk, wc=wc, q_scale=q_scale),
        grid=(t // tm,),
        in_specs=[
            pl.BlockSpec((tm, d), row),
            pl.BlockSpec((1, d), full),
            pl.BlockSpec(w_bf.shape, full),
            pl.BlockSpec(wg_bf.shape, full),
            pl.BlockSpec((1, hk), full),
        ],
        out_specs=[pl.BlockSpec((tm, hk), row), pl.BlockSpec((tm, hk), row), pl.BlockSpec((tm, wc), row),
                   pl.BlockSpec((tm, wc), row), pl.BlockSpec((tm, hk), row),
                   pl.BlockSpec((None, 1, hk), lambda i: (i, 0, 0))],
        out_shape=outs,
        compiler_params=_cparams(1),
        name="proj1",
    )(x2d, ng, w_bf, wg_bf, bg)


def _segment_cumsum(x, seg):
    n = x.shape[0]
    row = lax.broadcasted_iota(jnp.int32, x.shape, 0) % seg
    d = 1
    while d < seg:
        x = x + jnp.where(row >= d, pltpu.roll(x, d, axis=0), 0.0)
        d *= 2
    return x


def _block_reference(bc, hb):
    n, w = bc.shape
    if hb >= SUBLANES:
        pieces = []
        for blk in range(n // (2 * hb)):
            r = blk * 2 * hb + hb - 1
            pieces.append(jnp.broadcast_to(bc[r:r + 1, :], (2 * hb, w)))
        return jnp.concatenate(pieces, axis=0) if len(pieces) > 1 else pieces[0]
    pos = lax.broadcasted_iota(jnp.int32, bc.shape, 0) % (2 * hb)
    out = bc
    for delta in range(-hb, hb):
        if delta == 0:
            continue
        shifted = pltpu.roll(bc, (-delta) % n, axis=0)
        out = jnp.where(pos == hb - 1 - delta, shifted, out)
    return out


def _gla_intra_scores(q, k, bc, seg, bounded):
    n = q.shape[0]
    ri = lax.broadcasted_iota(jnp.int32, (n, n), 0)
    ci = lax.broadcasted_iota(jnp.int32, (n, n), 1)
    if bounded:
        att = _dot_nt((q * jnp.exp(bc)).astype(BF16), (k * jnp.exp(-bc)).astype(BF16))
        att = jnp.where(ri >= ci, att, 0.0)
        if seg < n:
            att = jnp.where((ri // seg) == (ci // seg), att, 0.0)
        return att
    rowpos = lax.broadcasted_iota(jnp.int32, q.shape, 0)
    att = jnp.where(ri == ci, _dot_nt(q.astype(BF16), k.astype(BF16)), 0.0)
    for hb in _levels(seg):
        ref = _block_reference(bc, hb)
        is_q = (rowpos % (2 * hb)) >= hb
        f = jnp.exp(jnp.where(is_q, bc - ref, ref - bc))
        qt = jnp.where(is_q, q * f, 0.0).astype(BF16)
        kt = jnp.where(is_q, 0.0, k * f).astype(BF16)
        same = (ri // (2 * hb)) == (ci // (2 * hb))
        att = att + jnp.where(same, _dot_nt(qt, kt), 0.0)
    return att


def _levels(seg):
    out, hb = [], seg // 2
    while hb >= 1:
        out.append(hb)
        hb //= 2
    return out


def _rows_scale_matrix(row, n_cols):
    hi = row.astype(BF16).astype(F32)
    r1 = row - hi
    mid = r1.astype(BF16).astype(F32)
    lo = (r1 - mid).astype(BF16).astype(F32)
    pad = jnp.zeros((2 * SUBLANES - 3, row.shape[1]), F32)
    parts = jnp.concatenate([hi, mid, lo, pad], axis=0).astype(BF16)
    return _dot_tn(parts, jnp.ones((2 * SUBLANES, n_cols), BF16))


def _gla_prompt_body(q_ref, k_ref, v_ref, la_ref, og_ref, s0_ref, o_ref, sout_ref, st_sc, *, dk, dv,
                     bounded):
    ci = pl.program_id(1)
    chunk = q_ref.shape[0]

    @pl.when(ci == 0)
    def _():
        st_sc[...] = s0_ref[...]

    kls = [slice(h * dk, (h + 1) * dk) for h in range(H_C)]
    vls = [slice(h * dv, (h + 1) * dv) for h in range(H_C)]
    bcs = [_segment_cumsum(la_ref[:, kl], chunk) for kl in kls]
    atts = [_gla_intra_scores(q_ref[:, kl], k_ref[:, kl], bc, chunk, bounded) for kl, bc in zip(kls, bcs)]
    outs, upds, decays = [], [], []
    for h in range(H_C):
        q, k, v, bc = q_ref[:, kls[h]], k_ref[:, kls[h]], v_ref[:, vls[h]], bcs[h]
        b_last = bc[chunk - 1:chunk, :]
        k_dec = (k * jnp.exp(b_last - bc)).astype(BF16)
        outs.append(_dot((q * jnp.exp(bc)).astype(BF16), st_sc[h].astype(BF16)) + _dot(atts[h].astype(BF16), v))
        upds.append(_dot_tn(k_dec, v))
        decays.append(_rows_scale_matrix(jnp.exp(b_last), dv))
    for h in range(H_C):
        st_sc[h] = st_sc[h] * decays[h] + upds[h]
        o_ref[:, vls[h]] = _rms_rows(outs[h], og_ref[...]).astype(o_ref.dtype)

    @pl.when(ci == pl.num_programs(1) - 1)
    def _():
        sout_ref[...] = st_sc[...]


def _gla_prompt(q, k, v, la, og, s0, *, bounded):
    b, l, hk = q.shape
    wc = v.shape[2]
    dk, dv = hk // H_C, wc // H_C
    chunk = min(GLA_CHUNK_P, l)
    tok = lambda bi, ci: (bi, ci, 0)
    st = lambda bi, ci: (bi, 0, 0, 0)
    return pl.pallas_call(
        functools.partial(_gla_prompt_body, dk=dk, dv=dv, bounded=bounded),
        grid=(b, l // chunk),
        in_specs=[
            pl.BlockSpec((None, chunk, hk), tok),
            pl.BlockSpec((None, chunk, hk), tok),
            pl.BlockSpec((None, chunk, wc), tok),
            pl.BlockSpec((None, chunk, hk), tok),
            pl.BlockSpec((1, dv), lambda bi, ci: (0, 0)),
            pl.BlockSpec((None, H_C, dk, dv), st),
        ],
        out_specs=[pl.BlockSpec((None, chunk, wc), tok), pl.BlockSpec((None, H_C, dk, dv), st)],
        out_shape=[jax.ShapeDtypeStruct((b, l, wc), BF16), jax.ShapeDtypeStruct((b, H_C, dk, dv), F32)],
        scratch_shapes=[pltpu.VMEM((H_C, dk, dv), F32)],
        compiler_params=_cparams(2),
        name="gla_prompt" if bounded else "gla_prompt_any_decay",
    )(q, k, v, la, og, s0)


def _gla_sample_body(q_ref, k_ref, v_ref, la_ref, og_ref, s0_ref, o_ref, sout_ref, *, dk, dv, seg, bounded):
    n_seq = s0_ref.shape[0]
    for h in range(H_C):
        kl = slice(h * dk, (h + 1) * dk)
        vl = slice(h * dv, (h + 1) * dv)
        q, k, la = q_ref[:, kl], k_ref[:, kl], la_ref[:, kl]
        v = v_ref[:, vl]
        bc = _segment_cumsum(la, seg)
        att = _gla_intra_scores(q, k, bc, seg, bounded)
        o_intra = _dot(att.astype(BF16), v)
        q_dec = (q * jnp.exp(bc)).astype(BF16)
        outs = []
        for s in range(n_seq):
            rows = slice(s * seg, (s + 1) * seg)
            st = s0_ref[s, h]
            bl = bc[s * seg + seg - 1:s * seg + seg, :]
            k_dec = (k[rows] * jnp.exp(bl - bc[rows])).astype(BF16)
            outs.append(_dot(q_dec[rows], st.astype(BF16)))
            sout_ref[s, h] = st * _rows_scale_matrix(jnp.exp(bl), dv) + _dot_tn(k_dec, v[rows])
        o = jnp.concatenate(outs, axis=0) + o_intra
        o_ref[:, vl] = _rms_rows(o, og_ref[...]).astype(o_ref.dtype)


def _gla_sample(q, k, v, la, og, s0, *, seg, bounded):
    t, hk = q.shape
    wc = v.shape[1]
    dk, dv = hk // H_C, wc // H_C
    db = t // seg
    grp = min(GLA_GROUP_S, db)
    rows = grp * seg
    tok = lambda i: (i, 0)
    st = lambda i: (i, 0, 0, 0)
    return pl.pallas_call(
        functools.partial(_gla_sample_body, dk=dk, dv=dv, seg=seg, bounded=bounded),
        grid=(db // grp,),
        in_specs=[
            pl.BlockSpec((rows, hk), tok),
            pl.BlockSpec((rows, hk), tok),
            pl.BlockSpec((rows, wc), tok),
            pl.BlockSpec((rows, hk), tok),
            pl.BlockSpec((1, dv), lambda i: (0, 0)),
            pl.BlockSpec((grp, H_C, dk, dv), st),
        ],
        out_specs=[pl.BlockSpec((rows, wc), tok), pl.BlockSpec((grp, H_C, dk, dv), st)],
        out_shape=[jax.ShapeDtypeStruct((t, wc), BF16), jax.ShapeDtypeStruct((db, H_C, dk, dv), F32)],
        compiler_params=_cparams(1),
        name="gla_sample" if bounded else "gla_sample_any_decay",
    )(q, k, v, la, og, s0)


def _out1_body(x_ref, o_ref, gs_ref, w_ref, y_ref):
    y_ref[...] = x_ref[...] + _dot(o_ref[...] * gs_ref[...], w_ref[...])


def _out1(x2d, o, gs, w_bf):
    t, d = x2d.shape
    tm = _row_tile(t, STREAM_TILE)
    row = lambda i: (i, 0)
    return pl.pallas_call(
        _out1_body,
        grid=(t // tm,),
        in_specs=[pl.BlockSpec((tm, d), row), pl.BlockSpec((tm, o.shape[1]), row),
                  pl.BlockSpec((tm, gs.shape[1]), row), pl.BlockSpec(w_bf.shape, lambda i: (0, 0))],
        out_specs=pl.BlockSpec((tm, d), row),
        out_shape=jax.ShapeDtypeStruct((t, d), F32),
        compiler_params=_cparams(1),
        name="out1",
    )(x2d, o, gs, w_bf)


def _t5_bucket_np(dist):
    n = np.maximum(dist, 0)
    max_exact = N_BUCKETS // 2
    nf = np.maximum(n, 1).astype(np.float32)
    ratio = np.log(nf / np.float32(max_exact)) / np.float32(math.log(MAX_DISTANCE / max_exact))
    large = max_exact + (ratio * np.float32(N_BUCKETS - max_exact)).astype(np.int32)
    large = np.minimum(large, N_BUCKETS - 1)
    return np.where(n < max_exact, n, large)


def _bias_by_distance(rel_bias, n):
    buckets = _t5_bucket_np(np.arange(n))
    assert np.all(np.diff(buckets) >= 0)
    cuts = [0] + [int(i) + 1 for i in np.nonzero(np.diff(buckets))[0]] + [n]
    pieces = [jnp.broadcast_to(rel_bias[int(buckets[a])][:, None], (rel_bias.shape[1], e - a))
              for a, e in zip(cuts[:-1], cuts[1:])]
    return jnp.concatenate(pieces, axis=1).astype(F32) * np.float32(LOG2E)


def _toeplitz(w, n):
    h = w.shape[0]
    flat = jnp.tile(w, (1, n))[:, :n * (2 * n - 1)]
    return flat.reshape(h, n, 2 * n - 1)[:, :, n - 1:]


def _prompt_bias_tiles(rel_bias, tile):
    assert int(_t5_bucket_np(np.array([tile + 1]))[0]) == N_BUCKETS - 1
    tab = _bias_by_distance(rel_bias, 2 * tile)
    tab = tab - tab[:, 2 * tile - 1:]
    h = tab.shape[0]
    masked = jnp.full((h, tile - 1), MASK_VALUE, F32)
    pad = jnp.zeros((h, 1), F32)
    w0 = jnp.concatenate([masked, tab[:, :tile], pad], axis=1)
    w1 = jnp.concatenate([tab[:, 1:], pad], axis=1)
    tiles = jnp.stack([_toeplitz(w0, tile), _toeplitz(w1, tile), jnp.zeros((h, tile, tile), F32)], axis=1)
    return jnp.concatenate([tiles, tiles], axis=-1)


def _sample_bias(rel_bias, past, s_len):
    n = past + s_len
    rev = _bias_by_distance(rel_bias, n)[:, ::-1]
    h = rev.shape[0]
    bp = jnp.stack([rev[:, s_len - 1 - q:s_len - 1 - q + past] for q in range(s_len)], axis=1)
    small = jnp.concatenate([rev[:, past:], jnp.full((h, LANES), MASK_VALUE, F32)], axis=1)
    bn = jnp.stack([small[:, s_len - 1 - q:s_len - 1 - q + LANES] for q in range(s_len)], axis=1)
    bp = bp.reshape(h * s_len, past)
    bn = bn.reshape(h * s_len, LANES)
    return jnp.concatenate([bp, bp], axis=0), jnp.concatenate([bn, bn], axis=0)


def _sample_query_mask(s_len):
    rows = np.arange(2 * H_A * s_len)
    c = rows // (H_A * s_len)
    h = (rows // s_len) % H_A
    lane_grp = np.arange(H_A * 2 * DH_A) // DH_A
    return jnp.asarray(lane_grp[None, :] == (h * 2 + c)[:, None], dtype=BF16)


def _spatial_weights(sp_w, sp_b, seg, cg):
    g, n, _ = sp_w.shape
    w = sp_w * jnp.tril(jnp.ones((n, n), sp_w.dtype))
    reps = n // seg
    if reps > 1:
        eye = jnp.eye(reps, dtype=sp_w.dtype)
        w = jnp.einsum("ab,gts->gatbs", eye, w[:, :seg, :seg]).reshape(g, n, n)
    bias = jnp.tile(sp_b[:, :seg], (1, reps))
    bias = jnp.repeat(jnp.transpose(bias), cg, axis=1)
    return w.astype(BF16), bias.astype(F32)


def kernel(x_prompt, x_sample, cache_k, cache_v, state_gla, page_table, rel_bias, norm0_g, w_in0, q_norm_g,
           k_norm_g, lam, subln_g, ln_v_g, ln_v_b, spatial_w, spatial_b, w_out0, norm1_g, w_in1, w_gate,
           b_gate, gla_norm_g, w_out1):
    b, l, d = x_prompt.shape
    db, s_len, _ = x_sample.shape
    n_pool, page = cache_k.shape[1], cache_k.shape[2]
    past = page_table.shape[1] * page
    wa = H_A * DV_A
    lam_init = 0.8 - 0.6 * math.exp(-0.3 * 0)
    rank = w_gate.shape[1]
    hk = w_gate.shape[2]
    wc = w_out1.shape[1]
    dk, dv = hk // H_C, wc // H_C

    w0 = w_in0[0].astype(BF16)
    wo0 = w_out0[0].astype(BF16)
    grp = np.arange(wa) // DH_A
    red = jnp.asarray((grp[:, None] == np.arange(LANES)[None, :]) / DH_A, dtype=BF16)
    expd = jnp.asarray(np.arange(LANES)[:, None] == grp[None, :], dtype=BF16)
    qg = (jnp.tile(q_norm_g[0].reshape(-1), H_A) * np.float32(DH_A ** -0.5 * LOG2E)).reshape(1, wa).astype(F32)
    kg = jnp.tile(k_norm_g[0].reshape(-1), H_A).reshape(1, wa).astype(F32)
    ng0 = norm0_g[0].reshape(1, d)
    lng = ln_v_g[0].reshape(1, -1)
    lnb = ln_v_b[0].reshape(1, -1)
    sub_scaled = subln_g[0] * np.float32(1.0 - lam_init)
    subg_cols = jnp.broadcast_to(sub_scaled[:, None], (DV_A, ATTN_TILE)).astype(F32)
    subg_rows = jnp.tile(sub_scaled, H_A).reshape(1, wa).astype(F32)
    bias_tiles = _prompt_bias_tiles(rel_bias, ATTN_TILE)
    bias_past, bias_new = _sample_bias(rel_bias, past, s_len)
    wq_mask = _sample_query_mask(s_len)
    cg = ln_v_g.shape[1] // G_B
    wsp_p, bsp_p = _spatial_weights(spatial_w[0], spatial_b[0], CHUNK_B, cg)
    wsp_s, bsp_s = _spatial_weights(spatial_w[0], spatial_b[0], s_len, cg)

    w1 = w_in1[0]
    w1_bf = jnp.pad(w1, ((0, 0), (0, LANES - rank))).astype(BF16)
    wg = jnp.pad(w_gate[0], ((0, LANES - rank), (0, 0))).astype(BF16)
    bg = b_gate[0].reshape(1, hk)
    ng1 = norm1_g[0].reshape(1, d)
    og = gla_norm_g[0].reshape(1, dv)
    wo1 = w_out1[0].astype(BF16)

    xp = x_prompt.reshape(b * l, d)
    xs = x_sample.reshape(db * s_len, d)
    ck_t = jnp.transpose(cache_k[0], (0, 2, 3, 4, 1)).reshape(n_pool, wa, page)
    cv_rows = cache_v[0]

    qp, kbp, ktp, vp, vtp, gap, ubp, vbp = _proj0(xp, ng0, w0, red, expd, qg, kg, lng, lnb, vb_dtype=BF16,
                                                  prompt_batch=b)
    qs, ks, vs, gas, ubs, vbs = _proj0(xs, ng0, w0, red, expd, qg, kg, lng, lnb, vb_dtype=F32)

    score_bound = (np.float32(1.02 * math.sqrt(DH_A) * LOG2E)
                   * jnp.max(jnp.abs(q_norm_g[0])) * jnp.max(jnp.abs(k_norm_g[0]))
                   + np.float32(LOG2E) * jnp.max(jnp.abs(rel_bias - rel_bias[N_BUCKETS - 1])))
    n_rows = bias_past.shape[0]
    bias_past_halves = jnp.transpose(bias_past.reshape(n_rows, 2, past // 2), (1, 0, 2))
    attn_args = (page_table, qp.reshape(b, l, wa), kbp.reshape(b, l // ATTN_TILE, ATTN_TILE, wa), vtp, lam[0],
                 bias_tiles, subg_cols, qs.reshape(db, s_len, wa), ks.reshape(db, s_len, wa),
                 vs.reshape(db, s_len, wa), ck_t, cv_rows, wq_mask, bias_past_halves, bias_new, subg_rows)
    at_p, at_s = lax.cond(score_bound <= SAFE_SCORE_BOUND,
                          lambda args: _attention(*args, lam_init=lam_init, stable=False),
                          lambda args: _attention(*args, lam_init=lam_init, stable=True),
                          attn_args)

    yp0 = _mix0(xp, at_p.reshape(b * l, wa), gap, ubp, vbp, wsp_p, bsp_p, wo0)
    ys0 = _mix0(xs, at_s.reshape(db * s_len, wa), gas, ubs, vbs, wsp_s, bsp_s, wo0)

    q_scale = np.float32(dk ** -0.5)
    q1p, k1p, v1p, gsp, lap, lamin_p = _proj1(yp0, ng1, w1_bf, wg, bg, hk=hk, wc=wc, q_scale=q_scale)
    q1s, k1s, v1s, gss, las, lamin_s = _proj1(ys0, ng1, w1_bf, wg, bg, hk=hk, wc=wc, q_scale=q_scale)

    gla_p_args = (q1p.reshape(b, l, hk), k1p.reshape(b, l, hk), v1p.reshape(b, l, wc),
                  lap.reshape(b, l, hk), og, jnp.zeros((b, H_C, dk, dv), F32))
    o_p, st_p = lax.cond(-jnp.min(lamin_p) * min(GLA_CHUNK_P, l) <= SAFE_DECAY_BOUND,
                         lambda a: _gla_prompt(*a, bounded=True),
                         lambda a: _gla_prompt(*a, bounded=False), gla_p_args)
    gla_s_args = (q1s, k1s, v1s, las, og, state_gla[0])
    o_s, st_s = lax.cond(-jnp.min(lamin_s) * s_len <= SAFE_DECAY_BOUND,
                         lambda a: _gla_sample(*a, seg=s_len, bounded=True),
                         lambda a: _gla_sample(*a, seg=s_len, bounded=False), gla_s_args)

    yp1 = _out1(yp0, o_p.reshape(b * l, wc), gsp, wo1)
    ys1 = _out1(ys0, o_s, gss, wo1)

    return (
        yp1.reshape(b, l, d),
        ys1.reshape(db, s_len, d),
        jnp.transpose(ktp.reshape(b, H_A, 2, DH_A, l), (0, 4, 1, 2, 3))[None],
        vp.reshape(1, b, l, H_A, DV_A),
        ks.reshape(1, db, s_len, H_A, 2, DH_A),
        vs.reshape(1, db, s_len, H_A, DV_A),
        vbs.reshape(1, db, s_len, -1),
        st_p.reshape(1, b, H_C, dk, dv),
        st_s.reshape(1, db, H_C, dk, dv),
    )
```

```python
import functools
import math

import numpy as np
import jax
import jax.numpy as jnp
from jax import lax
from jax.experimental import pallas as pl
from jax.experimental.pallas import tpu as pltpu

F32 = jnp.float32
BF16 = jnp.bfloat16

H_A = 8
DH_A = 64
DV_A = 2 * DH_A
G_B = 8
CHUNK_B = 128
H_C = 4
GLA_TAU = 16.0
N_BUCKETS = 32
MAX_DISTANCE = 128
EPS = 1e-6
MASK_VALUE = -1e30
LOG2E = 1.4426950408889634
SAFE_SCORE_BOUND = 60.0
SAFE_DECAY_BOUND = 60.0

LANES = 128
SUBLANES = 8
VMEM_LIMIT_BYTES = 56 * 1024 * 1024

TOKEN_TILE = 256
STREAM_TILE = 512
ATTN_TILE = TOKEN_TILE
GLA_CHUNK_P = 128
GLA_GROUP_S = 8


def _cparams(n_axes):
    return pltpu.CompilerParams(
        dimension_semantics=("arbitrary",) * n_axes,
        vmem_limit_bytes=VMEM_LIMIT_BYTES,
    )


def _row_tile(t, preferred):
    tm = min(preferred, t)
    while t % tm:
        tm -= CHUNK_B
    assert tm > 0 and t % tm == 0
    return tm


def _dot(a, b):
    return jnp.dot(a, b, preferred_element_type=F32)


def _dot_nt(a, b):
    return lax.dot_general(a, b, (((1,), (1,)), ((), ())), preferred_element_type=F32)


def _dot_tn(a, b):
    return lax.dot_general(a, b, (((0,), (0,)), ((), ())), preferred_element_type=F32)


def _silu(x):
    return x * jax.nn.sigmoid(x)


def _gelu(x):
    return 0.5 * x * (1.0 + lax.erf(x * np.float32(math.sqrt(0.5))))


def _rms_rows(x, g):
    return x * lax.rsqrt(jnp.mean(x * x, axis=-1, keepdims=True) + EPS) * g


def _split_bf16(x):
    hi = x.astype(BF16)
    lo = (x - hi.astype(F32)).astype(BF16)
    return hi, lo


def _proj0_body(x_ref, ng_ref, w_ref, red_ref, exp_ref, qg_ref, kg_ref, lng_ref, lnb_ref, *out_refs,
                width, prompt):
    if prompt:
        q_ref, kb_ref, kt_ref, v_ref, vt_ref, ga_ref, ub_ref, vb_ref = out_refs
    else:
        q_ref, k_ref, v_ref, ga_ref, ub_ref, vb_ref = out_refs
    xb = _rms_rows(x_ref[...], ng_ref[...]).astype(BF16)

    def seg(i):
        return _dot(xb, w_ref[:, i * width:(i + 1) * width])

    def group_rms(z, g):
        ms = _dot((z * z).astype(BF16), red_ref[...])
        hi, lo = _split_bf16(lax.rsqrt(ms + EPS))
        scale = _dot(hi, exp_ref[...]) + _dot(lo, exp_ref[...])
        return z * scale * g

    q_ref[...] = group_rms(seg(0), qg_ref[...]).astype(q_ref.dtype)
    k = group_rms(seg(1), kg_ref[...])
    v = seg(2)
    v_ref[...] = v
    if prompt:
        kb_ref[...] = k.astype(kb_ref.dtype)
        kt_ref[...] = k.T
        vt_ref[...] = v.T.astype(vt_ref.dtype)
    else:
        k_ref[...] = k
    ga_ref[...] = _silu(seg(3)).astype(ga_ref.dtype)
    ub_ref[...] = (_gelu(seg(4)) * _silu(seg(6))).astype(ub_ref.dtype)
    hv = _gelu(seg(5))
    hc = hv - jnp.mean(hv, axis=-1, keepdims=True)
    vb = hc * lax.rsqrt(jnp.mean(hc * hc, axis=-1, keepdims=True) + EPS)
    vb_ref[...] = (vb * lng_ref[...] + lnb_ref[...]).astype(vb_ref.dtype)


def _proj0(x2d, ng, w_bf, red, expd, qg, kg, lng, lnb, *, vb_dtype, prompt_batch=None):
    t, d = x2d.shape
    n_in = w_bf.shape[1]
    width = n_in // 7
    tm = _row_tile(t, TOKEN_TILE)
    row = lambda i: (i, 0)
    full = lambda i: (0, 0)
    tok = lambda dt: jax.ShapeDtypeStruct((t, width), dt)
    row_spec = pl.BlockSpec((tm, width), row)
    prompt = prompt_batch is not None
    if prompt:
        per = t // prompt_batch // tm
        outs = [tok(BF16), tok(BF16), jax.ShapeDtypeStruct((prompt_batch, width, per * tm), F32), tok(F32),
                jax.ShapeDtypeStruct((prompt_batch, per, width, tm), BF16), tok(BF16), tok(BF16), tok(vb_dtype)]
        out_specs = [row_spec, row_spec,
                     pl.BlockSpec((None, width, tm), lambda i: (i // per, 0, i % per)), row_spec,
                     pl.BlockSpec((None, None, width, tm), lambda i: (i // per, i % per, 0, 0)),
                     row_spec, row_spec, row_spec]
    else:
        outs = [tok(BF16), tok(F32), tok(F32), tok(BF16), tok(BF16), tok(vb_dtype)]
        out_specs = [row_spec] * 6
    return pl.pallas_call(
        functools.partial(_proj0_body, width=width, prompt=prompt),
        grid=(t // tm,),
        in_specs=[
            pl.BlockSpec((tm, d), row),
            pl.BlockSpec((1, d), full),
            pl.BlockSpec((d, n_in), full),
            pl.BlockSpec(red.shape, full),
            pl.BlockSpec(expd.shape, full),
            pl.BlockSpec((1, width), full),
            pl.BlockSpec((1, width), full),
            pl.BlockSpec((1, width), full),
            pl.BlockSpec((1, width), full),
        ],
        out_specs=out_specs,
        out_shape=outs,
        compiler_params=_cparams(1),
        name="proj0",
    )(x2d, ng, w_bf, red, expd, qg, kg, lng, lnb)


def _lambda_value(lam_ref, lam_init):
    lf = lam_ref[...]
    s1 = jnp.sum(lf[0:1] * lf[1:2], axis=-1, keepdims=True)
    s2 = jnp.sum(lf[2:3] * lf[3:4], axis=-1, keepdims=True)
    return jnp.exp(s1) - jnp.exp(s2) + lam_init


def _prompt_attention_step(qi, lam, q_ref, k_ref, vt_ref, bias_ref, subg_ref, o_ref, sa_sc, sb_sc, m_sc, l_sc,
                           acc_sc, *, stable):
    hd = 2 * DH_A
    n_heads = q_ref.shape[1] // hd
    tq = q_ref.shape[0]
    q = q_ref[...]
    lane = lax.broadcasted_iota(jnp.int32, (tq, hd), 1)
    q2t = []
    for hh in range(n_heads):
        qh = q[:, hh * hd:(hh + 1) * hd].astype(F32)
        q2 = jnp.concatenate([jnp.where(lane < DH_A, qh, 0.0), jnp.where(lane >= DH_A, qh, 0.0)], axis=0)
        q2t.append(q2.T.astype(BF16))

    def sublane_partial(x):
        return x.reshape(x.shape[0] // SUBLANES, SUBLANES, x.shape[1]).sum(axis=0)

    last_tile = 2 * qi + 1

    def scores(t, sbuf):
        j = last_tile - t
        for hh in range(n_heads):
            sbuf[hh] = _dot(k_ref[j, :, hh * hd:(hh + 1) * hd], q2t[hh])

    def consume(t, sbuf):
        j = last_tile - t
        kind = jnp.minimum(t, 3)
        ps = []
        for hh in range(n_heads):
            bt = bias_ref[hh, kind]
            s = sbuf[hh] + jnp.concatenate([bt, bt], axis=1)
            if stable:
                m = m_sc[hh]
                m_new = jnp.maximum(m, jnp.max(s, axis=0, keepdims=True))
                alpha = jnp.exp2(m - m_new)
                m_sc[hh] = m_new
                p = jnp.exp2(s - m_new)
                l_sc[hh] = alpha * l_sc[hh] + sublane_partial(p)
                acc_sc[hh] = alpha * acc_sc[hh]
            else:
                p = jnp.exp2(s)
                l_sc[hh] = l_sc[hh] + sublane_partial(p)
            ps.append(p.astype(BF16))
        for hh in range(n_heads):
            acc_sc[hh] = acc_sc[hh] + _dot(vt_ref[j, hh * DV_A:(hh + 1) * DV_A, :], ps[hh])

    m_sc[...] = jnp.full(m_sc.shape, MASK_VALUE, F32)
    l_sc[...] = jnp.zeros(l_sc.shape, F32)
    acc_sc[...] = jnp.zeros(acc_sc.shape, F32)
    n_keys = 2 * qi + 2
    scores(0, sa_sc)

    def body(pair, c):
        t = 2 * pair
        scores(t + 1, sb_sc)
        consume(t, sa_sc)
        scores(t + 2, sa_sc)
        consume(t + 1, sb_sc)
        return c

    lax.fori_loop(0, n_keys // 2 - 1, body, 0)
    scores(n_keys - 1, sb_sc)
    consume(n_keys - 2, sa_sc)
    consume(n_keys - 1, sb_sc)

    outs = []
    for hh in range(n_heads):
        l8, acc = l_sc[hh], acc_sc[hh]
        o_all = acc * (1.0 / jnp.sum(l8, axis=0, keepdims=True))
        o = o_all[:, :tq] - lam * o_all[:, tq:]
        on = o * lax.rsqrt(jnp.mean(o * o, axis=0, keepdims=True) + EPS) * subg_ref[...]
        outs.append(on.T)
    o_ref[...] = jnp.concatenate(outs, axis=1).astype(o_ref.dtype)


def _sample_attention_item(half, slot, sb, lam, qs_ref, kn_ref, vn_ref, mask_ref, bp_ref, bn_ref, subg_ref, os_ref,
                           kbuf, vbuf, sm_sc, sl_sc, sacc_sc, *, key_chunk):
    q = qs_ref[sb]
    s_len = q.shape[0]
    n_rows = mask_ref.shape[0]
    half_rows = H_A * s_len
    wq = jnp.concatenate([q] * (n_rows // s_len), axis=0) * mask_ref[...]
    first = half == 0
    n_chunks = kbuf.shape[2] // key_chunk

    pad_rows = bn_ref.shape[1] - s_len
    kn = jnp.concatenate([kn_ref[sb], jnp.zeros((pad_rows, kn_ref.shape[2]), F32)], axis=0).astype(BF16)
    vn = jnp.concatenate([vn_ref[sb], jnp.zeros((pad_rows, vn_ref.shape[2]), F32)], axis=0).astype(BF16)
    s_parts = [_dot_nt(wq, kn) + bn_ref[...] + jnp.where(first, 0.0, MASK_VALUE).astype(F32)]
    for c in range(n_chunks):
        keys = pl.ds(c * key_chunk, key_chunk)
        s_parts.append(_dot(wq, kbuf[slot, :, keys].astype(BF16)) + bp_ref[half, :, keys])

    m_old = jnp.where(first, MASK_VALUE, sm_sc[sb])
    m = m_old
    for s in s_parts:
        m = jnp.maximum(m, jnp.max(s, axis=1, keepdims=True))
    alpha = jnp.exp2(m_old - m)
    p_parts = [jnp.exp2(s - m) for s in s_parts]
    l = alpha * jnp.where(first, 0.0, sl_sc[sb])
    for p in p_parts:
        l = l + jnp.sum(p, axis=1, keepdims=True)
    sm_sc[sb] = m
    sl_sc[sb] = l
    row = lax.broadcasted_iota(jnp.int32, l.shape, 0)
    scale = jnp.where(row < half_rows, 1.0, -lam) / l

    def head_rows(x, h):
        r0 = h * s_len
        return jnp.concatenate([x[r0:r0 + s_len], x[half_rows + r0:half_rows + r0 + s_len]], axis=0)

    vhs = [vbuf[slot, h].astype(BF16) for h in range(H_A)]
    p_past = jnp.concatenate(p_parts[1:], axis=1) if n_chunks > 1 else p_parts[1]
    pvs = []
    for h in range(H_A):
        pv = _dot(head_rows(p_past, h).astype(BF16), vhs[h])
        pvs.append(pv + _dot(head_rows(p_parts[0], h).astype(BF16), vn[:, h * DV_A:(h + 1) * DV_A]))
    heads = []
    for h in range(H_A):
        acc = head_rows(alpha, h) * jnp.where(first, 0.0, sacc_sc[sb, h]) + pvs[h]
        sacc_sc[sb, h] = acc
        w = acc * head_rows(scale, h)
        o_h = w[:s_len] + w[s_len:]
        heads.append(o_h * lax.rsqrt(jnp.mean(o_h * o_h, axis=-1, keepdims=True) + EPS))
    os_ref[sb] = (jnp.concatenate(heads, axis=1) * subg_ref[...]).astype(os_ref.dtype)


def _attention_body(pt_ref, lam_ref, q_ref, k_ref, vt_ref, bias_ref, subg_ref,
                    qs_ref, kn_ref, vn_ref, mask_ref, bp_ref, bn_ref, subgr_ref, ck_hbm, cv_hbm,
                    o_ref, os_ref,
                    sa_sc, sb_sc, m_sc, l_sc, acc_sc, kbuf, vbuf, sem, sm_sc, sl_sc, sacc_sc,
                    *, lam_init, stable, items_per_step, pages_per_item, page, key_chunk):
    bi, hp, qi = pl.program_id(0), pl.program_id(1), pl.program_id(2)
    step = (bi * pl.num_programs(1) + hp) * pl.num_programs(2) + qi
    n_steps = pl.num_programs(0) * pl.num_programs(1) * pl.num_programs(2)
    ipg = items_per_step

    def item_copies(item, slot):
        seq, half = item // 2, item % 2
        cps = []
        for p in range(pages_per_item):
            pg = pt_ref[seq, half * pages_per_item + p]
            cps.append(pltpu.make_async_copy(ck_hbm.at[pg], kbuf.at[slot, :, pl.ds(p * page, page)],
                                             sem.at[slot, 0]))
            for h in range(H_A):
                cps.append(pltpu.make_async_copy(cv_hbm.at[pg, :, h, :],
                                                 vbuf.at[slot, h, pl.ds(p * page, page), :], sem.at[slot, 1]))
        return cps

    @pl.when(step == 0)
    def _():
        for slot in range(2):
            for c in item_copies(slot, slot):
                c.start()
        sm_sc[...] = jnp.zeros(sm_sc.shape, F32)
        sl_sc[...] = jnp.zeros(sl_sc.shape, F32)
        sacc_sc[...] = jnp.zeros(sacc_sc.shape, F32)

    lam = _lambda_value(lam_ref, lam_init)
    for r in range(ipg):
        item = step * ipg + r
        slot = r % 2
        for c in item_copies(item, slot):
            c.wait()
        _sample_attention_item(jnp.int32(r % 2), slot, r // 2, lam, qs_ref, kn_ref, vn_ref, mask_ref, bp_ref, bn_ref,
                               subgr_ref, os_ref, kbuf, vbuf, sm_sc, sl_sc, sacc_sc, key_chunk=key_chunk)

        @pl.when(item + 2 < n_steps * ipg)
        def _():
            for c in item_copies(item + 2, slot):
                c.start()

    _prompt_attention_step(qi, lam, q_ref, k_ref, vt_ref, bias_ref, subg_ref, o_ref, sa_sc, sb_sc, m_sc, l_sc,
                           acc_sc, stable=stable)


def _attention(page_table, q, k, vt, lam_p, bias_tiles, subg_cols, qs, k_new, v_new, cache_k, cache_v, wq_mask,
               bias_past, bias_new, subg_rows, *, lam_init, stable):
    b, l, w = q.shape
    db, s_len, _ = qs.shape
    tile = ATTN_TILE
    tq = 2 * tile
    n_tiles, n_q = l // tile, l // tq
    hp = 2
    wb = hp * 2 * DH_A
    nhp = H_A // hp
    n_steps = b * nhp * n_q
    n_pages, page = page_table.shape[1], cache_k.shape[2]
    pages_per_item = n_pages // 2
    ipg = (2 * db) // n_steps
    assert l % tq == 0 and n_pages % 2 == 0 and ipg * n_steps == 2 * db and ipg % 2 == 0
    spb = ipg // 2
    hk = pages_per_item * page
    key_chunk = min(512, hk)
    n_rows = wq_mask.shape[0]

    def sample_blk(bi, h, i, pt):
        return ((bi * nhp + h) * n_q + i, 0, 0)

    full2 = lambda bi, h, i, pt: (0, 0)
    grid_spec = pltpu.PrefetchScalarGridSpec(
        num_scalar_prefetch=1,
        grid=(b, nhp, n_q),
        in_specs=[
            pl.BlockSpec(lam_p.shape, full2),
            pl.BlockSpec((None, tq, wb), lambda bi, h, i, pt: (bi, i, h)),
            pl.BlockSpec((None, n_tiles, tile, wb), lambda bi, h, i, pt: (bi, 0, 0, h)),
            pl.BlockSpec((None, n_tiles, wb, tile), lambda bi, h, i, pt: (bi, 0, h, 0)),
            pl.BlockSpec((hp, 4, tile, tq), lambda bi, h, i, pt: (h, 0, 0, 0)),
            pl.BlockSpec((DV_A, tq), full2),
            pl.BlockSpec((spb, s_len, w), sample_blk),
            pl.BlockSpec((spb, s_len, w), sample_blk),
            pl.BlockSpec((spb, s_len, w), sample_blk),
            pl.BlockSpec(wq_mask.shape, full2),
            pl.BlockSpec(bias_past.shape, lambda bi, h, i, pt: (0, 0, 0)),
            pl.BlockSpec(bias_new.shape, full2),
            pl.BlockSpec(subg_rows.shape, full2),
            pl.BlockSpec(memory_space=pl.ANY),
            pl.BlockSpec(memory_space=pl.ANY),
        ],
        out_specs=[pl.BlockSpec((None, tq, wb), lambda bi, h, i, pt: (bi, i, h)),
                   pl.BlockSpec((spb, s_len, w), sample_blk)],
        scratch_shapes=[
            pltpu.VMEM((hp, tile, 2 * tq), F32),
            pltpu.VMEM((hp, tile, 2 * tq), F32),
            pltpu.VMEM((hp, 1, 2 * tq), F32),
            pltpu.VMEM((hp, SUBLANES, 2 * tq), F32),
            pltpu.VMEM((hp, DV_A, 2 * tq), F32),
            pltpu.VMEM((2, w, hk), F32),
            pltpu.VMEM((2, H_A, hk, DV_A), F32),
            pltpu.SemaphoreType.DMA((2, 2)),
            pltpu.VMEM((spb, n_rows, 1), F32),
            pltpu.VMEM((spb, n_rows, 1), F32),
            pltpu.VMEM((spb, H_A, 2 * s_len, DV_A), F32),
        ],
    )
    return pl.pallas_call(
        functools.partial(_attention_body, lam_init=lam_init, stable=stable, items_per_step=ipg,
                          pages_per_item=pages_per_item, page=page, key_chunk=key_chunk),
        grid_spec=grid_spec,
        out_shape=[jax.ShapeDtypeStruct((b, l, w), BF16), jax.ShapeDtypeStruct((db, s_len, w), BF16)],
        compiler_params=_cparams(3),
        name="attention_stable" if stable else "attention",
    )(page_table, lam_p, q, k, vt, bias_tiles, subg_cols, qs, k_new, v_new, wq_mask, bias_past, bias_new,
      subg_rows, cache_k, cache_v)


def _mix0_body(x_ref, at_ref, ga_ref, ub_ref, vb_ref, wsp_ref, bsp_ref, wo_ref, y_ref):
    tm = x_ref.shape[0]
    wa = at_ref.shape[1]
    vb = vb_ref[...].astype(BF16)
    n_sub = tm // CHUNK_B
    cg = vb.shape[1] // G_B
    cols = []
    for g in range(G_B):
        lanes = slice(g * cg, (g + 1) * cg)
        rows = [_dot(wsp_ref[g], vb[sb * CHUNK_B:(sb + 1) * CHUNK_B, lanes]) for sb in range(n_sub)]
        cols.append(jnp.concatenate(rows, axis=0) if n_sub > 1 else rows[0])
    bsp = bsp_ref[...]
    bias = jnp.concatenate([bsp] * n_sub, axis=0) if n_sub > 1 else bsp
    mixed = jnp.concatenate(cols, axis=1) + bias
    ob = (ub_ref[...].astype(F32) * mixed).astype(BF16)
    oa = at_ref[...] * ga_ref[...]
    y_ref[...] = x_ref[...] + _dot(oa, wo_ref[:wa, :]) + _dot(ob, wo_ref[wa:, :])


def _mix0(x2d, attn, gas, ub, vb, wsp, bsp, wo_bf):
    t, d = x2d.shape
    tm = _row_tile(t, STREAM_TILE)
    row = lambda i: (i, 0)
    full = lambda i: (0, 0)
    return pl.pallas_call(
        _mix0_body,
        grid=(t // tm,),
        in_specs=[
            pl.BlockSpec((tm, d), row),
            pl.BlockSpec((tm, attn.shape[1]), row),
            pl.BlockSpec((tm, gas.shape[1]), row),
            pl.BlockSpec((tm, ub.shape[1]), row),
            pl.BlockSpec((tm, vb.shape[1]), row),
            pl.BlockSpec(wsp.shape, lambda i: (0, 0, 0)),
            pl.BlockSpec(bsp.shape, full),
            pl.BlockSpec(wo_bf.shape, full),
        ],
        out_specs=pl.BlockSpec((tm, d), row),
        out_shape=jax.ShapeDtypeStruct((t, d), F32),
        compiler_params=_cparams(1),
        name="mix0",
    )(x2d, attn, gas, ub, vb, wsp, bsp, wo_bf)


def _proj1_body(x_ref, ng_ref, w_ref, wg_ref, bg_ref, q_ref, k_ref, v_ref, gs_ref, la_ref, lamin_ref,
                *, hk, wc, q_scale):
    xb = _rms_rows(x_ref[...], ng_ref[...]).astype(BF16)
    q_ref[...] = _dot(xb, w_ref[:, 0:hk]) * q_scale
    k_ref[...] = _dot(xb, w_ref[:, hk:2 * hk])
    v_ref[...] = _dot(xb, w_ref[:, 2 * hk:2 * hk + wc]).astype(v_ref.dtype)
    gs_ref[...] = _silu(_dot(xb, w_ref[:, 2 * hk + wc:2 * hk + 2 * wc])).astype(gs_ref.dtype)
    za = _dot(xb, w_ref[:, 2 * hk + 2 * wc:]).astype(BF16)
    xg = _dot(za, wg_ref[...]) + bg_ref[...]
    la = (jnp.minimum(xg, 0.0) - jnp.log1p(jnp.exp(-jnp.abs(xg)))) * np.float32(1.0 / GLA_TAU)
    la_ref[...] = la
    lamin_ref[...] = jnp.min(la, axis=0, keepdims=True)


def _proj1(x2d, ng, w_bf, wg_bf, bg, *, hk, wc, q_scale):
    t, d = x2d.shape
    tm = _row_tile(t, STREAM_TILE)
    row = lambda i: (i, 0)
    full = lambda i: (0, 0)
    outs = [
        jax.ShapeDtypeStruct((t, hk), F32),
        jax.ShapeDtypeStruct((t, hk), F32),
        jax.ShapeDtypeStruct((t, wc), BF16),
        jax.ShapeDtypeStruct((t, wc), BF16),
        jax.ShapeDtypeStruct((t, hk), F32),
        jax.ShapeDtypeStruct((t // tm, 1, hk), F32),
    ]
    return pl.pallas_call(
        functools.partial(_proj1_body, hk=hk, wc=wc, q_scale=q_scale),
        grid=(t // tm,),
        in_specs=[
            pl.BlockSpec((tm, d), row),
            pl.BlockSpec((1, d), full),
            pl.BlockSpec(w_bf.shape, full),
            pl.BlockSpec(wg_bf.shape, full),
            pl.BlockSpec((1, hk), full),
        ],
        out_specs=[pl.BlockSpec((tm, hk), row), pl.BlockSpec((tm, hk), row), pl.BlockSpec((tm, wc), row),
                   pl.BlockSpec((tm, wc), row), pl.BlockSpec((tm, hk), row),
                   pl.BlockSpec((None, 1, hk), lambda i: (i, 0, 0))],
        out_shape=outs,
        compiler_params=_cparams(1),
        name="proj1",
    )(x2d, ng, w_bf, wg_bf, bg)


def _segment_cumsum(x, seg):
    n = x.shape[0]
    row = lax.broadcasted_iota(jnp.int32, x.shape, 0) % seg
    d = 1
    while d < seg:
        x = x + jnp.where(row >= d, pltpu.roll(x, d, axis=0), 0.0)
        d *= 2
    return x


def _block_reference(bc, hb):
    n, w = bc.shape
    if hb >= SUBLANES:
        pieces = []
        for blk in range(n // (2 * hb)):
            r = blk * 2 * hb + hb - 1
            pieces.append(jnp.broadcast_to(bc[r:r + 1, :], (2 * hb, w)))
        return jnp.concatenate(pieces, axis=0) if len(pieces) > 1 else pieces[0]
    pos = lax.broadcasted_iota(jnp.int32, bc.shape, 0) % (2 * hb)
    out = bc
    for delta in range(-hb, hb):
        if delta == 0:
            continue
        shifted = pltpu.roll(bc, (-delta) % n, axis=0)
        out = jnp.where(pos == hb - 1 - delta, shifted, out)
    return out


def _gla_intra_scores(q, k, bc, seg, bounded):
    n = q.shape[0]
    ri = lax.broadcasted_iota(jnp.int32, (n, n), 0)
    ci = lax.broadcasted_iota(jnp.int32, (n, n), 1)
    if bounded:
        att = _dot_nt((q * jnp.exp(bc)).astype(BF16), (k * jnp.exp(-bc)).astype(BF16))
        att = jnp.where(ri >= ci, att, 0.0)
        if seg < n:
            att = jnp.where((ri // seg) == (ci // seg), att, 0.0)
        return att
    rowpos = lax.broadcasted_iota(jnp.int32, q.shape, 0)
    att = jnp.where(ri == ci, _dot_nt(q.astype(BF16), k.astype(BF16)), 0.0)
    for hb in _levels(seg):
        ref = _block_reference(bc, hb)
        is_q = (rowpos % (2 * hb)) >= hb
        f = jnp.exp(jnp.where(is_q, bc - ref, ref - bc))
        qt = jnp.where(is_q, q * f, 0.0).astype(BF16)
        kt = jnp.where(is_q, 0.0, k * f).astype(BF16)
        same = (ri // (2 * hb)) == (ci // (2 * hb))
        att = att + jnp.where(same, _dot_nt(qt, kt), 0.0)
    return att


def _levels(seg):
    out, hb = [], seg // 2
    while hb >= 1:
        out.append(hb)
        hb //= 2
    return out


def _rows_scale_matrix(row, n_cols):
    hi = row.astype(BF16).astype(F32)
    r1 = row - hi
    mid = r1.astype(BF16).astype(F32)
    lo = (r1 - mid).astype(BF16).astype(F32)
    pad = jnp.zeros((2 * SUBLANES - 3, row.shape[1]), F32)
    parts = jnp.concatenate([hi, mid, lo, pad], axis=0).astype(BF16)
    return _dot_tn(parts, jnp.ones((2 * SUBLANES, n_cols), BF16))


def _gla_prompt_body(q_ref, k_ref, v_ref, la_ref, og_ref, s0_ref, o_ref, sout_ref, st_sc, *, dk, dv,
                     bounded):
    ci = pl.program_id(1)
    chunk = q_ref.shape[0]

    @pl.when(ci == 0)
    def _():
        st_sc[...] = s0_ref[...]

    kls = [slice(h * dk, (h + 1) * dk) for h in range(H_C)]
    vls = [slice(h * dv, (h + 1) * dv) for h in range(H_C)]
    bcs = [_segment_cumsum(la_ref[:, kl], chunk) for kl in kls]
    atts = [_gla_intra_scores(q_ref[:, kl], k_ref[:, kl], bc, chunk, bounded) for kl, bc in zip(kls, bcs)]
    outs, upds, decays = [], [], []
    for h in range(H_C):
        q, k, v, bc = q_ref[:, kls[h]], k_ref[:, kls[h]], v_ref[:, vls[h]], bcs[h]
        b_last = bc[chunk - 1:chunk, :]
        k_dec = (k * jnp.exp(b_last - bc)).astype(BF16)
        outs.append(_dot((q * jnp.exp(bc)).astype(BF16), st_sc[h].astype(BF16)) + _dot(atts[h].astype(BF16), v))
        upds.append(_dot_tn(k_dec, v))
        decays.append(_rows_scale_matrix(jnp.exp(b_last), dv))
    for h in range(H_C):
        st_sc[h] = st_sc[h] * decays[h] + upds[h]
        o_ref[:, vls[h]] = _rms_rows(outs[h], og_ref[...]).astype(o_ref.dtype)

    @pl.when(ci == pl.num_programs(1) - 1)
    def _():
        sout_ref[...] = st_sc[...]


def _gla_prompt(q, k, v, la, og, s0, *, bounded):
    b, l, hk = q.shape
    wc = v.shape[2]
    dk, dv = hk // H_C, wc // H_C
    chunk = min(GLA_CHUNK_P, l)
    tok = lambda bi, ci: (bi, ci, 0)
    st = lambda bi, ci: (bi, 0, 0, 0)
    return pl.pallas_call(
        functools.partial(_gla_prompt_body, dk=dk, dv=dv, bounded=bounded),
        grid=(b, l // chunk),
        in_specs=[
            pl.BlockSpec((None, chunk, hk), tok),
            pl.BlockSpec((None, chunk, hk), tok),
            pl.BlockSpec((None, chunk, wc), tok),
            pl.BlockSpec((None, chunk, hk), tok),
            pl.BlockSpec((1, dv), lambda bi, ci: (0, 0)),
            pl.BlockSpec((None, H_C, dk, dv), st),
        ],
        out_specs=[pl.BlockSpec((None, chunk, wc), tok), pl.BlockSpec((None, H_C, dk, dv), st)],
        out_shape=[jax.ShapeDtypeStruct((b, l, wc), BF16), jax.ShapeDtypeStruct((b, H_C, dk, dv), F32)],
        scratch_shapes=[pltpu.VMEM((H_C, dk, dv), F32)],
        compiler_params=_cparams(2),
        name="gla_prompt" if bounded else "gla_prompt_any_decay",
    )(q, k, v, la, og, s0)


def _gla_sample_body(q_ref, k_ref, v_ref, la_ref, og_ref, s0_ref, o_ref, sout_ref, *, dk, dv, seg, bounded):
    n_seq = s0_ref.shape[0]
    for h in range(H_C):
        kl = slice(h * dk, (h + 1) * dk)
        vl = slice(h * dv, (h + 1) * dv)
        q, k, la = q_ref[:, kl], k_ref[:, kl], la_ref[:, kl]
        v = v_ref[:, vl]
        bc = _segment_cumsum(la, seg)
        att = _gla_intra_scores(q, k, bc, seg, bounded)
        o_intra = _dot(att.astype(BF16), v)
        q_dec = (q * jnp.exp(bc)).astype(BF16)
        outs = []
        for s in range(n_seq):
            rows = slice(s * seg, (s + 1) * seg)
            st = s0_ref[s, h]
            bl = bc[s * seg + seg - 1:s * seg + seg, :]
            k_dec = (k[rows] * jnp.exp(bl - bc[rows])).astype(BF16)
            outs.append(_dot(q_dec[rows], st.astype(BF16)))
            sout_ref[s, h] = st * _rows_scale_matrix(jnp.exp(bl), dv) + _dot_tn(k_dec, v[rows])
        o = jnp.concatenate(outs, axis=0) + o_intra
        o_ref[:, vl] = _rms_rows(o, og_ref[...]).astype(o_ref.dtype)


def _gla_sample(q, k, v, la, og, s0, *, seg, bounded):
    t, hk = q.shape
    wc = v.shape[1]
    dk, dv = hk // H_C, wc // H_C
    db = t // seg
    grp = min(GLA_GROUP_S, db)
    rows = grp * seg
    tok = lambda i: (i, 0)
    st = lambda i: (i, 0, 0, 0)
    return pl.pallas_call(
        functools.partial(_gla_sample_body, dk=dk, dv=dv, seg=seg, bounded=bounded),
        grid=(db // grp,),
        in_specs=[
            pl.BlockSpec((rows, hk), tok),
            pl.BlockSpec((rows, hk), tok),
            pl.BlockSpec((rows, wc), tok),
            pl.BlockSpec((rows, hk), tok),
            pl.BlockSpec((1, dv), lambda i: (0, 0)),
            pl.BlockSpec((grp, H_C, dk, dv), st),
        ],
        out_specs=[pl.BlockSpec((rows, wc), tok), pl.BlockSpec((grp, H_C, dk, dv), st)],
        out_shape=[jax.ShapeDtypeStruct((t, wc), BF16), jax.ShapeDtypeStruct((db, H_C, dk, dv), F32)],
        compiler_params=_cparams(1),
        name="gla_sample" if bounded else "gla_sample_any_decay",
    )(q, k, v, la, og, s0)


def _out1_body(x_ref, o_ref, gs_ref, w_ref, y_ref):
    y_ref[...] = x_ref[...] + _dot(o_ref[...] * gs_ref[...], w_ref[...])


def _out1(x2d, o, gs, w_bf):
    t, d = x2d.shape
    tm = _row_tile(t, STREAM_TILE)
    row = lambda i: (i, 0)
    return pl.pallas_call(
        _out1_body,
        grid=(t // tm,),
        in_specs=[pl.BlockSpec((tm, d), row), pl.BlockSpec((tm, o.shape[1]), row),
                  pl.BlockSpec((tm, gs.shape[1]), row), pl.BlockSpec(w_bf.shape, lambda i: (0, 0))],
        out_specs=pl.BlockSpec((tm, d), row),
        out_shape=jax.ShapeDtypeStruct((t, d), F32),
        compiler_params=_cparams(1),
        name="out1",
    )(x2d, o, gs, w_bf)


def _seq_minor_body(x_hbm, o_ref, buf, sem):
    s_len = buf.shape[0]
    copies = [pltpu.make_async_copy(x_hbm.at[:, t, :], buf.at[t], sem.at[t]) for t in range(s_len)]
    for c in copies:
        c.start()
    for t, c in enumerate(copies):
        c.wait()
        o_ref[t] = buf[t].T


def _seq_minor(x2d, db, s_len):
    w = x2d.shape[1]
    return pl.pallas_call(
        _seq_minor_body,
        grid=(1,),
        in_specs=[pl.BlockSpec(memory_space=pl.ANY)],
        out_specs=pl.BlockSpec((s_len, w, db), lambda i: (0, 0, 0)),
        out_shape=jax.ShapeDtypeStruct((s_len, w, db), x2d.dtype),
        scratch_shapes=[pltpu.VMEM((s_len, db, w), x2d.dtype), pltpu.SemaphoreType.DMA((s_len,))],
        compiler_params=_cparams(1),
        name="seq_minor",
    )(x2d.reshape(db, s_len, w))


def _t5_bucket_np(dist):
    n = np.maximum(dist, 0)
    max_exact = N_BUCKETS // 2
    nf = np.maximum(n, 1).astype(np.float32)
    ratio = np.log(nf / np.float32(max_exact)) / np.float32(math.log(MAX_DISTANCE / max_exact))
    large = max_exact + (ratio * np.float32(N_BUCKETS - max_exact)).astype(np.int32)
    large = np.minimum(large, N_BUCKETS - 1)
    return np.where(n < max_exact, n, large)


def _bias_by_distance(rel_bias, n):
    buckets = _t5_bucket_np(np.arange(n))
    assert np.all(np.diff(buckets) >= 0)
    cuts = [0] + [int(i) + 1 for i in np.nonzero(np.diff(buckets))[0]] + [n]
    pieces = [jnp.broadcast_to(rel_bias[int(buckets[a])][:, None], (rel_bias.shape[1], e - a))
              for a, e in zip(cuts[:-1], cuts[1:])]
    return jnp.concatenate(pieces, axis=1).astype(F32) * np.float32(LOG2E)


def _toeplitz(w, n_r, n_c):
    h, lw = w.shape
    assert lw == n_r + n_c
    flat = jnp.tile(w, (1, n_r))[:, :n_r * (lw - 1)]
    return flat.reshape(h, n_r, lw - 1)[:, :, n_r - 1:n_r - 1 + n_c]


def _prompt_bias_tiles(rel_bias, tile):
    assert int(_t5_bucket_np(np.array([tile + 1]))[0]) == N_BUCKETS - 1
    tq = 2 * tile
    tab = _bias_by_distance(rel_bias, tq)
    tab = tab - tab[:, tq - 1:]
    h = tab.shape[0]
    by_dist = jnp.concatenate([jnp.full((h, tq - 1), MASK_VALUE, F32), tab, jnp.zeros((h, tile), F32)], axis=1)
    pad = jnp.zeros((h, 1), F32)
    tiles = []
    for t in range(3):
        start = (tq - 1) - (tile - 1) + (t - 1) * tile
        tiles.append(_toeplitz(jnp.concatenate([by_dist[:, start:start + tile + tq - 1], pad], axis=1), tile, tq))
    tiles.append(jnp.zeros((h, tile, tq), F32))
    return jnp.stack(tiles, axis=1)


def _sample_bias(rel_bias, past, s_len):
    n = past + s_len
    rev = _bias_by_distance(rel_bias, n)[:, ::-1]
    h = rev.shape[0]
    bp = jnp.stack([rev[:, s_len - 1 - q:s_len - 1 - q + past] for q in range(s_len)], axis=1)
    small = jnp.concatenate([rev[:, past:], jnp.full((h, LANES), MASK_VALUE, F32)], axis=1)
    bn = jnp.stack([small[:, s_len - 1 - q:s_len - 1 - q + LANES] for q in range(s_len)], axis=1)
    bp = bp.reshape(h * s_len, past)
    bn = bn.reshape(h * s_len, LANES)
    return jnp.concatenate([bp, bp], axis=0), jnp.concatenate([bn, bn], axis=0)


def _sample_query_mask(s_len):
    rows = np.arange(2 * H_A * s_len)
    c = rows // (H_A * s_len)
    h = (rows // s_len) % H_A
    lane_grp = np.arange(H_A * 2 * DH_A) // DH_A
    return jnp.asarray(lane_grp[None, :] == (h * 2 + c)[:, None], dtype=BF16)


def _spatial_weights(sp_w, sp_b, seg, cg):
    g, n, _ = sp_w.shape
    w = sp_w * jnp.tril(jnp.ones((n, n), sp_w.dtype))
    reps = n // seg
    if reps > 1:
        eye = jnp.eye(reps, dtype=sp_w.dtype)
        w = jnp.einsum("ab,gts->gatbs", eye, w[:, :seg, :seg]).reshape(g, n, n)
    bias = jnp.tile(sp_b[:, :seg], (1, reps))
    bias = jnp.repeat(jnp.transpose(bias), cg, axis=1)
    return w.astype(BF16), bias.astype(F32)


def kernel(x_prompt, x_sample, cache_k, cache_v, state_gla, page_table, rel_bias, norm0_g, w_in0, q_norm_g,
           k_norm_g, lam, subln_g, ln_v_g, ln_v_b, spatial_w, spatial_b, w_out0, norm1_g, w_in1, w_gate,
           b_gate, gla_norm_g, w_out1):
    b, l, d = x_prompt.shape
    db, s_len, _ = x_sample.shape
    n_pool, page = cache_k.shape[1], cache_k.shape[2]
    past = page_table.shape[1] * page
    wa = H_A * DV_A
    lam_init = 0.8 - 0.6 * math.exp(-0.3 * 0)
    rank = w_gate.shape[1]
    hk = w_gate.shape[2]
    wc = w_out1.shape[1]
    dk, dv = hk // H_C, wc // H_C

    w0 = w_in0[0].astype(BF16)
    wo0 = w_out0[0].astype(BF16)
    grp = np.arange(wa) // DH_A
    red = jnp.asarray((grp[:, None] == np.arange(LANES)[None, :]) / DH_A, dtype=BF16)
    expd = jnp.asarray(np.arange(LANES)[:, None] == grp[None, :], dtype=BF16)
    qg = (jnp.tile(q_norm_g[0].reshape(-1), H_A) * np.float32(DH_A ** -0.5 * LOG2E)).reshape(1, wa).astype(F32)
    kg = jnp.tile(k_norm_g[0].reshape(-1), H_A).reshape(1, wa).astype(F32)
    ng0 = norm0_g[0].reshape(1, d)
    lng = ln_v_g[0].reshape(1, -1)
    lnb = ln_v_b[0].reshape(1, -1)
    sub_scaled = subln_g[0] * np.float32(1.0 - lam_init)
    subg_cols = jnp.broadcast_to(sub_scaled[:, None], (DV_A, 2 * ATTN_TILE)).astype(F32)
    subg_rows = jnp.tile(sub_scaled, H_A).reshape(1, wa).astype(F32)
    bias_tiles = _prompt_bias_tiles(rel_bias, ATTN_TILE)
    bias_past, bias_new = _sample_bias(rel_bias, past, s_len)
    wq_mask = _sample_query_mask(s_len)
    cg = ln_v_g.shape[1] // G_B
    wsp_p, bsp_p = _spatial_weights(spatial_w[0], spatial_b[0], CHUNK_B, cg)
    wsp_s, bsp_s = _spatial_weights(spatial_w[0], spatial_b[0], s_len, cg)

    w1 = w_in1[0]
    w1_bf = jnp.pad(w1, ((0, 0), (0, LANES - rank))).astype(BF16)
    wg = jnp.pad(w_gate[0], ((0, LANES - rank), (0, 0))).astype(BF16)
    bg = b_gate[0].reshape(1, hk)
    ng1 = norm1_g[0].reshape(1, d)
    og = gla_norm_g[0].reshape(1, dv)
    wo1 = w_out1[0].astype(BF16)

    xp = x_prompt.reshape(b * l, d)
    xs = x_sample.reshape(db * s_len, d)
    ck_t = jnp.transpose(cache_k[0], (0, 2, 3, 4, 1)).reshape(n_pool, wa, page)
    cv_rows = cache_v[0]

    qp, kbp, ktp, vp, vtp, gap, ubp, vbp = _proj0(xp, ng0, w0, red, expd, qg, kg, lng, lnb, vb_dtype=BF16,
                                                  prompt_batch=b)
    qs, ks, vs, gas, ubs, vbs = _proj0(xs, ng0, w0, red, expd, qg, kg, lng, lnb, vb_dtype=F32)

    score_bound = (np.float32(1.02 * math.sqrt(DH_A) * LOG2E)
                   * jnp.max(jnp.abs(q_norm_g[0])) * jnp.max(jnp.abs(k_norm_g[0]))
                   + np.float32(LOG2E) * jnp.max(jnp.abs(rel_bias - rel_bias[N_BUCKETS - 1])))
    n_rows = bias_past.shape[0]
    bias_past_halves = jnp.transpose(bias_past.reshape(n_rows, 2, past // 2), (1, 0, 2))
    attn_args = (page_table, qp.reshape(b, l, wa), kbp.reshape(b, l // ATTN_TILE, ATTN_TILE, wa), vtp, lam[0],
                 bias_tiles, subg_cols, qs.reshape(db, s_len, wa), ks.reshape(db, s_len, wa),
                 vs.reshape(db, s_len, wa), ck_t, cv_rows, wq_mask, bias_past_halves, bias_new, subg_rows)
    at_p, at_s = lax.cond(score_bound <= SAFE_SCORE_BOUND,
                          lambda args: _attention(*args, lam_init=lam_init, stable=False),
                          lambda args: _attention(*args, lam_init=lam_init, stable=True),
                          attn_args)

    yp0 = _mix0(xp, at_p.reshape(b * l, wa), gap, ubp, vbp, wsp_p, bsp_p, wo0)
    ys0 = _mix0(xs, at_s.reshape(db * s_len, wa), gas, ubs, vbs, wsp_s, bsp_s, wo0)

    q_scale = np.float32(dk ** -0.5)
    q1p, k1p, v1p, gsp, lap, lamin_p = _proj1(yp0, ng1, w1_bf, wg, bg, hk=hk, wc=wc, q_scale=q_scale)
    q1s, k1s, v1s, gss, las, lamin_s = _proj1(ys0, ng1, w1_bf, wg, bg, hk=hk, wc=wc, q_scale=q_scale)

    gla_p_args = (q1p.reshape(b, l, hk), k1p.reshape(b, l, hk), v1p.reshape(b, l, wc),
                  lap.reshape(b, l, hk), og, jnp.zeros((b, H_C, dk, dv), F32))
    o_p, st_p = lax.cond(-jnp.min(lamin_p) * min(GLA_CHUNK_P, l) <= SAFE_DECAY_BOUND,
                         lambda a: _gla_prompt(*a, bounded=True),
                         lambda a: _gla_prompt(*a, bounded=False), gla_p_args)
    gla_s_args = (q1s, k1s, v1s, las, og, state_gla[0])
    o_s, st_s = lax.cond(-jnp.min(lamin_s) * s_len <= SAFE_DECAY_BOUND,
                         lambda a: _gla_sample(*a, seg=s_len, bounded=True),
                         lambda a: _gla_sample(*a, seg=s_len, bounded=False), gla_s_args)

    yp1 = _out1(yp0, o_p.reshape(b * l, wc), gsp, wo1)
    ys1 = _out1(ys0, o_s, gss, wo1)

    return (
        yp1.reshape(b, l, d),
        ys1.reshape(db, s_len, d),
        jnp.transpose(ktp.reshape(b, H_A, 2, DH_A, l), (0, 4, 1, 2, 3))[None],
        vp.reshape(1, b, l, H_A, DV_A),
        jnp.transpose(_seq_minor(ks, db, s_len).reshape(s_len, H_A, 2, DH_A, db), (4, 0, 1, 2, 3))[None],
        vs.reshape(1, db, s_len, H_A, DV_A),
        vbs.reshape(1, db, s_len, -1),
        st_p.reshape(1, b, H_C, dk, dv),
        st_s.reshape(1, db, H_C, dk, dv),
    )
```

```python
import functools
import math

import numpy as np
import jax
import jax.numpy as jnp
from jax import lax
from jax.experimental import pallas as pl
from jax.experimental.pallas import tpu as pltpu

F32 = jnp.float32
BF16 = jnp.bfloat16

H_A = 8
DH_A = 64
DV_A = 2 * DH_A
G_B = 8
CHUNK_B = 128
H_C = 4
GLA_TAU = 16.0
N_BUCKETS = 32
MAX_DISTANCE = 128
EPS = 1e-6
MASK_VALUE = -1e30
LOG2E = 1.4426950408889634
SAFE_SCORE_BOUND = 60.0
SAFE_DECAY_BOUND = 60.0

LANES = 128
SUBLANES = 8
VMEM_LIMIT_BYTES = 56 * 1024 * 1024

TOKEN_TILE = 256
STREAM_TILE = 512
ATTN_TILE = TOKEN_TILE
GLA_CHUNK_P = 128
GLA_GROUP_S = 8


def _cparams(n_axes):
    return pltpu.CompilerParams(
        dimension_semantics=("arbitrary",) * n_axes,
        vmem_limit_bytes=VMEM_LIMIT_BYTES,
    )


def _row_tile(t, preferred):
    tm = min(preferred, t)
    while t % tm:
        tm -= CHUNK_B
    assert tm > 0 and t % tm == 0
    return tm


def _dot(a, b):
    return jnp.dot(a, b, preferred_element_type=F32)


def _dot_nt(a, b):
    return lax.dot_general(a, b, (((1,), (1,)), ((), ())), preferred_element_type=F32)


def _dot_tn(a, b):
    return lax.dot_general(a, b, (((0,), (0,)), ((), ())), preferred_element_type=F32)


def _silu(x):
    return x * jax.nn.sigmoid(x)


def _gelu(x):
    return 0.5 * x * (1.0 + lax.erf(x * np.float32(math.sqrt(0.5))))


def _rms_rows(x, g):
    return x * lax.rsqrt(jnp.mean(x * x, axis=-1, keepdims=True) + EPS) * g


def _split_bf16(x):
    hi = x.astype(BF16)
    lo = (x - hi.astype(F32)).astype(BF16)
    return hi, lo


def _proj0_body(x_ref, ng_ref, w_ref, red_ref, exp_ref, qg_ref, kg_ref, lng_ref, lnb_ref, *out_refs,
                width, prompt):
    if prompt:
        q_ref, kb_ref, kt_ref, v_ref, vt_ref, ga_ref, ub_ref, vb_ref = out_refs
    else:
        q_ref, k_ref, v_ref, ga_ref, ub_ref, vb_ref = out_refs
    xb = _rms_rows(x_ref[...], ng_ref[...]).astype(BF16)

    def seg(i):
        return _dot(xb, w_ref[:, i * width:(i + 1) * width])

    def group_rms(z, g):
        ms = _dot((z * z).astype(BF16), red_ref[...])
        hi, lo = _split_bf16(lax.rsqrt(ms + EPS))
        scale = _dot(hi, exp_ref[...]) + _dot(lo, exp_ref[...])
        return z * scale * g

    q_ref[...] = group_rms(seg(0), qg_ref[...]).astype(q_ref.dtype)
    k = group_rms(seg(1), kg_ref[...])
    v = seg(2)
    v_ref[...] = v
    if prompt:
        kb_ref[...] = k.astype(kb_ref.dtype)
        kt_ref[...] = k.T
        vt_ref[...] = v.T.astype(vt_ref.dtype)
    else:
        k_ref[...] = k
    ga_ref[...] = _silu(seg(3)).astype(ga_ref.dtype)
    ub_ref[...] = (_gelu(seg(4)) * _silu(seg(6))).astype(ub_ref.dtype)
    hv = _gelu(seg(5))
    hc = hv - jnp.mean(hv, axis=-1, keepdims=True)
    vb = hc * lax.rsqrt(jnp.mean(hc * hc, axis=-1, keepdims=True) + EPS)
    vb_ref[...] = (vb * lng_ref[...] + lnb_ref[...]).astype(vb_ref.dtype)


def _proj0(x2d, ng, w_bf, red, expd, qg, kg, lng, lnb, *, vb_dtype, prompt_batch=None):
    t, d = x2d.shape
    n_in = w_bf.shape[1]
    width = n_in // 7
    tm = _row_tile(t, TOKEN_TILE)
    row = lambda i: (i, 0)
    full = lambda i: (0, 0)
    tok = lambda dt: jax.ShapeDtypeStruct((t, width), dt)
    row_spec = pl.BlockSpec((tm, width), row)
    prompt = prompt_batch is not None
    if prompt:
        per = t // prompt_batch // tm
        outs = [tok(BF16), tok(BF16), jax.ShapeDtypeStruct((prompt_batch, width, per * tm), F32), tok(F32),
                jax.ShapeDtypeStruct((prompt_batch, per, width, tm), BF16), tok(BF16), tok(BF16), tok(vb_dtype)]
        out_specs = [row_spec, row_spec,
                     pl.BlockSpec((None, width, tm), lambda i: (i // per, 0, i % per)), row_spec,
                     pl.BlockSpec((None, None, width, tm), lambda i: (i // per, i % per, 0, 0)),
                     row_spec, row_spec, row_spec]
    else:
        outs = [tok(BF16), tok(F32), tok(F32), tok(BF16), tok(BF16), tok(vb_dtype)]
        out_specs = [row_spec] * 6
    return pl.pallas_call(
        functools.partial(_proj0_body, width=width, prompt=prompt),
        grid=(t // tm,),
        in_specs=[
            pl.BlockSpec((tm, d), row),
            pl.BlockSpec((1, d), full),
            pl.BlockSpec((d, n_in), full),
            pl.BlockSpec(red.shape, full),
            pl.BlockSpec(expd.shape, full),
            pl.BlockSpec((1, width), full),
            pl.BlockSpec((1, width), full),
            pl.BlockSpec((1, width), full),
            pl.BlockSpec((1, width), full),
        ],
        out_specs=out_specs,
        out_shape=outs,
        compiler_params=_cparams(1),
        name="proj0",
    )(x2d, ng, w_bf, red, expd, qg, kg, lng, lnb)


def _lambda_value(lam_ref, lam_init):
    lf = lam_ref[...]
    s1 = jnp.sum(lf[0:1] * lf[1:2], axis=-1, keepdims=True)
    s2 = jnp.sum(lf[2:3] * lf[3:4], axis=-1, keepdims=True)
    return jnp.exp(s1) - jnp.exp(s2) + lam_init


def _prompt_attention_step(qi, lam, q_ref, k_ref, vt_ref, bias_ref, subg_ref, o_ref, sa_sc, sb_sc, m_sc, l_sc,
                           acc_sc, *, stable):
    hd = 2 * DH_A
    n_heads = q_ref.shape[1] // hd
    tq = q_ref.shape[0]
    q = q_ref[...]
    lane = lax.broadcasted_iota(jnp.int32, (tq, hd), 1)
    q2t = []
    for hh in range(n_heads):
        qh = q[:, hh * hd:(hh + 1) * hd].astype(F32)
        q2 = jnp.concatenate([jnp.where(lane < DH_A, qh, 0.0), jnp.where(lane >= DH_A, qh, 0.0)], axis=0)
        q2t.append(q2.T.astype(BF16))

    def sublane_partial(x):
        return x.reshape(x.shape[0] // SUBLANES, SUBLANES, x.shape[1]).sum(axis=0)

    last_tile = 2 * qi + 1

    def scores(t, sbuf):
        j = last_tile - t
        for hh in range(n_heads):
            sbuf[hh] = _dot(k_ref[j, :, hh * hd:(hh + 1) * hd], q2t[hh])

    def consume(t, sbuf):
        j = last_tile - t
        kind = jnp.minimum(t, 3)
        ps = []
        for hh in range(n_heads):
            bt = bias_ref[hh, kind]
            s = sbuf[hh] + jnp.concatenate([bt, bt], axis=1)
            if stable:
                m = m_sc[hh]
                m_new = jnp.maximum(m, jnp.max(s, axis=0, keepdims=True))
                alpha = jnp.exp2(m - m_new)
                m_sc[hh] = m_new
                p = jnp.exp2(s - m_new)
                l_sc[hh] = alpha * l_sc[hh] + sublane_partial(p)
                acc_sc[hh] = alpha * acc_sc[hh]
            else:
                p = jnp.exp2(s)
                l_sc[hh] = l_sc[hh] + sublane_partial(p)
            ps.append(p.astype(BF16))
        for hh in range(n_heads):
            acc_sc[hh] = acc_sc[hh] + _dot(vt_ref[j, hh * DV_A:(hh + 1) * DV_A, :], ps[hh])

    m_sc[...] = jnp.full(m_sc.shape, MASK_VALUE, F32)
    l_sc[...] = jnp.zeros(l_sc.shape, F32)
    acc_sc[...] = jnp.zeros(acc_sc.shape, F32)
    n_keys = 2 * qi + 2
    scores(0, sa_sc)

    def body(pair, c):
        t = 2 * pair
        scores(t + 1, sb_sc)
        consume(t, sa_sc)
        scores(t + 2, sa_sc)
        consume(t + 1, sb_sc)
        return c

    lax.fori_loop(0, n_keys // 2 - 1, body, 0)
    scores(n_keys - 1, sb_sc)
    consume(n_keys - 2, sa_sc)
    consume(n_keys - 1, sb_sc)

    outs = []
    for hh in range(n_heads):
        l8, acc = l_sc[hh], acc_sc[hh]
        o_all = acc * (1.0 / jnp.sum(l8, axis=0, keepdims=True))
        o = o_all[:, :tq] - lam * o_all[:, tq:]
        on = o * lax.rsqrt(jnp.mean(o * o, axis=0, keepdims=True) + EPS) * subg_ref[...]
        outs.append(on.T)
    o_ref[...] = jnp.concatenate(outs, axis=1).astype(o_ref.dtype)


def _sample_attention_item(half, slot, sb, lam, qs_ref, kn_ref, vn_ref, mask_ref, bp_ref, bn_ref, subg_ref, os_ref,
                           kbuf, vbuf, sm_sc, sl_sc, sacc_sc, *, key_chunk):
    q = qs_ref[sb]
    s_len = q.shape[0]
    n_rows = mask_ref.shape[0]
    half_rows = H_A * s_len
    wq = jnp.concatenate([q] * (n_rows // s_len), axis=0) * mask_ref[...]
    first = half == 0
    n_chunks = kbuf.shape[2] // key_chunk

    pad_rows = bn_ref.shape[1] - s_len
    kn = jnp.concatenate([kn_ref[sb], jnp.zeros((pad_rows, kn_ref.shape[2]), F32)], axis=0).astype(BF16)
    vn = jnp.concatenate([vn_ref[sb], jnp.zeros((pad_rows, vn_ref.shape[2]), F32)], axis=0).astype(BF16)
    s_parts = [_dot_nt(wq, kn) + bn_ref[...] + jnp.where(first, 0.0, MASK_VALUE).astype(F32)]
    for c in range(n_chunks):
        keys = pl.ds(c * key_chunk, key_chunk)
        s_parts.append(_dot(wq, kbuf[slot, :, keys].astype(BF16)) + bp_ref[half, :, keys])

    m_old = jnp.where(first, MASK_VALUE, sm_sc[sb])
    m = m_old
    for s in s_parts:
        m = jnp.maximum(m, jnp.max(s, axis=1, keepdims=True))
    alpha = jnp.exp2(m_old - m)
    p_parts = [jnp.exp2(s - m) for s in s_parts]
    l = alpha * jnp.where(first, 0.0, sl_sc[sb])
    for p in p_parts:
        l = l + jnp.sum(p, axis=1, keepdims=True)
    sm_sc[sb] = m
    sl_sc[sb] = l
    row = lax.broadcasted_iota(jnp.int32, l.shape, 0)
    scale = jnp.where(row < half_rows, 1.0, -lam) / l

    def head_rows(x, h):
        r0 = h * s_len
        return jnp.concatenate([x[r0:r0 + s_len], x[half_rows + r0:half_rows + r0 + s_len]], axis=0)

    vhs = [vbuf[slot, h].astype(BF16) for h in range(H_A)]
    p_past = jnp.concatenate(p_parts[1:], axis=1) if n_chunks > 1 else p_parts[1]
    pvs = []
    for h in range(H_A):
        pv = _dot(head_rows(p_past, h).astype(BF16), vhs[h])
        pvs.append(pv + _dot(head_rows(p_parts[0], h).astype(BF16), vn[:, h * DV_A:(h + 1) * DV_A]))
    heads = []
    for h in range(H_A):
        acc = head_rows(alpha, h) * jnp.where(first, 0.0, sacc_sc[sb, h]) + pvs[h]
        sacc_sc[sb, h] = acc
        w = acc * head_rows(scale, h)
        o_h = w[:s_len] + w[s_len:]
        heads.append(o_h * lax.rsqrt(jnp.mean(o_h * o_h, axis=-1, keepdims=True) + EPS))
    os_ref[sb] = (jnp.concatenate(heads, axis=1) * subg_ref[...]).astype(os_ref.dtype)


def _attention_body(pt_ref, lam_ref, q_ref, k_ref, vt_ref, bias_ref, subg_ref,
                    qs_ref, kn_ref, vn_ref, mask_ref, bp_ref, bn_ref, subgr_ref, ck_hbm, cv_hbm,
                    o_ref, os_ref,
                    sa_sc, sb_sc, m_sc, l_sc, acc_sc, kbuf, vbuf, sem, sm_sc, sl_sc, sacc_sc,
                    *, lam_init, stable, items_per_step, pages_per_item, page, key_chunk):
    bi, hp, qi = pl.program_id(0), pl.program_id(1), pl.program_id(2)
    step = (bi * pl.num_programs(1) + hp) * pl.num_programs(2) + qi
    n_steps = pl.num_programs(0) * pl.num_programs(1) * pl.num_programs(2)
    ipg = items_per_step

    def item_copies(item, slot):
        seq, half = item // 2, item % 2
        cps = []
        for p in range(pages_per_item):
            pg = pt_ref[seq, half * pages_per_item + p]
            cps.append(pltpu.make_async_copy(ck_hbm.at[pg], kbuf.at[slot, :, pl.ds(p * page, page)],
                                             sem.at[slot, 0]))
            for h in range(H_A):
                cps.append(pltpu.make_async_copy(cv_hbm.at[pg, :, h, :],
                                                 vbuf.at[slot, h, pl.ds(p * page, page), :], sem.at[slot, 1]))
        return cps

    @pl.when(step == 0)
    def _():
        for slot in range(2):
            for c in item_copies(slot, slot):
                c.start()
        sm_sc[...] = jnp.zeros(sm_sc.shape, F32)
        sl_sc[...] = jnp.zeros(sl_sc.shape, F32)
        sacc_sc[...] = jnp.zeros(sacc_sc.shape, F32)

    lam = _lambda_value(lam_ref, lam_init)
    for r in range(ipg):
        item = step * ipg + r
        slot = r % 2
        for c in item_copies(item, slot):
            c.wait()
        _sample_attention_item(jnp.int32(r % 2), slot, r // 2, lam, qs_ref, kn_ref, vn_ref, mask_ref, bp_ref, bn_ref,
                               subgr_ref, os_ref, kbuf, vbuf, sm_sc, sl_sc, sacc_sc, key_chunk=key_chunk)

        @pl.when(item + 2 < n_steps * ipg)
        def _():
            for c in item_copies(item + 2, slot):
                c.start()

    _prompt_attention_step(qi, lam, q_ref, k_ref, vt_ref, bias_ref, subg_ref, o_ref, sa_sc, sb_sc, m_sc, l_sc,
                           acc_sc, stable=stable)


def _attention(page_table, q, k, vt, lam_p, bias_tiles, subg_cols, qs, k_new, v_new, cache_k, cache_v, wq_mask,
               bias_past, bias_new, subg_rows, *, lam_init, stable):
    b, l, w = q.shape
    db, s_len, _ = qs.shape
    tile = ATTN_TILE
    tq = 2 * tile
    n_tiles, n_q = l // tile, l // tq
    hp = 2
    wb = hp * 2 * DH_A
    nhp = H_A // hp
    n_steps = b * nhp * n_q
    n_pages, page = page_table.shape[1], cache_k.shape[2]
    pages_per_item = n_pages // 2
    ipg = (2 * db) // n_steps
    assert l % tq == 0 and n_pages % 2 == 0 and ipg * n_steps == 2 * db and ipg % 2 == 0
    spb = ipg // 2
    hk = pages_per_item * page
    key_chunk = min(512, hk)
    n_rows = wq_mask.shape[0]

    def sample_blk(bi, h, i, pt):
        return ((bi * nhp + h) * n_q + i, 0, 0)

    full2 = lambda bi, h, i, pt: (0, 0)
    grid_spec = pltpu.PrefetchScalarGridSpec(
        num_scalar_prefetch=1,
        grid=(b, nhp, n_q),
        in_specs=[
            pl.BlockSpec(lam_p.shape, full2),
            pl.BlockSpec((None, tq, wb), lambda bi, h, i, pt: (bi, i, h)),
            pl.BlockSpec((None, n_tiles, tile, wb), lambda bi, h, i, pt: (bi, 0, 0, h)),
            pl.BlockSpec((None, n_tiles, wb, tile), lambda bi, h, i, pt: (bi, 0, h, 0)),
            pl.BlockSpec((hp, 4, tile, tq), lambda bi, h, i, pt: (h, 0, 0, 0)),
            pl.BlockSpec((DV_A, tq), full2),
            pl.BlockSpec((spb, s_len, w), sample_blk),
            pl.BlockSpec((spb, s_len, w), sample_blk),
            pl.BlockSpec((spb, s_len, w), sample_blk),
            pl.BlockSpec(wq_mask.shape, full2),
            pl.BlockSpec(bias_past.shape, lambda bi, h, i, pt: (0, 0, 0)),
            pl.BlockSpec(bias_new.shape, full2),
            pl.BlockSpec(subg_rows.shape, full2),
            pl.BlockSpec(memory_space=pl.ANY),
            pl.BlockSpec(memory_space=pl.ANY),
        ],
        out_specs=[pl.BlockSpec((None, tq, wb), lambda bi, h, i, pt: (bi, i, h)),
                   pl.BlockSpec((spb, s_len, w), sample_blk)],
        scratch_shapes=[
            pltpu.VMEM((hp, tile, 2 * tq), F32),
            pltpu.VMEM((hp, tile, 2 * tq), F32),
            pltpu.VMEM((hp, 1, 2 * tq), F32),
            pltpu.VMEM((hp, SUBLANES, 2 * tq), F32),
            pltpu.VMEM((hp, DV_A, 2 * tq), F32),
            pltpu.VMEM((2, w, hk), F32),
            pltpu.VMEM((2, H_A, hk, DV_A), F32),
            pltpu.SemaphoreType.DMA((2, 2)),
            pltpu.VMEM((spb, n_rows, 1), F32),
            pltpu.VMEM((spb, n_rows, 1), F32),
            pltpu.VMEM((spb, H_A, 2 * s_len, DV_A), F32),
        ],
    )
    return pl.pallas_call(
        functools.partial(_attention_body, lam_init=lam_init, stable=stable, items_per_step=ipg,
                          pages_per_item=pages_per_item, page=page, key_chunk=key_chunk),
        grid_spec=grid_spec,
        out_shape=[jax.ShapeDtypeStruct((b, l, w), BF16), jax.ShapeDtypeStruct((db, s_len, w), BF16)],
        compiler_params=_cparams(3),
        name="attention_stable" if stable else "attention",
    )(page_table, lam_p, q, k, vt, bias_tiles, subg_cols, qs, k_new, v_new, wq_mask, bias_past, bias_new,
      subg_rows, cache_k, cache_v)


def _mix0_body(x_ref, at_ref, ga_ref, ub_ref, vb_ref, wsp_ref, bsp_ref, wo_ref, y_ref):
    tm = x_ref.shape[0]
    wa = at_ref.shape[1]
    vb = vb_ref[...].astype(BF16)
    n_sub = tm // CHUNK_B
    cg = vb.shape[1] // G_B
    cols = []
    for g in range(G_B):
        lanes = slice(g * cg, (g + 1) * cg)
        rows = [_dot(wsp_ref[g], vb[sb * CHUNK_B:(sb + 1) * CHUNK_B, lanes]) for sb in range(n_sub)]
        cols.append(jnp.concatenate(rows, axis=0) if n_sub > 1 else rows[0])
    bsp = bsp_ref[...]
    bias = jnp.concatenate([bsp] * n_sub, axis=0) if n_sub > 1 else bsp
    mixed = jnp.concatenate(cols, axis=1) + bias
    ob = (ub_ref[...].astype(F32) * mixed).astype(BF16)
    oa = at_ref[...] * ga_ref[...]
    y_ref[...] = x_ref[...] + _dot(oa, wo_ref[:wa, :]) + _dot(ob, wo_ref[wa:, :])


def _mix0(x2d, attn, gas, ub, vb, wsp, bsp, wo_bf):
    t, d = x2d.shape
    tm = _row_tile(t, STREAM_TILE)
    row = lambda i: (i, 0)
    full = lambda i: (0, 0)
    return pl.pallas_call(
        _mix0_body,
        grid=(t // tm,),
        in_specs=[
            pl.BlockSpec((tm, d), row),
            pl.BlockSpec((tm, attn.shape[1]), row),
            pl.BlockSpec((tm, gas.shape[1]), row),
            pl.BlockSpec((tm, ub.shape[1]), row),
            pl.BlockSpec((tm, vb.shape[1]), row),
            pl.BlockSpec(wsp.shape, lambda i: (0, 0, 0)),
            pl.BlockSpec(bsp.shape, full),
            pl.BlockSpec(wo_bf.shape, full),
        ],
        out_specs=pl.BlockSpec((tm, d), row),
        out_shape=jax.ShapeDtypeStruct((t, d), F32),
        compiler_params=_cparams(1),
        name="mix0",
    )(x2d, attn, gas, ub, vb, wsp, bsp, wo_bf)


def _proj1_body(x_ref, ng_ref, w_ref, wg_ref, bg_ref, q_ref, k_ref, v_ref, gs_ref, la_ref, lamin_ref,
                *, hk, wc, q_scale):
    xb = _rms_rows(x_ref[...], ng_ref[...]).astype(BF16)
    q_ref[...] = _dot(xb, w_ref[:, 0:hk]) * q_scale
    k_ref[...] = _dot(xb, w_ref[:, hk:2 * hk])
    v_ref[...] = _dot(xb, w_ref[:, 2 * hk:2 * hk + wc]).astype(v_ref.dtype)
    gs_ref[...] = _silu(_dot(xb, w_ref[:, 2 * hk + wc:2 * hk + 2 * wc])).astype(gs_ref.dtype)
    za = _dot(xb, w_ref[:, 2 * hk + 2 * wc:]).astype(BF16)
    xg = _dot(za, wg_ref[...]) + bg_ref[...]
    la = (jnp.minimum(xg, 0.0) - jnp.log1p(jnp.exp(-jnp.abs(xg)))) * np.float32(1.0 / GLA_TAU)
    la_ref[...] = la
    lamin_ref[...] = jnp.min(la, axis=0, keepdims=True)


def _proj1(x2d, ng, w_bf, wg_bf, bg, *, hk, wc, q_scale):
    t, d = x2d.shape
    tm = _row_tile(t, STREAM_TILE)
    row = lambda i: (i, 0)
    full = lambda i: (0, 0)
    outs = [
        jax.ShapeDtypeStruct((t, hk), F32),
        jax.ShapeDtypeStruct((t, hk), F32),
        jax.ShapeDtypeStruct((t, wc), BF16),
        jax.ShapeDtypeStruct((t, wc), BF16),
        jax.ShapeDtypeStruct((t, hk), F32),
        jax.ShapeDtypeStruct((t // tm, 1, hk), F32),
    ]
    return pl.pallas_call(
        functools.partial(_proj1_body, hk=hk, wc=wc, q_scale=q_scale),
        grid=(t // tm,),
        in_specs=[
            pl.BlockSpec((tm, d), row),
            pl.BlockSpec((1, d), full),
            pl.BlockSpec(w_bf.shape, full),
            pl.BlockSpec(wg_bf.shape, full),
            pl.BlockSpec((1, hk), full),
        ],
        out_specs=[pl.BlockSpec((tm, hk), row), pl.BlockSpec((tm, hk), row), pl.BlockSpec((tm, wc), row),
                   pl.BlockSpec((tm, wc), row), pl.BlockSpec((tm, hk), row),
                   pl.BlockSpec((None, 1, hk), lambda i: (i, 0, 0))],
        out_shape=outs,
        compiler_params=_cparams(1),
        name="proj1",
    )(x2d, ng, w_bf, wg_bf, bg)


def _segment_cumsum(x, seg):
    n = x.shape[0]
    row = lax.broadcasted_iota(jnp.int32, x.shape, 0) % seg
    d = 1
    while d < seg:
        x = x + jnp.where(row >= d, pltpu.roll(x, d, axis=0), 0.0)
        d *= 2
    return x


def _block_reference(bc, hb):
    n, w = bc.shape
    if hb >= SUBLANES:
        pieces = []
        for blk in range(n // (2 * hb)):
            r = blk * 2 * hb + hb - 1
            pieces.append(jnp.broadcast_to(bc[r:r + 1, :], (2 * hb, w)))
        return jnp.concatenate(pieces, axis=0) if len(pieces) > 1 else pieces[0]
    pos = lax.broadcasted_iota(jnp.int32, bc.shape, 0) % (2 * hb)
    out = bc
    for delta in range(-hb, hb):
        if delta == 0:
            continue
        shifted = pltpu.roll(bc, (-delta) % n, axis=0)
        out = jnp.where(pos == hb - 1 - delta, shifted, out)
    return out


def _gla_intra_scores(q, k, bc, seg, bounded):
    n = q.shape[0]
    ri = lax.broadcasted_iota(jnp.int32, (n, n), 0)
    ci = lax.broadcasted_iota(jnp.int32, (n, n), 1)
    if bounded:
        att = _dot_nt((q * jnp.exp(bc)).astype(BF16), (k * jnp.exp(-bc)).astype(BF16))
        att = jnp.where(ri >= ci, att, 0.0)
        if seg < n:
            att = jnp.where((ri // seg) == (ci // seg), att, 0.0)
        return att
    rowpos = lax.broadcasted_iota(jnp.int32, q.shape, 0)
    att = jnp.where(ri == ci, _dot_nt(q.astype(BF16), k.astype(BF16)), 0.0)
    for hb in _levels(seg):
        ref = _block_reference(bc, hb)
        is_q = (rowpos % (2 * hb)) >= hb
        f = jnp.exp(jnp.where(is_q, bc - ref, ref - bc))
        qt = jnp.where(is_q, q * f, 0.0).astype(BF16)
        kt = jnp.where(is_q, 0.0, k * f).astype(BF16)
        same = (ri // (2 * hb)) == (ci // (2 * hb))
        att = att + jnp.where(same, _dot_nt(qt, kt), 0.0)
    return att


def _levels(seg):
    out, hb = [], seg // 2
    while hb >= 1:
        out.append(hb)
        hb //= 2
    return out


def _rows_scale_matrix(row, n_cols):
    hi = row.astype(BF16).astype(F32)
    r1 = row - hi
    mid = r1.astype(BF16).astype(F32)
    lo = (r1 - mid).astype(BF16).astype(F32)
    pad = jnp.zeros((2 * SUBLANES - 3, row.shape[1]), F32)
    parts = jnp.concatenate([hi, mid, lo, pad], axis=0).astype(BF16)
    return _dot_tn(parts, jnp.ones((2 * SUBLANES, n_cols), BF16))


def _gla_prompt_body(q_ref, k_ref, v_ref, la_ref, og_ref, s0_ref, o_ref, sout_ref, st_sc, *, dk, dv,
                     bounded):
    ci = pl.program_id(1)
    chunk = q_ref.shape[0]

    @pl.when(ci == 0)
    def _():
        st_sc[...] = s0_ref[...]

    kls = [slice(h * dk, (h + 1) * dk) for h in range(H_C)]
    vls = [slice(h * dv, (h + 1) * dv) for h in range(H_C)]
    bcs = [_segment_cumsum(la_ref[:, kl], chunk) for kl in kls]
    atts = [_gla_intra_scores(q_ref[:, kl], k_ref[:, kl], bc, chunk, bounded) for kl, bc in zip(kls, bcs)]
    outs, upds, decays = [], [], []
    for h in range(H_C):
        q, k, v, bc = q_ref[:, kls[h]], k_ref[:, kls[h]], v_ref[:, vls[h]], bcs[h]
        b_last = bc[chunk - 1:chunk, :]
        k_dec = (k * jnp.exp(b_last - bc)).astype(BF16)
        outs.append(_dot((q * jnp.exp(bc)).astype(BF16), st_sc[h].astype(BF16)) + _dot(atts[h].astype(BF16), v))
        upds.append(_dot_tn(k_dec, v))
        decays.append(_rows_scale_matrix(jnp.exp(b_last), dv))
    for h in range(H_C):
        st_sc[h] = st_sc[h] * decays[h] + upds[h]
        o_ref[:, vls[h]] = _rms_rows(outs[h], og_ref[...]).astype(o_ref.dtype)

    @pl.when(ci == pl.num_programs(1) - 1)
    def _():
        sout_ref[...] = st_sc[...]


def _gla_prompt(q, k, v, la, og, s0, *, bounded):
    b, l, hk = q.shape
    wc = v.shape[2]
    dk, dv = hk // H_C, wc // H_C
    chunk = min(GLA_CHUNK_P, l)
    tok = lambda bi, ci: (bi, ci, 0)
    st = lambda bi, ci: (bi, 0, 0, 0)
    return pl.pallas_call(
        functools.partial(_gla_prompt_body, dk=dk, dv=dv, bounded=bounded),
        grid=(b, l // chunk),
        in_specs=[
            pl.BlockSpec((None, chunk, hk), tok),
            pl.BlockSpec((None, chunk, hk), tok),
            pl.BlockSpec((None, chunk, wc), tok),
            pl.BlockSpec((None, chunk, hk), tok),
            pl.BlockSpec((1, dv), lambda bi, ci: (0, 0)),
            pl.BlockSpec((None, H_C, dk, dv), st),
        ],
        out_specs=[pl.BlockSpec((None, chunk, wc), tok), pl.BlockSpec((None, H_C, dk, dv), st)],
        out_shape=[jax.ShapeDtypeStruct((b, l, wc), BF16), jax.ShapeDtypeStruct((b, H_C, dk, dv), F32)],
        scratch_shapes=[pltpu.VMEM((H_C, dk, dv), F32)],
        compiler_params=_cparams(2),
        name="gla_prompt" if bounded else "gla_prompt_any_decay",
    )(q, k, v, la, og, s0)


def _gla_sample_body(q_ref, k_ref, v_ref, la_ref, og_ref, s0_ref, o_ref, sout_ref, *, dk, dv, seg, bounded):
    n_seq = s0_ref.shape[0]
    for h in range(H_C):
        kl = slice(h * dk, (h + 1) * dk)
        vl = slice(h * dv, (h + 1) * dv)
        q, k, la = q_ref[:, kl], k_ref[:, kl], la_ref[:, kl]
        v = v_ref[:, vl]
        bc = _segment_cumsum(la, seg)
        att = _gla_intra_scores(q, k, bc, seg, bounded)
        o_intra = _dot(att.astype(BF16), v)
        q_dec = (q * jnp.exp(bc)).astype(BF16)
        outs = []
        for s in range(n_seq):
            rows = slice(s * seg, (s + 1) * seg)
            st = s0_ref[s, h]
            bl = bc[s * seg + seg - 1:s * seg + seg, :]
            k_dec = (k[rows] * jnp.exp(bl - bc[rows])).astype(BF16)
            outs.append(_dot(q_dec[rows], st.astype(BF16)))
            sout_ref[s, h] = st * _rows_scale_matrix(jnp.exp(bl), dv) + _dot_tn(k_dec, v[rows])
        o = jnp.concatenate(outs, axis=0) + o_intra
        o_ref[:, vl] = _rms_rows(o, og_ref[...]).astype(o_ref.dtype)


def _gla_sample(q, k, v, la, og, s0, *, seg, bounded):
    t, hk = q.shape
    wc = v.shape[1]
    dk, dv = hk // H_C, wc // H_C
    db = t // seg
    grp = min(GLA_GROUP_S, db)
    rows = grp * seg
    tok = lambda i: (i, 0)
    st = lambda i: (i, 0, 0, 0)
    return pl.pallas_call(
        functools.partial(_gla_sample_body, dk=dk, dv=dv, seg=seg, bounded=bounded),
        grid=(db // grp,),
        in_specs=[
            pl.BlockSpec((rows, hk), tok),
            pl.BlockSpec((rows, hk), tok),
            pl.BlockSpec((rows, wc), tok),
            pl.BlockSpec((rows, hk), tok),
            pl.BlockSpec((1, dv), lambda i: (0, 0)),
            pl.BlockSpec((grp, H_C, dk, dv), st),
        ],
        out_specs=[pl.BlockSpec((rows, wc), tok), pl.BlockSpec((grp, H_C, dk, dv), st)],
        out_shape=[jax.ShapeDtypeStruct((t, wc), BF16), jax.ShapeDtypeStruct((db, H_C, dk, dv), F32)],
        compiler_params=_cparams(1),
        name="gla_sample" if bounded else "gla_sample_any_decay",
    )(q, k, v, la, og, s0)


def _out1_body(x_ref, o_ref, gs_ref, w_ref, y_ref):
    y_ref[...] = x_ref[...] + _dot(o_ref[...] * gs_ref[...], w_ref[...])


def _out1(x2d, o, gs, w_bf):
    t, d = x2d.shape
    tm = _row_tile(t, STREAM_TILE)
    row = lambda i: (i, 0)
    return pl.pallas_call(
        _out1_body,
        grid=(t // tm,),
        in_specs=[pl.BlockSpec((tm, d), row), pl.BlockSpec((tm, o.shape[1]), row),
                  pl.BlockSpec((tm, gs.shape[1]), row), pl.BlockSpec(w_bf.shape, lambda i: (0, 0))],
        out_specs=pl.BlockSpec((tm, d), row),
        out_shape=jax.ShapeDtypeStruct((t, d), F32),
        compiler_params=_cparams(1),
        name="out1",
    )(x2d, o, gs, w_bf)


def _seq_minor_body(x_hbm, o_ref, buf, sem):
    s_len = buf.shape[0]
    copies = [pltpu.make_async_copy(x_hbm.at[:, t, :], buf.at[t], sem.at[t]) for t in range(s_len)]
    for c in copies:
        c.start()
    for t, c in enumerate(copies):
        c.wait()
        o_ref[t] = buf[t].T


def _seq_minor(x2d, db, s_len):
    w = x2d.shape[1]
    return pl.pallas_call(
        _seq_minor_body,
        grid=(1,),
        in_specs=[pl.BlockSpec(memory_space=pl.ANY)],
        out_specs=pl.BlockSpec((s_len, w, db), lambda i: (0, 0, 0)),
        out_shape=jax.ShapeDtypeStruct((s_len, w, db), x2d.dtype),
        scratch_shapes=[pltpu.VMEM((s_len, db, w), x2d.dtype), pltpu.SemaphoreType.DMA((s_len,))],
        compiler_params=_cparams(1),
        name="seq_minor",
    )(x2d.reshape(db, s_len, w))


def _t5_bucket_np(dist):
    n = np.maximum(dist, 0)
    max_exact = N_BUCKETS // 2
    nf = np.maximum(n, 1).astype(np.float32)
    ratio = np.log(nf / np.float32(max_exact)) / np.float32(math.log(MAX_DISTANCE / max_exact))
    large = max_exact + (ratio * np.float32(N_BUCKETS - max_exact)).astype(np.int32)
    large = np.minimum(large, N_BUCKETS - 1)
    return np.where(n < max_exact, n, large)


def _bias_by_distance(rel_bias, n):
    buckets = _t5_bucket_np(np.arange(n))
    assert np.all(np.diff(buckets) >= 0)
    cuts = [0] + [int(i) + 1 for i in np.nonzero(np.diff(buckets))[0]] + [n]
    pieces = [jnp.broadcast_to(rel_bias[int(buckets[a])][:, None], (rel_bias.shape[1], e - a))
              for a, e in zip(cuts[:-1], cuts[1:])]
    return jnp.concatenate(pieces, axis=1).astype(F32) * np.float32(LOG2E)


def _toeplitz(w, n):
    h = w.shape[0]
    flat = jnp.tile(w, (1, n))[:, :n * (2 * n - 1)]
    return flat.reshape(h, n, 2 * n - 1)[:, :, n - 1:]


def _prompt_bias_tiles(rel_bias, tile):
    assert int(_t5_bucket_np(np.array([tile + 1]))[0]) == N_BUCKETS - 1
    tab = _bias_by_distance(rel_bias, 2 * tile)
    tab = tab - tab[:, 2 * tile - 1:]
    h = tab.shape[0]
    pad = jnp.zeros((h, 1), F32)
    diag = _toeplitz(jnp.concatenate([jnp.full((h, tile - 1), MASK_VALUE, F32), tab[:, :tile], pad], axis=1), tile)
    near = _toeplitz(jnp.concatenate([tab[:, 1:], pad], axis=1), tile)
    masked = jnp.full((h, tile, tile), MASK_VALUE, F32)
    zero = jnp.zeros((h, tile, tile), F32)
    tiles = [jnp.concatenate(p, axis=-1) for p in ((masked, diag), (diag, near), (near, zero), (zero, zero))]
    return jnp.stack(tiles, axis=1)


def _sample_bias(rel_bias, past, s_len):
    n = past + s_len
    rev = _bias_by_distance(rel_bias, n)[:, ::-1]
    h = rev.shape[0]
    bp = jnp.stack([rev[:, s_len - 1 - q:s_len - 1 - q + past] for q in range(s_len)], axis=1)
    small = jnp.concatenate([rev[:, past:], jnp.full((h, LANES), MASK_VALUE, F32)], axis=1)
    bn = jnp.stack([small[:, s_len - 1 - q:s_len - 1 - q + LANES] for q in range(s_len)], axis=1)
    bp = bp.reshape(h * s_len, past)
    bn = bn.reshape(h * s_len, LANES)
    return jnp.concatenate([bp, bp], axis=0), jnp.concatenate([bn, bn], axis=0)


def _sample_query_mask(s_len):
    rows = np.arange(2 * H_A * s_len)
    c = rows // (H_A * s_len)
    h = (rows // s_len) % H_A
    lane_grp = np.arange(H_A * 2 * DH_A) // DH_A
    return jnp.asarray(lane_grp[None, :] == (h * 2 + c)[:, None], dtype=BF16)


def _spatial_weights(sp_w, sp_b, seg, cg):
    g, n, _ = sp_w.shape
    w = sp_w * jnp.tril(jnp.ones((n, n), sp_w.dtype))
    reps = n // seg
    if reps > 1:
        eye = jnp.eye(reps, dtype=sp_w.dtype)
        w = jnp.einsum("ab,gts->gatbs", eye, w[:, :seg, :seg]).reshape(g, n, n)
    bias = jnp.tile(sp_b[:, :seg], (1, reps))
    bias = jnp.repeat(jnp.transpose(bias), cg, axis=1)
    return w.astype(BF16), bias.astype(F32)


def kernel(x_prompt, x_sample, cache_k, cache_v, state_gla, page_table, rel_bias, norm0_g, w_in0, q_norm_g,
           k_norm_g, lam, subln_g, ln_v_g, ln_v_b, spatial_w, spatial_b, w_out0, norm1_g, w_in1, w_gate,
           b_gate, gla_norm_g, w_out1):
    b, l, d = x_prompt.shape
    db, s_len, _ = x_sample.shape
    n_pool, page = cache_k.shape[1], cache_k.shape[2]
    past = page_table.shape[1] * page
    wa = H_A * DV_A
    lam_init = 0.8 - 0.6 * math.exp(-0.3 * 0)
    rank = w_gate.shape[1]
    hk = w_gate.shape[2]
    wc = w_out1.shape[1]
    dk, dv = hk // H_C, wc // H_C

    w0 = w_in0[0].astype(BF16)
    wo0 = w_out0[0].astype(BF16)
    grp = np.arange(wa) // DH_A
    red = jnp.asarray((grp[:, None] == np.arange(LANES)[None, :]) / DH_A, dtype=BF16)
    expd = jnp.asarray(np.arange(LANES)[:, None] == grp[None, :], dtype=BF16)
    qg = (jnp.tile(q_norm_g[0].reshape(-1), H_A) * np.float32(DH_A ** -0.5 * LOG2E)).reshape(1, wa).astype(F32)
    kg = jnp.tile(k_norm_g[0].reshape(-1), H_A).reshape(1, wa).astype(F32)
    ng0 = norm0_g[0].reshape(1, d)
    lng = ln_v_g[0].reshape(1, -1)
    lnb = ln_v_b[0].reshape(1, -1)
    sub_scaled = subln_g[0] * np.float32(1.0 - lam_init)
    subg_cols = jnp.broadcast_to(sub_scaled[:, None], (DV_A, 2 * ATTN_TILE)).astype(F32)
    subg_rows = jnp.tile(sub_scaled, H_A).reshape(1, wa).astype(F32)
    bias_tiles = _prompt_bias_tiles(rel_bias, ATTN_TILE)
    bias_past, bias_new = _sample_bias(rel_bias, past, s_len)
    wq_mask = _sample_query_mask(s_len)
    cg = ln_v_g.shape[1] // G_B
    wsp_p, bsp_p = _spatial_weights(spatial_w[0], spatial_b[0], CHUNK_B, cg)
    wsp_s, bsp_s = _spatial_weights(spatial_w[0], spatial_b[0], s_len, cg)

    w1 = w_in1[0]
    w1_bf = jnp.pad(w1, ((0, 0), (0, LANES - rank))).astype(BF16)
    wg = jnp.pad(w_gate[0], ((0, LANES - rank), (0, 0))).astype(BF16)
    bg = b_gate[0].reshape(1, hk)
    ng1 = norm1_g[0].reshape(1, d)
    og = gla_norm_g[0].reshape(1, dv)
    wo1 = w_out1[0].astype(BF16)

    xp = x_prompt.reshape(b * l, d)
    xs = x_sample.reshape(db * s_len, d)
    ck_t = jnp.transpose(cache_k[0], (0, 2, 3, 4, 1)).reshape(n_pool, wa, page)
    cv_rows = cache_v[0]

    qp, kbp, ktp, vp, vtp, gap, ubp, vbp = _proj0(xp, ng0, w0, red, expd, qg, kg, lng, lnb, vb_dtype=BF16,
                                                  prompt_batch=b)
    qs, ks, vs, gas, ubs, vbs = _proj0(xs, ng0, w0, red, expd, qg, kg, lng, lnb, vb_dtype=F32)

    score_bound = (np.float32(1.02 * math.sqrt(DH_A) * LOG2E)
                   * jnp.max(jnp.abs(q_norm_g[0])) * jnp.max(jnp.abs(k_norm_g[0]))
                   + np.float32(LOG2E) * jnp.max(jnp.abs(rel_bias - rel_bias[N_BUCKETS - 1])))
    n_rows = bias_past.shape[0]
    bias_past_halves = jnp.transpose(bias_past.reshape(n_rows, 2, past // 2), (1, 0, 2))
    attn_args = (page_table, qp.reshape(b, l, wa), kbp.reshape(b, l // ATTN_TILE, ATTN_TILE, wa), vtp, lam[0],
                 bias_tiles, subg_cols, qs.reshape(db, s_len, wa), ks.reshape(db, s_len, wa),
                 vs.reshape(db, s_len, wa), ck_t, cv_rows, wq_mask, bias_past_halves, bias_new, subg_rows)
    at_p, at_s = lax.cond(score_bound <= SAFE_SCORE_BOUND,
                          lambda args: _attention(*args, lam_init=lam_init, stable=False),
                          lambda args: _attention(*args, lam_init=lam_init, stable=True),
                          attn_args)

    yp0 = _mix0(xp, at_p.reshape(b * l, wa), gap, ubp, vbp, wsp_p, bsp_p, wo0)
    ys0 = _mix0(xs, at_s.reshape(db * s_len, wa), gas, ubs, vbs, wsp_s, bsp_s, wo0)

    q_scale = np.float32(dk ** -0.5)
    q1p, k1p, v1p, gsp, lap, lamin_p = _proj1(yp0, ng1, w1_bf, wg, bg, hk=hk, wc=wc, q_scale=q_scale)
    q1s, k1s, v1s, gss, las, lamin_s = _proj1(ys0, ng1, w1_bf, wg, bg, hk=hk, wc=wc, q_scale=q_scale)

    gla_p_args = (q1p.reshape(b, l, hk), k1p.reshape(b, l, hk), v1p.reshape(b, l, wc),
                  lap.reshape(b, l, hk), og, jnp.zeros((b, H_C, dk, dv), F32))
    o_p, st_p = lax.cond(-jnp.min(lamin_p) * min(GLA_CHUNK_P, l) <= SAFE_DECAY_BOUND,
                         lambda a: _gla_prompt(*a, bounded=True),
                         lambda a: _gla_prompt(*a, bounded=False), gla_p_args)
    gla_s_args = (q1s, k1s, v1s, las, og, state_gla[0])
    o_s, st_s = lax.cond(-jnp.min(lamin_s) * s_len <= SAFE_DECAY_BOUND,
                         lambda a: _gla_sample(*a, seg=s_len, bounded=True),
                         lambda a: _gla_sample(*a, seg=s_len, bounded=False), gla_s_args)

    yp1 = _out1(yp0, o_p.reshape(b * l, wc), gsp, wo1)
    ys1 = _out1(ys0, o_s, gss, wo1)

    return (
        yp1.reshape(b, l, d),
        ys1.reshape(db, s_len, d),
        jnp.transpose(ktp.reshape(b, H_A, 2, DH_A, l), (0, 4, 1, 2, 3))[None],
        vp.reshape(1, b, l, H_A, DV_A),
        jnp.transpose(_seq_minor(ks, db, s_len).reshape(s_len, H_A, 2, DH_A, db), (4, 0, 1, 2, 3))[None],
        vs.reshape(1, db, s_len, H_A, DV_A),
        vbs.reshape(1, db, s_len, -1),
        st_p.reshape(1, b, H_C, dk, dv),
        st_s.reshape(1, db, H_C, dk, dv),
    )
```

```python
import functools
import math

import numpy as np
import jax
import jax.numpy as jnp
from jax import lax
from jax.experimental import pallas as pl
from jax.experimental.pallas import tpu as pltpu

F32 = jnp.float32
BF16 = jnp.bfloat16

H_A = 8
DH_A = 64
DV_A = 2 * DH_A
G_B = 8
CHUNK_B = 128
H_C = 4
GLA_TAU = 16.0
N_BUCKETS = 32
MAX_DISTANCE = 128
EPS = 1e-6
MASK_VALUE = -1e30
LOG2E = 1.4426950408889634
SAFE_SCORE_BOUND = 60.0
SAFE_DECAY_BOUND = 60.0

LANES = 128
SUBLANES = 8
VMEM_LIMIT_BYTES = 56 * 1024 * 1024

TOKEN_TILE = 256
STREAM_TILE = 512
ATTN_TILE = TOKEN_TILE
GLA_CHUNK_P = 128
GLA_GROUP_S = 8


def _cparams(n_axes):
    return pltpu.CompilerParams(
        dimension_semantics=("arbitrary",) * n_axes,
        vmem_limit_bytes=VMEM_LIMIT_BYTES,
    )


def _row_tile(t, preferred):
    tm = min(preferred, t)
    while t % tm:
        tm -= CHUNK_B
    assert tm > 0 and t % tm == 0
    return tm


def _dot(a, b):
    return jnp.dot(a, b, preferred_element_type=F32)


def _dot_nt(a, b):
    return lax.dot_general(a, b, (((1,), (1,)), ((), ())), preferred_element_type=F32)


def _dot_tn(a, b):
    return lax.dot_general(a, b, (((0,), (0,)), ((), ())), preferred_element_type=F32)


def _silu(x):
    return x * jax.nn.sigmoid(x)


def _gelu(x):
    return 0.5 * x * (1.0 + lax.erf(x * np.float32(math.sqrt(0.5))))


def _rms_rows(x, g):
    return x * lax.rsqrt(jnp.mean(x * x, axis=-1, keepdims=True) + EPS) * g


def _split_bf16(x):
    hi = x.astype(BF16)
    lo = (x - hi.astype(F32)).astype(BF16)
    return hi, lo


def _proj0_body(x_ref, ng_ref, w_ref, red_ref, exp_ref, qg_ref, kg_ref, lng_ref, lnb_ref, *out_refs,
                width, prompt):
    if prompt:
        q_ref, kb_ref, kt_ref, v_ref, vt_ref, ga_ref, ub_ref, vb_ref = out_refs
    else:
        q_ref, k_ref, v_ref, ga_ref, ub_ref, vb_ref = out_refs
    xb = _rms_rows(x_ref[...], ng_ref[...]).astype(BF16)

    def seg(i):
        return _dot(xb, w_ref[:, i * width:(i + 1) * width])

    def group_rms(z, g):
        ms = _dot((z * z).astype(BF16), red_ref[...])
        hi, lo = _split_bf16(lax.rsqrt(ms + EPS))
        scale = _dot(hi, exp_ref[...]) + _dot(lo, exp_ref[...])
        return z * scale * g

    q_ref[...] = group_rms(seg(0), qg_ref[...]).astype(q_ref.dtype)
    k = group_rms(seg(1), kg_ref[...])
    v = seg(2)
    v_ref[...] = v
    if prompt:
        kb_ref[...] = k.astype(kb_ref.dtype)
        kt_ref[...] = k.T
        vt_ref[...] = v.T.astype(vt_ref.dtype)
    else:
        k_ref[...] = k
    ga_ref[...] = _silu(seg(3)).astype(ga_ref.dtype)
    ub_ref[...] = (_gelu(seg(4)) * _silu(seg(6))).astype(ub_ref.dtype)
    hv = _gelu(seg(5))
    hc = hv - jnp.mean(hv, axis=-1, keepdims=True)
    vb = hc * lax.rsqrt(jnp.mean(hc * hc, axis=-1, keepdims=True) + EPS)
    vb_ref[...] = (vb * lng_ref[...] + lnb_ref[...]).astype(vb_ref.dtype)


def _proj0(x2d, ng, w_bf, red, expd, qg, kg, lng, lnb, *, vb_dtype, prompt_batch=None):
    t, d = x2d.shape
    n_in = w_bf.shape[1]
    width = n_in // 7
    tm = _row_tile(t, TOKEN_TILE)
    row = lambda i: (i, 0)
    full = lambda i: (0, 0)
    tok = lambda dt: jax.ShapeDtypeStruct((t, width), dt)
    row_spec = pl.BlockSpec((tm, width), row)
    prompt = prompt_batch is not None
    if prompt:
        per = t // prompt_batch // tm
        outs = [tok(BF16), tok(BF16), jax.ShapeDtypeStruct((prompt_batch, width, per * tm), F32), tok(F32),
                jax.ShapeDtypeStruct((prompt_batch, per, width, tm), BF16), tok(BF16), tok(BF16), tok(vb_dtype)]
        out_specs = [row_spec, row_spec,
                     pl.BlockSpec((None, width, tm), lambda i: (i // per, 0, i % per)), row_spec,
                     pl.BlockSpec((None, None, width, tm), lambda i: (i // per, i % per, 0, 0)),
                     row_spec, row_spec, row_spec]
    else:
        outs = [tok(BF16), tok(F32), tok(F32), tok(BF16), tok(BF16), tok(vb_dtype)]
        out_specs = [row_spec] * 6
    return pl.pallas_call(
        functools.partial(_proj0_body, width=width, prompt=prompt),
        grid=(t // tm,),
        in_specs=[
            pl.BlockSpec((tm, d), row),
            pl.BlockSpec((1, d), full),
            pl.BlockSpec((d, n_in), full),
            pl.BlockSpec(red.shape, full),
            pl.BlockSpec(expd.shape, full),
            pl.BlockSpec((1, width), full),
            pl.BlockSpec((1, width), full),
            pl.BlockSpec((1, width), full),
            pl.BlockSpec((1, width), full),
        ],
        out_specs=out_specs,
        out_shape=outs,
        compiler_params=_cparams(1),
        name="proj0",
    )(x2d, ng, w_bf, red, expd, qg, kg, lng, lnb)


def _lambda_value(lam_ref, lam_init):
    lf = lam_ref[...]
    s1 = jnp.sum(lf[0:1] * lf[1:2], axis=-1, keepdims=True)
    s2 = jnp.sum(lf[2:3] * lf[3:4], axis=-1, keepdims=True)
    return jnp.exp(s1) - jnp.exp(s2) + lam_init


def _prompt_attention_step(qi, lam, q_ref, k_ref, vt_ref, bias_ref, subg_ref, o_ref, sa_sc, sb_sc, m_sc, l_sc,
                           acc_sc, *, stable):
    hd = 2 * DH_A
    n_heads = q_ref.shape[1] // hd
    tq = q_ref.shape[0]
    q = q_ref[...]
    lane = lax.broadcasted_iota(jnp.int32, (tq, hd), 1)
    q2t = []
    for hh in range(n_heads):
        qh = q[:, hh * hd:(hh + 1) * hd].astype(F32)
        q2 = jnp.concatenate([jnp.where(lane < DH_A, qh, 0.0), jnp.where(lane >= DH_A, qh, 0.0)], axis=0)
        q2t.append(q2.T.astype(BF16))

    def sublane_partial(x):
        return x.reshape(x.shape[0] // SUBLANES, SUBLANES, x.shape[1]).sum(axis=0)

    last_tile = 2 * qi + 1

    def scores(t, sbuf):
        j = last_tile - t
        for hh in range(n_heads):
            sbuf[hh] = _dot(k_ref[j, :, hh * hd:(hh + 1) * hd], q2t[hh])

    def consume(t, sbuf):
        j = last_tile - t
        kind = jnp.minimum(t, 3)
        ps = []
        for hh in range(n_heads):
            bt = bias_ref[hh, kind]
            s = sbuf[hh] + jnp.concatenate([bt, bt], axis=1)
            if stable:
                m = m_sc[hh]
                m_new = jnp.maximum(m, jnp.max(s, axis=0, keepdims=True))
                alpha = jnp.exp2(m - m_new)
                m_sc[hh] = m_new
                p = jnp.exp2(s - m_new)
                l_sc[hh] = alpha * l_sc[hh] + sublane_partial(p)
                acc_sc[hh] = alpha * acc_sc[hh]
            else:
                p = jnp.exp2(s)
                l_sc[hh] = l_sc[hh] + sublane_partial(p)
            ps.append(p.astype(BF16))
        for hh in range(n_heads):
            acc_sc[hh] = acc_sc[hh] + _dot(vt_ref[j, hh * DV_A:(hh + 1) * DV_A, :], ps[hh])

    m_sc[...] = jnp.full(m_sc.shape, MASK_VALUE, F32)
    l_sc[...] = jnp.zeros(l_sc.shape, F32)
    acc_sc[...] = jnp.zeros(acc_sc.shape, F32)
    n_keys = 2 * qi + 2
    scores(0, sa_sc)

    def body(pair, c):
        t = 2 * pair
        scores(t + 1, sb_sc)
        consume(t, sa_sc)
        scores(t + 2, sa_sc)
        consume(t + 1, sb_sc)
        return c

    lax.fori_loop(0, n_keys // 2 - 1, body, 0)
    scores(n_keys - 1, sb_sc)
    consume(n_keys - 2, sa_sc)
    consume(n_keys - 1, sb_sc)

    outs = []
    for hh in range(n_heads):
        l8, acc = l_sc[hh], acc_sc[hh]
        o_all = acc * (1.0 / jnp.sum(l8, axis=0, keepdims=True))
        o = o_all[:, :tq] - lam * o_all[:, tq:]
        on = o * lax.rsqrt(jnp.mean(o * o, axis=0, keepdims=True) + EPS) * subg_ref[...]
        outs.append(on.T)
    o_ref[...] = jnp.concatenate(outs, axis=1).astype(o_ref.dtype)


def _sample_attention_item(half, slot, sb, lam, qs_ref, kn_ref, vn_ref, mask_ref, bp_ref, bn_ref, subg_ref, os_ref,
                           kbuf, vbuf, sm_sc, sl_sc, sacc_sc, *, key_chunk):
    q = qs_ref[sb]
    s_len = q.shape[0]
    n_rows = mask_ref.shape[0]
    half_rows = H_A * s_len
    wq = jnp.concatenate([q] * (n_rows // s_len), axis=0) * mask_ref[...]
    first = half == 0
    n_chunks = kbuf.shape[2] // key_chunk

    pad_rows = bn_ref.shape[1] - s_len
    kn = jnp.concatenate([kn_ref[sb], jnp.zeros((pad_rows, kn_ref.shape[2]), F32)], axis=0).astype(BF16)
    vn = jnp.concatenate([vn_ref[sb], jnp.zeros((pad_rows, vn_ref.shape[2]), F32)], axis=0).astype(BF16)
    s_parts = [_dot_nt(wq, kn) + bn_ref[...] + jnp.where(first, 0.0, MASK_VALUE).astype(F32)]
    for c in range(n_chunks):
        keys = pl.ds(c * key_chunk, key_chunk)
        s_parts.append(_dot(wq, kbuf[slot, :, keys].astype(BF16)) + bp_ref[half, :, keys])

    m_old = jnp.where(first, MASK_VALUE, sm_sc[sb])
    m = m_old
    for s in s_parts:
        m = jnp.maximum(m, jnp.max(s, axis=1, keepdims=True))
    alpha = jnp.exp2(m_old - m)
    p_parts = [jnp.exp2(s - m) for s in s_parts]
    l = alpha * jnp.where(first, 0.0, sl_sc[sb])
    for p in p_parts:
        l = l + jnp.sum(p, axis=1, keepdims=True)
    sm_sc[sb] = m
    sl_sc[sb] = l
    row = lax.broadcasted_iota(jnp.int32, l.shape, 0)
    scale = jnp.where(row < half_rows, 1.0, -lam) / l

    def head_rows(x, h):
        r0 = h * s_len
        return jnp.concatenate([x[r0:r0 + s_len], x[half_rows + r0:half_rows + r0 + s_len]], axis=0)

    vhs = [vbuf[slot, h].astype(BF16) for h in range(H_A)]
    p_past = jnp.concatenate(p_parts[1:], axis=1) if n_chunks > 1 else p_parts[1]
    pvs = []
    for h in range(H_A):
        pv = _dot(head_rows(p_past, h).astype(BF16), vhs[h])
        pvs.append(pv + _dot(head_rows(p_parts[0], h).astype(BF16), vn[:, h * DV_A:(h + 1) * DV_A]))
    heads = []
    for h in range(H_A):
        acc = head_rows(alpha, h) * jnp.where(first, 0.0, sacc_sc[sb, h]) + pvs[h]
        sacc_sc[sb, h] = acc
        w = acc * head_rows(scale, h)
        o_h = w[:s_len] + w[s_len:]
        heads.append(o_h * lax.rsqrt(jnp.mean(o_h * o_h, axis=-1, keepdims=True) + EPS))
    os_ref[sb] = (jnp.concatenate(heads, axis=1) * subg_ref[...]).astype(os_ref.dtype)


def _attention_body(pt_ref, lam_ref, q_ref, k_ref, vt_ref, bias_ref, subg_ref,
                    qs_ref, kn_ref, vn_ref, mask_ref, bp_ref, bn_ref, subgr_ref, ck_hbm, cv_hbm,
                    o_ref, os_ref,
                    sa_sc, sb_sc, m_sc, l_sc, acc_sc, kbuf, vbuf, sem, sm_sc, sl_sc, sacc_sc,
                    *, lam_init, stable, items_per_step, pages_per_item, page, key_chunk):
    bi, hp, qi = pl.program_id(0), pl.program_id(1), pl.program_id(2)
    step = (bi * pl.num_programs(1) + hp) * pl.num_programs(2) + qi
    n_steps = pl.num_programs(0) * pl.num_programs(1) * pl.num_programs(2)
    ipg = items_per_step

    def item_copies(item, slot):
        seq, half = item // 2, item % 2
        cps = []
        for p in range(pages_per_item):
            pg = pt_ref[seq, half * pages_per_item + p]
            cps.append(pltpu.make_async_copy(ck_hbm.at[pg], kbuf.at[slot, :, pl.ds(p * page, page)],
                                             sem.at[slot, 0]))
            for h in range(H_A):
                cps.append(pltpu.make_async_copy(cv_hbm.at[pg, :, h, :],
                                                 vbuf.at[slot, h, pl.ds(p * page, page), :], sem.at[slot, 1]))
        return cps

    @pl.when(step == 0)
    def _():
        for slot in range(2):
            for c in item_copies(slot, slot):
                c.start()
        sm_sc[...] = jnp.zeros(sm_sc.shape, F32)
        sl_sc[...] = jnp.zeros(sl_sc.shape, F32)
        sacc_sc[...] = jnp.zeros(sacc_sc.shape, F32)

    lam = _lambda_value(lam_ref, lam_init)
    for r in range(ipg):
        item = step * ipg + r
        slot = r % 2
        for c in item_copies(item, slot):
            c.wait()
        _sample_attention_item(jnp.int32(r % 2), slot, r // 2, lam, qs_ref, kn_ref, vn_ref, mask_ref, bp_ref, bn_ref,
                               subgr_ref, os_ref, kbuf, vbuf, sm_sc, sl_sc, sacc_sc, key_chunk=key_chunk)

        @pl.when(item + 2 < n_steps * ipg)
        def _():
            for c in item_copies(item + 2, slot):
                c.start()

    _prompt_attention_step(qi, lam, q_ref, k_ref, vt_ref, bias_ref, subg_ref, o_ref, sa_sc, sb_sc, m_sc, l_sc,
                           acc_sc, stable=stable)


def _attention(page_table, q, k, vt, lam_p, bias_tiles, subg_cols, qs, k_new, v_new, cache_k, cache_v, wq_mask,
               bias_past, bias_new, subg_rows, *, lam_init, stable):
    b, l, w = q.shape
    db, s_len, _ = qs.shape
    tile = ATTN_TILE
    tq = 2 * tile
    n_tiles, n_q = l // tile, l // tq
    hp = 2
    wb = hp * 2 * DH_A
    nhp = H_A // hp
    n_steps = b * nhp * n_q
    n_pages, page = page_table.shape[1], cache_k.shape[2]
    pages_per_item = n_pages // 2
    ipg = (2 * db) // n_steps
    assert l % tq == 0 and n_pages % 2 == 0 and ipg * n_steps == 2 * db and ipg % 2 == 0
    spb = ipg // 2
    hk = pages_per_item * page
    key_chunk = min(512, hk)
    n_rows = wq_mask.shape[0]

    def sample_blk(bi, h, i, pt):
        return ((bi * nhp + h) * n_q + i, 0, 0)

    full2 = lambda bi, h, i, pt: (0, 0)
    grid_spec = pltpu.PrefetchScalarGridSpec(
        num_scalar_prefetch=1,
        grid=(b, nhp, n_q),
        in_specs=[
            pl.BlockSpec(lam_p.shape, full2),
            pl.BlockSpec((None, tq, wb), lambda bi, h, i, pt: (bi, i, h)),
            pl.BlockSpec((None, n_tiles, tile, wb), lambda bi, h, i, pt: (bi, 0, 0, h)),
            pl.BlockSpec((None, n_tiles, wb, tile), lambda bi, h, i, pt: (bi, 0, h, 0)),
            pl.BlockSpec((hp, 4, tile, tq), lambda bi, h, i, pt: (h, 0, 0, 0)),
            pl.BlockSpec((DV_A, tq), full2),
            pl.BlockSpec((spb, s_len, w), sample_blk),
            pl.BlockSpec((spb, s_len, w), sample_blk),
            pl.BlockSpec((spb, s_len, w), sample_blk),
            pl.BlockSpec(wq_mask.shape, full2),
            pl.BlockSpec(bias_past.shape, lambda bi, h, i, pt: (0, 0, 0)),
            pl.BlockSpec(bias_new.shape, full2),
            pl.BlockSpec(subg_rows.shape, full2),
            pl.BlockSpec(memory_space=pl.ANY),
            pl.BlockSpec(memory_space=pl.ANY),
        ],
        out_specs=[pl.BlockSpec((None, tq, wb), lambda bi, h, i, pt: (bi, i, h)),
                   pl.BlockSpec((spb, s_len, w), sample_blk)],
        scratch_shapes=[
            pltpu.VMEM((hp, tile, 2 * tq), F32),
            pltpu.VMEM((hp, tile, 2 * tq), F32),
            pltpu.VMEM((hp, 1, 2 * tq), F32),
            pltpu.VMEM((hp, SUBLANES, 2 * tq), F32),
            pltpu.VMEM((hp, DV_A, 2 * tq), F32),
            pltpu.VMEM((2, w, hk), F32),
            pltpu.VMEM((2, H_A, hk, DV_A), F32),
            pltpu.SemaphoreType.DMA((2, 2)),
            pltpu.VMEM((spb, n_rows, 1), F32),
            pltpu.VMEM((spb, n_rows, 1), F32),
            pltpu.VMEM((spb, H_A, 2 * s_len, DV_A), F32),
        ],
    )
    return pl.pallas_call(
        functools.partial(_attention_body, lam_init=lam_init, stable=stable, items_per_step=ipg,
                          pages_per_item=pages_per_item, page=page, key_chunk=key_chunk),
        grid_spec=grid_spec,
        out_shape=[jax.ShapeDtypeStruct((b, l, w), BF16), jax.ShapeDtypeStruct((db, s_len, w), BF16)],
        compiler_params=_cparams(3),
        name="attention_stable" if stable else "attention",
    )(page_table, lam_p, q, k, vt, bias_tiles, subg_cols, qs, k_new, v_new, wq_mask, bias_past, bias_new,
      subg_rows, cache_k, cache_v)


def _mix0_body(x_ref, at_ref, ga_ref, ub_ref, vb_ref, wsp_ref, bsp_ref, wo_ref, y_ref):
    tm = x_ref.shape[0]
    wa = at_ref.shape[1]
    vb = vb_ref[...].astype(BF16)
    n_sub = tm // CHUNK_B
    cg = vb.shape[1] // G_B
    cols = []
    for g in range(G_B):
        lanes = slice(g * cg, (g + 1) * cg)
        rows = [_dot(wsp_ref[g], vb[sb * CHUNK_B:(sb + 1) * CHUNK_B, lanes]) for sb in range(n_sub)]
        cols.append(jnp.concatenate(rows, axis=0) if n_sub > 1 else rows[0])
    bsp = bsp_ref[...]
    bias = jnp.concatenate([bsp] * n_sub, axis=0) if n_sub > 1 else bsp
    mixed = jnp.concatenate(cols, axis=1) + bias
    ob = (ub_ref[...].astype(F32) * mixed).astype(BF16)
    oa = at_ref[...] * ga_ref[...]
    y_ref[...] = x_ref[...] + _dot(oa, wo_ref[:wa, :]) + _dot(ob, wo_ref[wa:, :])


def _mix0(x2d, attn, gas, ub, vb, wsp, bsp, wo_bf):
    t, d = x2d.shape
    tm = _row_tile(t, STREAM_TILE)
    row = lambda i: (i, 0)
    full = lambda i: (0, 0)
    return pl.pallas_call(
        _mix0_body,
        grid=(t // tm,),
        in_specs=[
            pl.BlockSpec((tm, d), row),
            pl.BlockSpec((tm, attn.shape[1]), row),
            pl.BlockSpec((tm, gas.shape[1]), row),
            pl.BlockSpec((tm, ub.shape[1]), row),
            pl.BlockSpec((tm, vb.shape[1]), row),
            pl.BlockSpec(wsp.shape, lambda i: (0, 0, 0)),
            pl.BlockSpec(bsp.shape, full),
            pl.BlockSpec(wo_bf.shape, full),
        ],
        out_specs=pl.BlockSpec((tm, d), row),
        out_shape=jax.ShapeDtypeStruct((t, d), F32),
        compiler_params=_cparams(1),
        name="mix0",
    )(x2d, attn, gas, ub, vb, wsp, bsp, wo_bf)


def _proj1_body(x_ref, ng_ref, w_ref, wg_ref, bg_ref, q_ref, k_ref, v_ref, gs_ref, la_ref, lamin_ref,
                *, hk, wc, q_scale):
    xb = _rms_rows(x_ref[...], ng_ref[...]).astype(BF16)
    q_ref[...] = _dot(xb, w_ref[:, 0:hk]) * q_scale
    k_ref[...] = _dot(xb, w_ref[:, hk:2 * hk])
    v_ref[...] = _dot(xb, w_ref[:, 2 * hk:2 * hk + wc]).astype(v_ref.dtype)
    gs_ref[...] = _silu(_dot(xb, w_ref[:, 2 * hk + wc:2 * hk + 2 * wc])).astype(gs_ref.dtype)
    za = _dot(xb, w_ref[:, 2 * hk + 2 * wc:]).astype(BF16)
    xg = _dot(za, wg_ref[...]) + bg_ref[...]
    la = (jnp.minimum(xg, 0.0) - jnp.log1p(jnp.exp(-jnp.abs(xg)))) * np.float32(1.0 / GLA_TAU)
    la_ref[...] = la
    lamin_ref[...] = jnp.min(la, axis=0, keepdims=True)


def _proj1(x2d, ng, w_bf, wg_bf, bg, *, hk, wc, q_scale):
    t, d = x2d.shape
    tm = _row_tile(t, STREAM_TILE)
    row = lambda i: (i, 0)
    full = lambda i: (0, 0)
    outs = [
        jax.ShapeDtypeStruct((t, hk), F32),
        jax.ShapeDtypeStruct((t, hk), F32),
        jax.ShapeDtypeStruct((t, wc), BF16),
        jax.ShapeDtypeStruct((t, wc), BF16),
        jax.ShapeDtypeStruct((t, hk), F32),
        jax.ShapeDtypeStruct((t // tm, 1, hk), F32),
    ]
    return pl.pallas_call(
        functools.partial(_proj1_body, hk=hk, wc=wc, q_scale=q_scale),
        grid=(t // tm,),
        in_specs=[
            pl.BlockSpec((tm, d), row),
            pl.BlockSpec((1, d), full),
            pl.BlockSpec(w_bf.shape, full),
            pl.BlockSpec(wg_bf.shape, full),
            pl.BlockSpec((1, hk), full),
        ],
        out_specs=[pl.BlockSpec((tm, hk), row), pl.BlockSpec((tm, hk), row), pl.BlockSpec((tm, wc), row),
                   pl.BlockSpec((tm, wc), row), pl.BlockSpec((tm, hk), row),
                   pl.BlockSpec((None, 1, hk), lambda i: (i, 0, 0))],
        out_shape=outs,
        compiler_params=_cparams(1),
        name="proj1",
    )(x2d, ng, w_bf, wg_bf, bg)


def _segment_cumsum(x, seg):
    n = x.shape[0]
    row = lax.broadcasted_iota(jnp.int32, x.shape, 0) % seg
    d = 1
    while d < seg:
        x = x + jnp.where(row >= d, pltpu.roll(x, d, axis=0), 0.0)
        d *= 2
    return x


def _block_reference(bc, hb):
    n, w = bc.shape
    if hb >= SUBLANES:
        pieces = []
        for blk in range(n // (2 * hb)):
            r = blk * 2 * hb + hb - 1
            pieces.append(jnp.broadcast_to(bc[r:r + 1, :], (2 * hb, w)))
        return jnp.concatenate(pieces, axis=0) if len(pieces) > 1 else pieces[0]
    pos = lax.broadcasted_iota(jnp.int32, bc.shape, 0) % (2 * hb)
    out = bc
    for delta in range(-hb, hb):
        if delta == 0:
            continue
        shifted = pltpu.roll(bc, (-delta) % n, axis=0)
        out = jnp.where(pos == hb - 1 - delta, shifted, out)
    return out


def _gla_intra_scores(q, k, bc, seg, bounded):
    n = q.shape[0]
    ri = lax.broadcasted_iota(jnp.int32, (n, n), 0)
    ci = lax.broadcasted_iota(jnp.int32, (n, n), 1)
    if bounded:
        att = _dot_nt((q * jnp.exp(bc)).astype(BF16), (k * jnp.exp(-bc)).astype(BF16))
        att = jnp.where(ri >= ci, att, 0.0)
        if seg < n:
            att = jnp.where((ri // seg) == (ci // seg), att, 0.0)
        return att
    rowpos = lax.broadcasted_iota(jnp.int32, q.shape, 0)
    att = jnp.where(ri == ci, _dot_nt(q.astype(BF16), k.astype(BF16)), 0.0)
    for hb in _levels(seg):
        ref = _block_reference(bc, hb)
        is_q = (rowpos % (2 * hb)) >= hb
        f = jnp.exp(jnp.where(is_q, bc - ref, ref - bc))
        qt = jnp.where(is_q, q * f, 0.0).astype(BF16)
        kt = jnp.where(is_q, 0.0, k * f).astype(BF16)
        same = (ri // (2 * hb)) == (ci // (2 * hb))
        att = att + jnp.where(same, _dot_nt(qt, kt), 0.0)
    return att


def _levels(seg):
    out, hb = [], seg // 2
    while hb >= 1:
        out.append(hb)
        hb //= 2
    return out


def _rows_scale_matrix(row, n_cols):
    hi = row.astype(BF16).astype(F32)
    r1 = row - hi
    mid = r1.astype(BF16).astype(F32)
    lo = (r1 - mid).astype(BF16).astype(F32)
    pad = jnp.zeros((2 * SUBLANES - 3, row.shape[1]), F32)
    parts = jnp.concatenate([hi, mid, lo, pad], axis=0).astype(BF16)
    return _dot_tn(parts, jnp.ones((2 * SUBLANES, n_cols), BF16))


def _gla_prompt_body(q_ref, k_ref, v_ref, la_ref, og_ref, s0_ref, o_ref, sout_ref, st_sc, *, dk, dv,
                     bounded):
    ci = pl.program_id(1)
    n_seq, chunk = q_ref.shape[0], q_ref.shape[1]

    @pl.when(ci == 0)
    def _():
        st_sc[...] = s0_ref[...]

    units = [(s, h, slice(h * dk, (h + 1) * dk), slice(h * dv, (h + 1) * dv))
             for s in range(n_seq) for h in range(H_C)]
    bcs = [_segment_cumsum(la_ref[s, :, kl], chunk) for s, _, kl, _ in units]
    atts = [_gla_intra_scores(q_ref[s, :, kl], k_ref[s, :, kl], bc, chunk, bounded)
            for (s, _, kl, _), bc in zip(units, bcs)]
    outs, upds, decays = [], [], []
    for (s, h, kl, vl), bc, att in zip(units, bcs, atts):
        q, k, v = q_ref[s, :, kl], k_ref[s, :, kl], v_ref[s, :, vl]
        b_last = bc[chunk - 1:chunk, :]
        k_dec = (k * jnp.exp(b_last - bc)).astype(BF16)
        outs.append(_dot((q * jnp.exp(bc)).astype(BF16), st_sc[s, h].astype(BF16)) + _dot(att.astype(BF16), v))
        upds.append(_dot_tn(k_dec, v))
        decays.append(_rows_scale_matrix(jnp.exp(b_last), dv))
    for (s, h, _, vl), out, upd, decay in zip(units, outs, upds, decays):
        st_sc[s, h] = st_sc[s, h] * decay + upd
        o_ref[s, :, vl] = _rms_rows(out, og_ref[...]).astype(o_ref.dtype)

    @pl.when(ci == pl.num_programs(1) - 1)
    def _():
        sout_ref[...] = st_sc[...]


def _gla_prompt(q, k, v, la, og, s0, *, bounded):
    b, l, hk = q.shape
    wc = v.shape[2]
    dk, dv = hk // H_C, wc // H_C
    chunk = min(GLA_CHUNK_P, l)
    nb = 2 if b % 2 == 0 else 1
    tok = lambda bi, ci: (bi, ci, 0)
    st = lambda bi, ci: (bi, 0, 0, 0)
    return pl.pallas_call(
        functools.partial(_gla_prompt_body, dk=dk, dv=dv, bounded=bounded),
        grid=(b // nb, l // chunk),
        in_specs=[
            pl.BlockSpec((nb, chunk, hk), tok),
            pl.BlockSpec((nb, chunk, hk), tok),
            pl.BlockSpec((nb, chunk, wc), tok),
            pl.BlockSpec((nb, chunk, hk), tok),
            pl.BlockSpec((1, dv), lambda bi, ci: (0, 0)),
            pl.BlockSpec((nb, H_C, dk, dv), st),
        ],
        out_specs=[pl.BlockSpec((nb, chunk, wc), tok), pl.BlockSpec((nb, H_C, dk, dv), st)],
        out_shape=[jax.ShapeDtypeStruct((b, l, wc), BF16), jax.ShapeDtypeStruct((b, H_C, dk, dv), F32)],
        scratch_shapes=[pltpu.VMEM((nb, H_C, dk, dv), F32)],
        compiler_params=_cparams(2),
        name="gla_prompt" if bounded else "gla_prompt_any_decay",
    )(q, k, v, la, og, s0)


def _gla_sample_body(q_ref, k_ref, v_ref, la_ref, og_ref, s0_ref, o_ref, sout_ref, *, dk, dv, seg, bounded):
    n_seq = s0_ref.shape[0]
    for h in range(H_C):
        kl = slice(h * dk, (h + 1) * dk)
        vl = slice(h * dv, (h + 1) * dv)
        q, k, la = q_ref[:, kl], k_ref[:, kl], la_ref[:, kl]
        v = v_ref[:, vl]
        bc = _segment_cumsum(la, seg)
        att = _gla_intra_scores(q, k, bc, seg, bounded)
        o_intra = _dot(att.astype(BF16), v)
        q_dec = (q * jnp.exp(bc)).astype(BF16)
        outs = []
        for s in range(n_seq):
            rows = slice(s * seg, (s + 1) * seg)
            st = s0_ref[s, h]
            bl = bc[s * seg + seg - 1:s * seg + seg, :]
            k_dec = (k[rows] * jnp.exp(bl - bc[rows])).astype(BF16)
            outs.append(_dot(q_dec[rows], st.astype(BF16)))
            sout_ref[s, h] = st * _rows_scale_matrix(jnp.exp(bl), dv) + _dot_tn(k_dec, v[rows])
        o = jnp.concatenate(outs, axis=0) + o_intra
        o_ref[:, vl] = _rms_rows(o, og_ref[...]).astype(o_ref.dtype)


def _gla_sample(q, k, v, la, og, s0, *, seg, bounded):
    t, hk = q.shape
    wc = v.shape[1]
    dk, dv = hk // H_C, wc // H_C
    db = t // seg
    grp = min(GLA_GROUP_S, db)
    rows = grp * seg
    tok = lambda i: (i, 0)
    st = lambda i: (i, 0, 0, 0)
    return pl.pallas_call(
        functools.partial(_gla_sample_body, dk=dk, dv=dv, seg=seg, bounded=bounded),
        grid=(db // grp,),
        in_specs=[
            pl.BlockSpec((rows, hk), tok),
            pl.BlockSpec((rows, hk), tok),
            pl.BlockSpec((rows, wc), tok),
            pl.BlockSpec((rows, hk), tok),
            pl.BlockSpec((1, dv), lambda i: (0, 0)),
            pl.BlockSpec((grp, H_C, dk, dv), st),
        ],
        out_specs=[pl.BlockSpec((rows, wc), tok), pl.BlockSpec((grp, H_C, dk, dv), st)],
        out_shape=[jax.ShapeDtypeStruct((t, wc), BF16), jax.ShapeDtypeStruct((db, H_C, dk, dv), F32)],
        compiler_params=_cparams(1),
        name="gla_sample" if bounded else "gla_sample_any_decay",
    )(q, k, v, la, og, s0)


def _out1_body(x_ref, o_ref, gs_ref, w_ref, y_ref):
    y_ref[...] = x_ref[...] + _dot(o_ref[...] * gs_ref[...], w_ref[...])


def _out1(x2d, o, gs, w_bf):
    t, d = x2d.shape
    tm = _row_tile(t, STREAM_TILE)
    row = lambda i: (i, 0)
    return pl.pallas_call(
        _out1_body,
        grid=(t // tm,),
        in_specs=[pl.BlockSpec((tm, d), row), pl.BlockSpec((tm, o.shape[1]), row),
                  pl.BlockSpec((tm, gs.shape[1]), row), pl.BlockSpec(w_bf.shape, lambda i: (0, 0))],
        out_specs=pl.BlockSpec((tm, d), row),
        out_shape=jax.ShapeDtypeStruct((t, d), F32),
        compiler_params=_cparams(1),
        name="out1",
    )(x2d, o, gs, w_bf)


def _seq_minor_body(x_hbm, o_ref, buf, sem):
    s_len = buf.shape[0]
    copies = [pltpu.make_async_copy(x_hbm.at[:, t, :], buf.at[t], sem.at[t]) for t in range(s_len)]
    for c in copies:
        c.start()
    for t, c in enumerate(copies):
        c.wait()
        o_ref[t] = buf[t].T


def _seq_minor(x2d, db, s_len):
    w = x2d.shape[1]
    return pl.pallas_call(
        _seq_minor_body,
        grid=(1,),
        in_specs=[pl.BlockSpec(memory_space=pl.ANY)],
        out_specs=pl.BlockSpec((s_len, w, db), lambda i: (0, 0, 0)),
        out_shape=jax.ShapeDtypeStruct((s_len, w, db), x2d.dtype),
        scratch_shapes=[pltpu.VMEM((s_len, db, w), x2d.dtype), pltpu.SemaphoreType.DMA((s_len,))],
        compiler_params=_cparams(1),
        name="seq_minor",
    )(x2d.reshape(db, s_len, w))


def _t5_bucket_np(dist):
    n = np.maximum(dist, 0)
    max_exact = N_BUCKETS // 2
    nf = np.maximum(n, 1).astype(np.float32)
    ratio = np.log(nf / np.float32(max_exact)) / np.float32(math.log(MAX_DISTANCE / max_exact))
    large = max_exact + (ratio * np.float32(N_BUCKETS - max_exact)).astype(np.int32)
    large = np.minimum(large, N_BUCKETS - 1)
    return np.where(n < max_exact, n, large)


def _bias_by_distance(rel_bias, n):
    buckets = _t5_bucket_np(np.arange(n))
    assert np.all(np.diff(buckets) >= 0)
    cuts = [0] + [int(i) + 1 for i in np.nonzero(np.diff(buckets))[0]] + [n]
    pieces = [jnp.broadcast_to(rel_bias[int(buckets[a])][:, None], (rel_bias.shape[1], e - a))
              for a, e in zip(cuts[:-1], cuts[1:])]
    return jnp.concatenate(pieces, axis=1).astype(F32) * np.float32(LOG2E)


def _toeplitz(w, n):
    h = w.shape[0]
    flat = jnp.tile(w, (1, n))[:, :n * (2 * n - 1)]
    return flat.reshape(h, n, 2 * n - 1)[:, :, n - 1:]


def _prompt_bias_tiles(rel_bias, tile):
    assert int(_t5_bucket_np(np.array([tile + 1]))[0]) == N_BUCKETS - 1
    tab = _bias_by_distance(rel_bias, 2 * tile)
    tab = tab - tab[:, 2 * tile - 1:]
    h = tab.shape[0]
    pad = jnp.zeros((h, 1), F32)
    diag = _toeplitz(jnp.concatenate([jnp.full((h, tile - 1), MASK_VALUE, F32), tab[:, :tile], pad], axis=1), tile)
    near = _toeplitz(jnp.concatenate([tab[:, 1:], pad], axis=1), tile)
    masked = jnp.full((h, tile, tile), MASK_VALUE, F32)
    zero = jnp.zeros((h, tile, tile), F32)
    tiles = [jnp.concatenate(p, axis=-1) for p in ((masked, diag), (diag, near), (near, zero), (zero, zero))]
    return jnp.stack(tiles, axis=1)


def _sample_bias(rel_bias, past, s_len):
    n = past + s_len
    rev = _bias_by_distance(rel_bias, n)[:, ::-1]
    h = rev.shape[0]
    bp = jnp.stack([rev[:, s_len - 1 - q:s_len - 1 - q + past] for q in range(s_len)], axis=1)
    small = jnp.concatenate([rev[:, past:], jnp.full((h, LANES), MASK_VALUE, F32)], axis=1)
    bn = jnp.stack([small[:, s_len - 1 - q:s_len - 1 - q + LANES] for q in range(s_len)], axis=1)
    bp = bp.reshape(h * s_len, past)
    bn = bn.reshape(h * s_len, LANES)
    return jnp.concatenate([bp, bp], axis=0), jnp.concatenate([bn, bn], axis=0)


def _sample_query_mask(s_len):
    rows = np.arange(2 * H_A * s_len)
    c = rows // (H_A * s_len)
    h = (rows // s_len) % H_A
    lane_grp = np.arange(H_A * 2 * DH_A) // DH_A
    return jnp.asarray(lane_grp[None, :] == (h * 2 + c)[:, None], dtype=BF16)


def _spatial_weights(sp_w, sp_b, seg, cg):
    g, n, _ = sp_w.shape
    w = sp_w * jnp.tril(jnp.ones((n, n), sp_w.dtype))
    reps = n // seg
    if reps > 1:
        eye = jnp.eye(reps, dtype=sp_w.dtype)
        w = jnp.einsum("ab,gts->gatbs", eye, w[:, :seg, :seg]).reshape(g, n, n)
    bias = jnp.tile(sp_b[:, :seg], (1, reps))
    bias = jnp.repeat(jnp.transpose(bias), cg, axis=1)
    return w.astype(BF16), bias.astype(F32)


def kernel(x_prompt, x_sample, cache_k, cache_v, state_gla, page_table, rel_bias, norm0_g, w_in0, q_norm_g,
           k_norm_g, lam, subln_g, ln_v_g, ln_v_b, spatial_w, spatial_b, w_out0, norm1_g, w_in1, w_gate,
           b_gate, gla_norm_g, w_out1):
    b, l, d = x_prompt.shape
    db, s_len, _ = x_sample.shape
    n_pool, page = cache_k.shape[1], cache_k.shape[2]
    past = page_table.shape[1] * page
    wa = H_A * DV_A
    lam_init = 0.8 - 0.6 * math.exp(-0.3 * 0)
    rank = w_gate.shape[1]
    hk = w_gate.shape[2]
    wc = w_out1.shape[1]
    dk, dv = hk // H_C, wc // H_C

    w0 = w_in0[0].astype(BF16)
    wo0 = w_out0[0].astype(BF16)
    grp = np.arange(wa) // DH_A
    red = jnp.asarray((grp[:, None] == np.arange(LANES)[None, :]) / DH_A, dtype=BF16)
    expd = jnp.asarray(np.arange(LANES)[:, None] == grp[None, :], dtype=BF16)
    qg = (jnp.tile(q_norm_g[0].reshape(-1), H_A) * np.float32(DH_A ** -0.5 * LOG2E)).reshape(1, wa).astype(F32)
    kg = jnp.tile(k_norm_g[0].reshape(-1), H_A).reshape(1, wa).astype(F32)
    ng0 = norm0_g[0].reshape(1, d)
    lng = ln_v_g[0].reshape(1, -1)
    lnb = ln_v_b[0].reshape(1, -1)
    sub_scaled = subln_g[0] * np.float32(1.0 - lam_init)
    subg_cols = jnp.broadcast_to(sub_scaled[:, None], (DV_A, 2 * ATTN_TILE)).astype(F32)
    subg_rows = jnp.tile(sub_scaled, H_A).reshape(1, wa).astype(F32)
    bias_tiles = _prompt_bias_tiles(rel_bias, ATTN_TILE)
    bias_past, bias_new = _sample_bias(rel_bias, past, s_len)
    wq_mask = _sample_query_mask(s_len)
    cg = ln_v_g.shape[1] // G_B
    wsp_p, bsp_p = _spatial_weights(spatial_w[0], spatial_b[0], CHUNK_B, cg)
    wsp_s, bsp_s = _spatial_weights(spatial_w[0], spatial_b[0], s_len, cg)

    w1 = w_in1[0]
    w1_bf = jnp.pad(w1, ((0, 0), (0, LANES - rank))).astype(BF16)
    wg = jnp.pad(w_gate[0], ((0, LANES - rank), (0, 0))).astype(BF16)
    bg = b_gate[0].reshape(1, hk)
    ng1 = norm1_g[0].reshape(1, d)
    og = gla_norm_g[0].reshape(1, dv)
    wo1 = w_out1[0].astype(BF16)

    xp = x_prompt.reshape(b * l, d)
    xs = x_sample.reshape(db * s_len, d)
    ck_t = jnp.transpose(cache_k[0], (0, 2, 3, 4, 1)).reshape(n_pool, wa, page)
    cv_rows = cache_v[0]

    qp, kbp, ktp, vp, vtp, gap, ubp, vbp = _proj0(xp, ng0, w0, red, expd, qg, kg, lng, lnb, vb_dtype=BF16,
                                                  prompt_batch=b)
    qs, ks, vs, gas, ubs, vbs = _proj0(xs, ng0, w0, red, expd, qg, kg, lng, lnb, vb_dtype=F32)

    score_bound = (np.float32(1.02 * math.sqrt(DH_A) * LOG2E)
                   * jnp.max(jnp.abs(q_norm_g[0])) * jnp.max(jnp.abs(k_norm_g[0]))
                   + np.float32(LOG2E) * jnp.max(jnp.abs(rel_bias - rel_bias[N_BUCKETS - 1])))
    n_rows = bias_past.shape[0]
    bias_past_halves = jnp.transpose(bias_past.reshape(n_rows, 2, past // 2), (1, 0, 2))
    attn_args = (page_table, qp.reshape(b, l, wa), kbp.reshape(b, l // ATTN_TILE, ATTN_TILE, wa), vtp, lam[0],
                 bias_tiles, subg_cols, qs.reshape(db, s_len, wa), ks.reshape(db, s_len, wa),
                 vs.reshape(db, s_len, wa), ck_t, cv_rows, wq_mask, bias_past_halves, bias_new, subg_rows)
    at_p, at_s = lax.cond(score_bound <= SAFE_SCORE_BOUND,
                          lambda args: _attention(*args, lam_init=lam_init, stable=False),
                          lambda args: _attention(*args, lam_init=lam_init, stable=True),
                          attn_args)

    yp0 = _mix0(xp, at_p.reshape(b * l, wa), gap, ubp, vbp, wsp_p, bsp_p, wo0)
    ys0 = _mix0(xs, at_s.reshape(db * s_len, wa), gas, ubs, vbs, wsp_s, bsp_s, wo0)

    q_scale = np.float32(dk ** -0.5)
    q1p, k1p, v1p, gsp, lap, lamin_p = _proj1(yp0, ng1, w1_bf, wg, bg, hk=hk, wc=wc, q_scale=q_scale)
    q1s, k1s, v1s, gss, las, lamin_s = _proj1(ys0, ng1, w1_bf, wg, bg, hk=hk, wc=wc, q_scale=q_scale)

    gla_p_args = (q1p.reshape(b, l, hk), k1p.reshape(b, l, hk), v1p.reshape(b, l, wc),
                  lap.reshape(b, l, hk), og, jnp.zeros((b, H_C, dk, dv), F32))
    o_p, st_p = lax.cond(-jnp.min(lamin_p) * min(GLA_CHUNK_P, l) <= SAFE_DECAY_BOUND,
                         lambda a: _gla_prompt(*a, bounded=True),
                         lambda a: _gla_prompt(*a, bounded=False), gla_p_args)
    gla_s_args = (q1s, k1s, v1s, las, og, state_gla[0])
    o_s, st_s = lax.cond(-jnp.min(lamin_s) * s_len <= SAFE_DECAY_BOUND,
                         lambda a: _gla_sample(*a, seg=s_len, bounded=True),
                         lambda a: _gla_sample(*a, seg=s_len, bounded=False), gla_s_args)

    yp1 = _out1(yp0, o_p.reshape(b * l, wc), gsp, wo1)
    ys1 = _out1(ys0, o_s, gss, wo1)

    return (
        yp1.reshape(b, l, d),
        ys1.reshape(db, s_len, d),
        jnp.transpose(ktp.reshape(b, H_A, 2, DH_A, l), (0, 4, 1, 2, 3))[None],
        vp.reshape(1, b, l, H_A, DV_A),
        jnp.transpose(_seq_minor(ks, db, s_len).reshape(s_len, H_A, 2, DH_A, db), (4, 0, 1, 2, 3))[None],
        vs.reshape(1, db, s_len, H_A, DV_A),
        vbs.reshape(1, db, s_len, -1),
        st_p.reshape(1, b, H_C, dk, dv),
        st_s.reshape(1, db, H_C, dk, dv),
    )
```

```python
import functools
import math

import numpy as np
import jax
import jax.numpy as jnp
from jax import lax
from jax.experimental import pallas as pl
from jax.experimental.pallas import tpu as pltpu

F32 = jnp.float32
BF16 = jnp.bfloat16

H_A = 8
DH_A = 64
DV_A = 2 * DH_A
G_B = 8
CHUNK_B = 128
H_C = 4
GLA_TAU = 16.0
N_BUCKETS = 32
MAX_DISTANCE = 128
EPS = 1e-6
MASK_VALUE = -1e30
LOG2E = 1.4426950408889634
SAFE_SCORE_BOUND = 60.0
SAFE_DECAY_BOUND = 60.0

LANES = 128
SUBLANES = 8
VMEM_LIMIT_BYTES = 56 * 1024 * 1024

TOKEN_TILE = 256
STREAM_TILE = 512
ATTN_TILE = TOKEN_TILE
GLA_CHUNK_P = 128
GLA_GROUP_S = 8


def _cparams(n_axes):
    return pltpu.CompilerParams(
        dimension_semantics=("arbitrary",) * n_axes,
        vmem_limit_bytes=VMEM_LIMIT_BYTES,
    )


def _row_tile(t, preferred):
    tm = min(preferred, t)
    while t % tm:
        tm -= CHUNK_B
    assert tm > 0 and t % tm == 0
    return tm


def _dot(a, b):
    return jnp.dot(a, b, preferred_element_type=F32)


def _dot_nt(a, b):
    return lax.dot_general(a, b, (((1,), (1,)), ((), ())), preferred_element_type=F32)


def _dot_tn(a, b):
    return lax.dot_general(a, b, (((0,), (0,)), ((), ())), preferred_element_type=F32)


def _silu(x):
    return x * jax.nn.sigmoid(x)


def _gelu(x):
    return 0.5 * x * (1.0 + lax.erf(x * np.float32(math.sqrt(0.5))))


def _rms_rows(x, g):
    return x * lax.rsqrt(jnp.mean(x * x, axis=-1, keepdims=True) + EPS) * g


def _split_bf16(x):
    hi = x.astype(BF16)
    lo = (x - hi.astype(F32)).astype(BF16)
    return hi, lo


def _proj0_body(x_ref, ng_ref, w_ref, red_ref, exp_ref, qg_ref, kg_ref, lng_ref, lnb_ref, *out_refs,
                width, prompt):
    if prompt:
        q_ref, kb_ref, kt_ref, v_ref, vt_ref, ga_ref, ub_ref, vb_ref = out_refs
    else:
        q_ref, k_ref, v_ref, ga_ref, ub_ref, vb_ref = out_refs
    xb = _rms_rows(x_ref[...], ng_ref[...]).astype(BF16)

    def seg(i):
        return _dot(xb, w_ref[:, i * width:(i + 1) * width])

    def group_rms(z, g):
        ms = _dot((z * z).astype(BF16), red_ref[...])
        hi, lo = _split_bf16(lax.rsqrt(ms + EPS))
        scale = _dot(hi, exp_ref[...]) + _dot(lo, exp_ref[...])
        return z * scale * g

    q_ref[...] = group_rms(seg(0), qg_ref[...]).astype(q_ref.dtype)
    k = group_rms(seg(1), kg_ref[...])
    v = seg(2)
    v_ref[...] = v
    if prompt:
        kb_ref[...] = k.astype(kb_ref.dtype)
        kt_ref[...] = k.T
        vt_ref[...] = v.T.astype(vt_ref.dtype)
    else:
        k_ref[...] = k
    ga_ref[...] = _silu(seg(3)).astype(ga_ref.dtype)
    ub_ref[...] = (_gelu(seg(4)) * _silu(seg(6))).astype(ub_ref.dtype)
    hv = _gelu(seg(5))
    hc = hv - jnp.mean(hv, axis=-1, keepdims=True)
    vb = hc * lax.rsqrt(jnp.mean(hc * hc, axis=-1, keepdims=True) + EPS)
    vb_ref[...] = (vb * lng_ref[...] + lnb_ref[...]).astype(vb_ref.dtype)


def _proj0(x2d, ng, w_bf, red, expd, qg, kg, lng, lnb, *, vb_dtype, prompt_batch=None):
    t, d = x2d.shape
    n_in = w_bf.shape[1]
    width = n_in // 7
    tm = _row_tile(t, TOKEN_TILE)
    row = lambda i: (i, 0)
    full = lambda i: (0, 0)
    tok = lambda dt: jax.ShapeDtypeStruct((t, width), dt)
    row_spec = pl.BlockSpec((tm, width), row)
    prompt = prompt_batch is not None
    if prompt:
        per = t // prompt_batch // tm
        outs = [tok(BF16), tok(BF16), jax.ShapeDtypeStruct((prompt_batch, width, per * tm), F32), tok(F32),
                jax.ShapeDtypeStruct((prompt_batch, per, width, tm), BF16), tok(BF16), tok(BF16), tok(vb_dtype)]
        out_specs = [row_spec, row_spec,
                     pl.BlockSpec((None, width, tm), lambda i: (i // per, 0, i % per)), row_spec,
                     pl.BlockSpec((None, None, width, tm), lambda i: (i // per, i % per, 0, 0)),
                     row_spec, row_spec, row_spec]
    else:
        outs = [tok(BF16), tok(F32), tok(F32), tok(BF16), tok(BF16), tok(vb_dtype)]
        out_specs = [row_spec] * 6
    return pl.pallas_call(
        functools.partial(_proj0_body, width=width, prompt=prompt),
        grid=(t // tm,),
        in_specs=[
            pl.BlockSpec((tm, d), row),
            pl.BlockSpec((1, d), full),
            pl.BlockSpec((d, n_in), full),
            pl.BlockSpec(red.shape, full),
            pl.BlockSpec(expd.shape, full),
            pl.BlockSpec((1, width), full),
            pl.BlockSpec((1, width), full),
            pl.BlockSpec((1, width), full),
            pl.BlockSpec((1, width), full),
        ],
        out_specs=out_specs,
        out_shape=outs,
        compiler_params=_cparams(1),
        name="proj0",
    )(x2d, ng, w_bf, red, expd, qg, kg, lng, lnb)


def _lambda_value(lam_ref, lam_init):
    lf = lam_ref[...]
    s1 = jnp.sum(lf[0:1] * lf[1:2], axis=-1, keepdims=True)
    s2 = jnp.sum(lf[2:3] * lf[3:4], axis=-1, keepdims=True)
    return jnp.exp(s1) - jnp.exp(s2) + lam_init


def _prompt_attention_step(qi, lam, q_ref, k_ref, vt_ref, bias_ref, subg_ref, o_ref, sa_sc, sb_sc, m_sc, l_sc,
                           acc_sc, *, stable):
    hd = 2 * DH_A
    n_heads = q_ref.shape[1] // hd
    tq = q_ref.shape[0]
    q = q_ref[...]
    lane = lax.broadcasted_iota(jnp.int32, (tq, hd), 1)
    q2t = []
    for hh in range(n_heads):
        qh = q[:, hh * hd:(hh + 1) * hd].astype(F32)
        q2 = jnp.concatenate([jnp.where(lane < DH_A, qh, 0.0), jnp.where(lane >= DH_A, qh, 0.0)], axis=0)
        q2t.append(q2.T.astype(BF16))

    def sublane_partial(x):
        return x.reshape(x.shape[0] // SUBLANES, SUBLANES, x.shape[1]).sum(axis=0)

    last_tile = 2 * qi + 1

    def scores(t, sbuf):
        j = last_tile - t
        for hh in range(n_heads):
            sbuf[hh] = _dot(k_ref[j, :, hh * hd:(hh + 1) * hd], q2t[hh])

    def consume(t, sbuf):
        j = last_tile - t
        kind = jnp.minimum(t, 3)
        ps = []
        for hh in range(n_heads):
            bt = bias_ref[hh, kind]
            s = sbuf[hh] + jnp.concatenate([bt, bt], axis=1)
            if stable:
                m = m_sc[hh]
                m_new = jnp.maximum(m, jnp.max(s, axis=0, keepdims=True))
                alpha = jnp.exp2(m - m_new)
                m_sc[hh] = m_new
                p = jnp.exp2(s - m_new)
                l_sc[hh] = alpha * l_sc[hh] + sublane_partial(p)
                acc_sc[hh] = alpha * acc_sc[hh]
            else:
                p = jnp.exp2(s)
                l_sc[hh] = l_sc[hh] + sublane_partial(p)
            ps.append(p.astype(BF16))
        for hh in range(n_heads):
            acc_sc[hh] = acc_sc[hh] + _dot(vt_ref[j, hh * DV_A:(hh + 1) * DV_A, :], ps[hh])

    m_sc[...] = jnp.full(m_sc.shape, MASK_VALUE, F32)
    l_sc[...] = jnp.zeros(l_sc.shape, F32)
    acc_sc[...] = jnp.zeros(acc_sc.shape, F32)
    n_keys = 2 * qi + 2
    scores(0, sa_sc)

    def body(pair, c):
        t = 2 * pair
        scores(t + 1, sb_sc)
        consume(t, sa_sc)
        scores(t + 2, sa_sc)
        consume(t + 1, sb_sc)
        return c

    lax.fori_loop(0, n_keys // 2 - 1, body, 0)
    scores(n_keys - 1, sb_sc)
    consume(n_keys - 2, sa_sc)
    consume(n_keys - 1, sb_sc)

    outs = []
    for hh in range(n_heads):
        l8, acc = l_sc[hh], acc_sc[hh]
        o_all = acc * (1.0 / jnp.sum(l8, axis=0, keepdims=True))
        o = o_all[:, :tq] - lam * o_all[:, tq:]
        on = o * lax.rsqrt(jnp.mean(o * o, axis=0, keepdims=True) + EPS) * subg_ref[...]
        outs.append(on.T)
    o_ref[...] = jnp.concatenate(outs, axis=1).astype(o_ref.dtype)


def _sample_attention_item(half, slot, sb, lam, qs_ref, kn_ref, vn_ref, mask_ref, bp_ref, bn_ref, subg_ref, os_ref,
                           kbuf, vbuf, sm_sc, sl_sc, sacc_sc, *, key_chunk):
    q = qs_ref[sb]
    s_len = q.shape[0]
    n_rows = mask_ref.shape[0]
    half_rows = H_A * s_len
    wq = jnp.concatenate([q] * (n_rows // s_len), axis=0) * mask_ref[...]
    first = half == 0
    n_chunks = kbuf.shape[2] // key_chunk

    pad_rows = bn_ref.shape[1] - s_len
    kn = jnp.concatenate([kn_ref[sb], jnp.zeros((pad_rows, kn_ref.shape[2]), F32)], axis=0).astype(BF16)
    vn = jnp.concatenate([vn_ref[sb], jnp.zeros((pad_rows, vn_ref.shape[2]), F32)], axis=0).astype(BF16)
    s_parts = [_dot_nt(wq, kn) + bn_ref[...] + jnp.where(first, 0.0, MASK_VALUE).astype(F32)]
    for c in range(n_chunks):
        keys = pl.ds(c * key_chunk, key_chunk)
        s_parts.append(_dot(wq, kbuf[slot, :, keys].astype(BF16)) + bp_ref[half, :, keys])

    m_old = jnp.where(first, MASK_VALUE, sm_sc[sb])
    m = m_old
    for s in s_parts:
        m = jnp.maximum(m, jnp.max(s, axis=1, keepdims=True))
    alpha = jnp.exp2(m_old - m)
    p_parts = [jnp.exp2(s - m) for s in s_parts]
    l = alpha * jnp.where(first, 0.0, sl_sc[sb])
    for p in p_parts:
        l = l + jnp.sum(p, axis=1, keepdims=True)
    sm_sc[sb] = m
    sl_sc[sb] = l
    row = lax.broadcasted_iota(jnp.int32, l.shape, 0)
    scale = jnp.where(row < half_rows, 1.0, -lam) / l

    def head_rows(x, h):
        r0 = h * s_len
        return jnp.concatenate([x[r0:r0 + s_len], x[half_rows + r0:half_rows + r0 + s_len]], axis=0)

    vhs = [vbuf[slot, h].astype(BF16) for h in range(H_A)]
    p_past = jnp.concatenate(p_parts[1:], axis=1) if n_chunks > 1 else p_parts[1]
    pvs = []
    for h in range(H_A):
        pv = _dot(head_rows(p_past, h).astype(BF16), vhs[h])
        pvs.append(pv + _dot(head_rows(p_parts[0], h).astype(BF16), vn[:, h * DV_A:(h + 1) * DV_A]))
    heads = []
    for h in range(H_A):
        acc = head_rows(alpha, h) * jnp.where(first, 0.0, sacc_sc[sb, h]) + pvs[h]
        sacc_sc[sb, h] = acc
        w = acc * head_rows(scale, h)
        o_h = w[:s_len] + w[s_len:]
        heads.append(o_h * lax.rsqrt(jnp.mean(o_h * o_h, axis=-1, keepdims=True) + EPS))
    os_ref[sb] = (jnp.concatenate(heads, axis=1) * subg_ref[...]).astype(os_ref.dtype)


def _attention_body(pt_ref, lam_ref, q_ref, k_ref, vt_ref, bias_ref, subg_ref,
                    qs_ref, kn_ref, vn_ref, mask_ref, bp_ref, bn_ref, subgr_ref, ck_hbm, cv_hbm,
                    o_ref, os_ref,
                    sa_sc, sb_sc, m_sc, l_sc, acc_sc, kbuf, vbuf, sem, sm_sc, sl_sc, sacc_sc,
                    *, lam_init, stable, items_per_step, pages_per_item, page, key_chunk):
    bi, hp, qi = pl.program_id(0), pl.program_id(1), pl.program_id(2)
    step = (bi * pl.num_programs(1) + hp) * pl.num_programs(2) + qi
    n_steps = pl.num_programs(0) * pl.num_programs(1) * pl.num_programs(2)
    ipg = items_per_step

    def item_copies(item, slot):
        seq, half = item // 2, item % 2
        cps = []
        for p in range(pages_per_item):
            pg = pt_ref[seq, half * pages_per_item + p]
            cps.append(pltpu.make_async_copy(ck_hbm.at[pg], kbuf.at[slot, :, pl.ds(p * page, page)],
                                             sem.at[slot, 0]))
            for h in range(H_A):
                cps.append(pltpu.make_async_copy(cv_hbm.at[pg, :, h, :],
                                                 vbuf.at[slot, h, pl.ds(p * page, page), :], sem.at[slot, 1]))
        return cps

    @pl.when(step == 0)
    def _():
        for slot in range(2):
            for c in item_copies(slot, slot):
                c.start()
        sm_sc[...] = jnp.zeros(sm_sc.shape, F32)
        sl_sc[...] = jnp.zeros(sl_sc.shape, F32)
        sacc_sc[...] = jnp.zeros(sacc_sc.shape, F32)

    lam = _lambda_value(lam_ref, lam_init)
    for r in range(ipg):
        item = step * ipg + r
        slot = r % 2
        for c in item_copies(item, slot):
            c.wait()
        _sample_attention_item(jnp.int32(r % 2), slot, r // 2, lam, qs_ref, kn_ref, vn_ref, mask_ref, bp_ref, bn_ref,
                               subgr_ref, os_ref, kbuf, vbuf, sm_sc, sl_sc, sacc_sc, key_chunk=key_chunk)

        @pl.when(item + 2 < n_steps * ipg)
        def _():
            for c in item_copies(item + 2, slot):
                c.start()

    _prompt_attention_step(qi, lam, q_ref, k_ref, vt_ref, bias_ref, subg_ref, o_ref, sa_sc, sb_sc, m_sc, l_sc,
                           acc_sc, stable=stable)


def _attention(page_table, q, k, vt, lam_p, bias_tiles, subg_cols, qs, k_new, v_new, cache_k, cache_v, wq_mask,
               bias_past, bias_new, subg_rows, *, lam_init, stable):
    b, l, w = q.shape
    db, s_len, _ = qs.shape
    tile = ATTN_TILE
    tq = 2 * tile
    n_tiles, n_q = l // tile, l // tq
    hp = 2
    wb = hp * 2 * DH_A
    nhp = H_A // hp
    n_steps = b * nhp * n_q
    n_pages, page = page_table.shape[1], cache_k.shape[2]
    pages_per_item = n_pages // 2
    ipg = (2 * db) // n_steps
    assert l % tq == 0 and n_pages % 2 == 0 and ipg * n_steps == 2 * db and ipg % 2 == 0
    spb = ipg // 2
    hk = pages_per_item * page
    key_chunk = min(512, hk)
    n_rows = wq_mask.shape[0]

    def sample_blk(bi, h, i, pt):
        return ((bi * nhp + h) * n_q + i, 0, 0)

    full2 = lambda bi, h, i, pt: (0, 0)
    grid_spec = pltpu.PrefetchScalarGridSpec(
        num_scalar_prefetch=1,
        grid=(b, nhp, n_q),
        in_specs=[
            pl.BlockSpec(lam_p.shape, full2),
            pl.BlockSpec((None, tq, wb), lambda bi, h, i, pt: (bi, i, h)),
            pl.BlockSpec((None, n_tiles, tile, wb), lambda bi, h, i, pt: (bi, 0, 0, h)),
            pl.BlockSpec((None, n_tiles, wb, tile), lambda bi, h, i, pt: (bi, 0, h, 0)),
            pl.BlockSpec((hp, 4, tile, tq), lambda bi, h, i, pt: (h, 0, 0, 0)),
            pl.BlockSpec((DV_A, tq), full2),
            pl.BlockSpec((spb, s_len, w), sample_blk),
            pl.BlockSpec((spb, s_len, w), sample_blk),
            pl.BlockSpec((spb, s_len, w), sample_blk),
            pl.BlockSpec(wq_mask.shape, full2),
            pl.BlockSpec(bias_past.shape, lambda bi, h, i, pt: (0, 0, 0)),
            pl.BlockSpec(bias_new.shape, full2),
            pl.BlockSpec(subg_rows.shape, full2),
            pl.BlockSpec(memory_space=pl.ANY),
            pl.BlockSpec(memory_space=pl.ANY),
        ],
        out_specs=[pl.BlockSpec((None, tq, wb), lambda bi, h, i, pt: (bi, i, h)),
                   pl.BlockSpec((spb, s_len, w), sample_blk)],
        scratch_shapes=[
            pltpu.VMEM((hp, tile, 2 * tq), F32),
            pltpu.VMEM((hp, tile, 2 * tq), F32),
            pltpu.VMEM((hp, 1, 2 * tq), F32),
            pltpu.VMEM((hp, SUBLANES, 2 * tq), F32),
            pltpu.VMEM((hp, DV_A, 2 * tq), F32),
            pltpu.VMEM((2, w, hk), F32),
            pltpu.VMEM((2, H_A, hk, DV_A), F32),
            pltpu.SemaphoreType.DMA((2, 2)),
            pltpu.VMEM((spb, n_rows, 1), F32),
            pltpu.VMEM((spb, n_rows, 1), F32),
            pltpu.VMEM((spb, H_A, 2 * s_len, DV_A), F32),
        ],
    )
    return pl.pallas_call(
        functools.partial(_attention_body, lam_init=lam_init, stable=stable, items_per_step=ipg,
                          pages_per_item=pages_per_item, page=page, key_chunk=key_chunk),
        grid_spec=grid_spec,
        out_shape=[jax.ShapeDtypeStruct((b, l, w), BF16), jax.ShapeDtypeStruct((db, s_len, w), BF16)],
        compiler_params=_cparams(3),
        name="attention_stable" if stable else "attention",
    )(page_table, lam_p, q, k, vt, bias_tiles, subg_cols, qs, k_new, v_new, wq_mask, bias_past, bias_new,
      subg_rows, cache_k, cache_v)


def _mix0_body(x_ref, at_ref, ga_ref, ub_ref, vb_ref, wsp_ref, bsp_ref, wo_ref, y_ref):
    tm = x_ref.shape[0]
    wa = at_ref.shape[1]
    vb = vb_ref[...].astype(BF16)
    n_sub = tm // CHUNK_B
    cg = vb.shape[1] // G_B
    cols = []
    for g in range(G_B):
        lanes = slice(g * cg, (g + 1) * cg)
        rows = [_dot(wsp_ref[g], vb[sb * CHUNK_B:(sb + 1) * CHUNK_B, lanes]) for sb in range(n_sub)]
        cols.append(jnp.concatenate(rows, axis=0) if n_sub > 1 else rows[0])
    bsp = bsp_ref[...]
    bias = jnp.concatenate([bsp] * n_sub, axis=0) if n_sub > 1 else bsp
    mixed = jnp.concatenate(cols, axis=1) + bias
    ob = (ub_ref[...].astype(F32) * mixed).astype(BF16)
    oa = at_ref[...] * ga_ref[...]
    y_ref[...] = x_ref[...] + _dot(oa, wo_ref[:wa, :]) + _dot(ob, wo_ref[wa:, :])


def _mix0(x2d, attn, gas, ub, vb, wsp, bsp, wo_bf):
    t, d = x2d.shape
    tm = _row_tile(t, STREAM_TILE)
    row = lambda i: (i, 0)
    full = lambda i: (0, 0)
    return pl.pallas_call(
        _mix0_body,
        grid=(t // tm,),
        in_specs=[
            pl.BlockSpec((tm, d), row),
            pl.BlockSpec((tm, attn.shape[1]), row),
            pl.BlockSpec((tm, gas.shape[1]), row),
            pl.BlockSpec((tm, ub.shape[1]), row),
            pl.BlockSpec((tm, vb.shape[1]), row),
            pl.BlockSpec(wsp.shape, lambda i: (0, 0, 0)),
            pl.BlockSpec(bsp.shape, full),
            pl.BlockSpec(wo_bf.shape, full),
        ],
        out_specs=pl.BlockSpec((tm, d), row),
        out_shape=jax.ShapeDtypeStruct((t, d), F32),
        compiler_params=_cparams(1),
        name="mix0",
    )(x2d, attn, gas, ub, vb, wsp, bsp, wo_bf)


def _proj1_body(x_ref, ng_ref, w_ref, wg_ref, bg_ref, q_ref, k_ref, v_ref, gs_ref, la_ref, lamin_ref,
                *, hk, wc, q_scale):
    xb = _rms_rows(x_ref[...], ng_ref[...]).astype(BF16)
    q_ref[...] = _dot(xb, w_ref[:, 0:hk]) * q_scale
    k_ref[...] = _dot(xb, w_ref[:, hk:2 * hk])
    v_ref[...] = _dot(xb, w_ref[:, 2 * hk:2 * hk + wc]).astype(v_ref.dtype)
    gs_ref[...] = _silu(_dot(xb, w_ref[:, 2 * hk + wc:2 * hk + 2 * wc])).astype(gs_ref.dtype)
    za = _dot(xb, w_ref[:, 2 * hk + 2 * wc:]).astype(BF16)
    xg = _dot(za, wg_ref[...]) + bg_ref[...]
    la = (jnp.minimum(xg, 0.0) - jnp.log1p(jnp.exp(-jnp.abs(xg)))) * np.float32(1.0 / GLA_TAU)
    la_ref[...] = la
    lamin_ref[...] = jnp.min(la, axis=0, keepdims=True)


def _proj1(x2d, ng, w_bf, wg_bf, bg, *, hk, wc, q_scale):
    t, d = x2d.shape
    tm = _row_tile(t, STREAM_TILE)
    row = lambda i: (i, 0)
    full = lambda i: (0, 0)
    outs = [
        jax.ShapeDtypeStruct((t, hk), F32),
        jax.ShapeDtypeStruct((t, hk), F32),
        jax.ShapeDtypeStruct((t, wc), BF16),
        jax.ShapeDtypeStruct((t, wc), BF16),
        jax.ShapeDtypeStruct((t, hk), F32),
        jax.ShapeDtypeStruct((t // tm, 1, hk), F32),
    ]
    return pl.pallas_call(
        functools.partial(_proj1_body, hk=hk, wc=wc, q_scale=q_scale),
        grid=(t // tm,),
        in_specs=[
            pl.BlockSpec((tm, d), row),
            pl.BlockSpec((1, d), full),
            pl.BlockSpec(w_bf.shape, full),
            pl.BlockSpec(wg_bf.shape, full),
            pl.BlockSpec((1, hk), full),
        ],
        out_specs=[pl.BlockSpec((tm, hk), row), pl.BlockSpec((tm, hk), row), pl.BlockSpec((tm, wc), row),
                   pl.BlockSpec((tm, wc), row), pl.BlockSpec((tm, hk), row),
                   pl.BlockSpec((None, 1, hk), lambda i: (i, 0, 0))],
        out_shape=outs,
        compiler_params=_cparams(1),
        name="proj1",
    )(x2d, ng, w_bf, wg_bf, bg)


def _segment_cumsum(x, seg):
    n = x.shape[0]
    row = lax.broadcasted_iota(jnp.int32, x.shape, 0) % seg
    d = 1
    while d < seg:
        x = x + jnp.where(row >= d, pltpu.roll(x, d, axis=0), 0.0)
        d *= 2
    return x


def _block_reference(bc, hb):
    n, w = bc.shape
    if hb >= SUBLANES:
        pieces = []
        for blk in range(n // (2 * hb)):
            r = blk * 2 * hb + hb - 1
            pieces.append(jnp.broadcast_to(bc[r:r + 1, :], (2 * hb, w)))
        return jnp.concatenate(pieces, axis=0) if len(pieces) > 1 else pieces[0]
    pos = lax.broadcasted_iota(jnp.int32, bc.shape, 0) % (2 * hb)
    out = bc
    for delta in range(-hb, hb):
        if delta == 0:
            continue
        shifted = pltpu.roll(bc, (-delta) % n, axis=0)
        out = jnp.where(pos == hb - 1 - delta, shifted, out)
    return out


def _gla_intra_scores(q, k, bc, seg, bounded):
    n = q.shape[0]
    ri = lax.broadcasted_iota(jnp.int32, (n, n), 0)
    ci = lax.broadcasted_iota(jnp.int32, (n, n), 1)
    if bounded:
        att = _dot_nt((q * jnp.exp(bc)).astype(BF16), (k * jnp.exp(-bc)).astype(BF16))
        att = jnp.where(ri >= ci, att, 0.0)
        if seg < n:
            att = jnp.where((ri // seg) == (ci // seg), att, 0.0)
        return att
    rowpos = lax.broadcasted_iota(jnp.int32, q.shape, 0)
    att = jnp.where(ri == ci, _dot_nt(q.astype(BF16), k.astype(BF16)), 0.0)
    for hb in _levels(seg):
        ref = _block_reference(bc, hb)
        is_q = (rowpos % (2 * hb)) >= hb
        f = jnp.exp(jnp.where(is_q, bc - ref, ref - bc))
        qt = jnp.where(is_q, q * f, 0.0).astype(BF16)
        kt = jnp.where(is_q, 0.0, k * f).astype(BF16)
        same = (ri // (2 * hb)) == (ci // (2 * hb))
        att = att + jnp.where(same, _dot_nt(qt, kt), 0.0)
    return att


def _levels(seg):
    out, hb = [], seg // 2
    while hb >= 1:
        out.append(hb)
        hb //= 2
    return out


def _rows_scale_matrix(row, n_cols):
    hi = row.astype(BF16).astype(F32)
    r1 = row - hi
    mid = r1.astype(BF16).astype(F32)
    lo = (r1 - mid).astype(BF16).astype(F32)
    pad = jnp.zeros((2 * SUBLANES - 3, row.shape[1]), F32)
    parts = jnp.concatenate([hi, mid, lo, pad], axis=0).astype(BF16)
    return _dot_tn(parts, jnp.ones((2 * SUBLANES, n_cols), BF16))


def _gla_prompt_body(q_ref, k_ref, v_ref, la_ref, og_ref, s0_ref, o_ref, sout_ref, st_sc, *, dk, dv,
                     bounded):
    ci = pl.program_id(1)
    n_seq, chunk = q_ref.shape[0], q_ref.shape[1]

    @pl.when(ci == 0)
    def _():
        st_sc[...] = s0_ref[...]

    units = [(s, h, slice(h * dk, (h + 1) * dk), slice(h * dv, (h + 1) * dv))
             for s in range(n_seq) for h in range(H_C)]
    bcs = [_segment_cumsum(la_ref[s, :, kl], chunk) for s, _, kl, _ in units]
    atts = [_gla_intra_scores(q_ref[s, :, kl], k_ref[s, :, kl], bc, chunk, bounded)
            for (s, _, kl, _), bc in zip(units, bcs)]
    outs, upds, decays = [], [], []
    for (s, h, kl, vl), bc, att in zip(units, bcs, atts):
        q, k, v = q_ref[s, :, kl], k_ref[s, :, kl], v_ref[s, :, vl]
        b_last = bc[chunk - 1:chunk, :]
        k_dec = (k * jnp.exp(b_last - bc)).astype(BF16)
        outs.append(_dot((q * jnp.exp(bc)).astype(BF16), st_sc[s, h].astype(BF16)) + _dot(att.astype(BF16), v))
        upds.append(_dot_tn(k_dec, v))
        decays.append(_rows_scale_matrix(jnp.exp(b_last), dv))
    for (s, h, _, vl), out, upd, decay in zip(units, outs, upds, decays):
        st_sc[s, h] = st_sc[s, h] * decay + upd
        o_ref[s, :, vl] = _rms_rows(out, og_ref[...]).astype(o_ref.dtype)

    @pl.when(ci == pl.num_programs(1) - 1)
    def _():
        sout_ref[...] = st_sc[...]


def _gla_prompt(q, k, v, la, og, s0, *, bounded):
    b, l, hk = q.shape
    wc = v.shape[2]
    dk, dv = hk // H_C, wc // H_C
    chunk = min(GLA_CHUNK_P, l)
    nb = 4 if b % 4 == 0 else (2 if b % 2 == 0 else 1)
    tok = lambda bi, ci: (bi, ci, 0)
    st = lambda bi, ci: (bi, 0, 0, 0)
    return pl.pallas_call(
        functools.partial(_gla_prompt_body, dk=dk, dv=dv, bounded=bounded),
        grid=(b // nb, l // chunk),
        in_specs=[
            pl.BlockSpec((nb, chunk, hk), tok),
            pl.BlockSpec((nb, chunk, hk), tok),
            pl.BlockSpec((nb, chunk, wc), tok),
            pl.BlockSpec((nb, chunk, hk), tok),
            pl.BlockSpec((1, dv), lambda bi, ci: (0, 0)),
            pl.BlockSpec((nb, H_C, dk, dv), st),
        ],
        out_specs=[pl.BlockSpec((nb, chunk, wc), tok), pl.BlockSpec((nb, H_C, dk, dv), st)],
        out_shape=[jax.ShapeDtypeStruct((b, l, wc), BF16), jax.ShapeDtypeStruct((b, H_C, dk, dv), F32)],
        scratch_shapes=[pltpu.VMEM((nb, H_C, dk, dv), F32)],
        compiler_params=_cparams(2),
        name="gla_prompt" if bounded else "gla_prompt_any_decay",
    )(q, k, v, la, og, s0)


def _gla_sample_body(q_ref, k_ref, v_ref, la_ref, og_ref, s0_ref, o_ref, sout_ref, *, dk, dv, seg, bounded):
    n_seq = s0_ref.shape[0]
    for h in range(H_C):
        kl = slice(h * dk, (h + 1) * dk)
        vl = slice(h * dv, (h + 1) * dv)
        q, k, la = q_ref[:, kl], k_ref[:, kl], la_ref[:, kl]
        v = v_ref[:, vl]
        bc = _segment_cumsum(la, seg)
        att = _gla_intra_scores(q, k, bc, seg, bounded)
        o_intra = _dot(att.astype(BF16), v)
        q_dec = (q * jnp.exp(bc)).astype(BF16)
        outs = []
        for s in range(n_seq):
            rows = slice(s * seg, (s + 1) * seg)
            st = s0_ref[s, h]
            bl = bc[s * seg + seg - 1:s * seg + seg, :]
            k_dec = (k[rows] * jnp.exp(bl - bc[rows])).astype(BF16)
            outs.append(_dot(q_dec[rows], st.astype(BF16)))
            sout_ref[s, h] = st * _rows_scale_matrix(jnp.exp(bl), dv) + _dot_tn(k_dec, v[rows])
        o = jnp.concatenate(outs, axis=0) + o_intra
        o_ref[:, vl] = _rms_rows(o, og_ref[...]).astype(o_ref.dtype)


def _gla_sample(q, k, v, la, og, s0, *, seg, bounded):
    t, hk = q.shape
    wc = v.shape[1]
    dk, dv = hk // H_C, wc // H_C
    db = t // seg
    grp = min(GLA_GROUP_S, db)
    rows = grp * seg
    tok = lambda i: (i, 0)
    st = lambda i: (i, 0, 0, 0)
    return pl.pallas_call(
        functools.partial(_gla_sample_body, dk=dk, dv=dv, seg=seg, bounded=bounded),
        grid=(db // grp,),
        in_specs=[
            pl.BlockSpec((rows, hk), tok),
            pl.BlockSpec((rows, hk), tok),
            pl.BlockSpec((rows, wc), tok),
            pl.BlockSpec((rows, hk), tok),
            pl.BlockSpec((1, dv), lambda i: (0, 0)),
            pl.BlockSpec((grp, H_C, dk, dv), st),
        ],
        out_specs=[pl.BlockSpec((rows, wc), tok), pl.BlockSpec((grp, H_C, dk, dv), st)],
        out_shape=[jax.ShapeDtypeStruct((t, wc), BF16), jax.ShapeDtypeStruct((db, H_C, dk, dv), F32)],
        compiler_params=_cparams(1),
        name="gla_sample" if bounded else "gla_sample_any_decay",
    )(q, k, v, la, og, s0)


def _out1_body(x_ref, o_ref, gs_ref, w_ref, y_ref):
    y_ref[...] = x_ref[...] + _dot(o_ref[...] * gs_ref[...], w_ref[...])


def _out1(x2d, o, gs, w_bf):
    t, d = x2d.shape
    tm = _row_tile(t, STREAM_TILE)
    row = lambda i: (i, 0)
    return pl.pallas_call(
        _out1_body,
        grid=(t // tm,),
        in_specs=[pl.BlockSpec((tm, d), row), pl.BlockSpec((tm, o.shape[1]), row),
                  pl.BlockSpec((tm, gs.shape[1]), row), pl.BlockSpec(w_bf.shape, lambda i: (0, 0))],
        out_specs=pl.BlockSpec((tm, d), row),
        out_shape=jax.ShapeDtypeStruct((t, d), F32),
        compiler_params=_cparams(1),
        name="out1",
    )(x2d, o, gs, w_bf)


def _seq_minor_body(x_hbm, o_ref, buf, sem):
    s_len = buf.shape[0]
    copies = [pltpu.make_async_copy(x_hbm.at[:, t, :], buf.at[t], sem.at[t]) for t in range(s_len)]
    for c in copies:
        c.start()
    for t, c in enumerate(copies):
        c.wait()
        o_ref[t] = buf[t].T


def _seq_minor(x2d, db, s_len):
    w = x2d.shape[1]
    return pl.pallas_call(
        _seq_minor_body,
        grid=(1,),
        in_specs=[pl.BlockSpec(memory_space=pl.ANY)],
        out_specs=pl.BlockSpec((s_len, w, db), lambda i: (0, 0, 0)),
        out_shape=jax.ShapeDtypeStruct((s_len, w, db), x2d.dtype),
        scratch_shapes=[pltpu.VMEM((s_len, db, w), x2d.dtype), pltpu.SemaphoreType.DMA((s_len,))],
        compiler_params=_cparams(1),
        name="seq_minor",
    )(x2d.reshape(db, s_len, w))


def _t5_bucket_np(dist):
    n = np.maximum(dist, 0)
    max_exact = N_BUCKETS // 2
    nf = np.maximum(n, 1).astype(np.float32)
    ratio = np.log(nf / np.float32(max_exact)) / np.float32(math.log(MAX_DISTANCE / max_exact))
    large = max_exact + (ratio * np.float32(N_BUCKETS - max_exact)).astype(np.int32)
    large = np.minimum(large, N_BUCKETS - 1)
    return np.where(n < max_exact, n, large)


def _bias_by_distance(rel_bias, n):
    buckets = _t5_bucket_np(np.arange(n))
    assert np.all(np.diff(buckets) >= 0)
    cuts = [0] + [int(i) + 1 for i in np.nonzero(np.diff(buckets))[0]] + [n]
    pieces = [jnp.broadcast_to(rel_bias[int(buckets[a])][:, None], (rel_bias.shape[1], e - a))
              for a, e in zip(cuts[:-1], cuts[1:])]
    return jnp.concatenate(pieces, axis=1).astype(F32) * np.float32(LOG2E)


def _toeplitz(w, n):
    h = w.shape[0]
    flat = jnp.tile(w, (1, n))[:, :n * (2 * n - 1)]
    return flat.reshape(h, n, 2 * n - 1)[:, :, n - 1:]


def _prompt_bias_tiles(rel_bias, tile):
    assert int(_t5_bucket_np(np.array([tile + 1]))[0]) == N_BUCKETS - 1
    tab = _bias_by_distance(rel_bias, 2 * tile)
    tab = tab - tab[:, 2 * tile - 1:]
    h = tab.shape[0]
    pad = jnp.zeros((h, 1), F32)
    diag = _toeplitz(jnp.concatenate([jnp.full((h, tile - 1), MASK_VALUE, F32), tab[:, :tile], pad], axis=1), tile)
    near = _toeplitz(jnp.concatenate([tab[:, 1:], pad], axis=1), tile)
    masked = jnp.full((h, tile, tile), MASK_VALUE, F32)
    zero = jnp.zeros((h, tile, tile), F32)
    tiles = [jnp.concatenate(p, axis=-1) for p in ((masked, diag), (diag, near), (near, zero), (zero, zero))]
    return jnp.stack(tiles, axis=1)


def _sample_bias(rel_bias, past, s_len):
    n = past + s_len
    rev = _bias_by_distance(rel_bias, n)[:, ::-1]
    h = rev.shape[0]
    bp = jnp.stack([rev[:, s_len - 1 - q:s_len - 1 - q + past] for q in range(s_len)], axis=1)
    small = jnp.concatenate([rev[:, past:], jnp.full((h, LANES), MASK_VALUE, F32)], axis=1)
    bn = jnp.stack([small[:, s_len - 1 - q:s_len - 1 - q + LANES] for q in range(s_len)], axis=1)
    bp = bp.reshape(h * s_len, past)
    bn = bn.reshape(h * s_len, LANES)
    return jnp.concatenate([bp, bp], axis=0), jnp.concatenate([bn, bn], axis=0)


def _sample_query_mask(s_len):
    rows = np.arange(2 * H_A * s_len)
    c = rows // (H_A * s_len)
    h = (rows // s_len) % H_A
    lane_grp = np.arange(H_A * 2 * DH_A) // DH_A
    return jnp.asarray(lane_grp[None, :] == (h * 2 + c)[:, None], dtype=BF16)


def _spatial_weights(sp_w, sp_b, seg, cg):
    g, n, _ = sp_w.shape
    w = sp_w * jnp.tril(jnp.ones((n, n), sp_w.dtype))
    reps = n // seg
    if reps > 1:
        eye = jnp.eye(reps, dtype=sp_w.dtype)
        w = jnp.einsum("ab,gts->gatbs", eye, w[:, :seg, :seg]).reshape(g, n, n)
    bias = jnp.tile(sp_b[:, :seg], (1, reps))
    bias = jnp.repeat(jnp.transpose(bias), cg, axis=1)
    return w.astype(BF16), bias.astype(F32)


def kernel(x_prompt, x_sample, cache_k, cache_v, state_gla, page_table, rel_bias, norm0_g, w_in0, q_norm_g,
           k_norm_g, lam, subln_g, ln_v_g, ln_v_b, spatial_w, spatial_b, w_out0, norm1_g, w_in1, w_gate,
           b_gate, gla_norm_g, w_out1):
    b, l, d = x_prompt.shape
    db, s_len, _ = x_sample.shape
    n_pool, page = cache_k.shape[1], cache_k.shape[2]
    past = page_table.shape[1] * page
    wa = H_A * DV_A
    lam_init = 0.8 - 0.6 * math.exp(-0.3 * 0)
    rank = w_gate.shape[1]
    hk = w_gate.shape[2]
    wc = w_out1.shape[1]
    dk, dv = hk // H_C, wc // H_C

    w0 = w_in0[0].astype(BF16)
    wo0 = w_out0[0].astype(BF16)
    grp = np.arange(wa) // DH_A
    red = jnp.asarray((grp[:, None] == np.arange(LANES)[None, :]) / DH_A, dtype=BF16)
    expd = jnp.asarray(np.arange(LANES)[:, None] == grp[None, :], dtype=BF16)
    qg = (jnp.tile(q_norm_g[0].reshape(-1), H_A) * np.float32(DH_A ** -0.5 * LOG2E)).reshape(1, wa).astype(F32)
    kg = jnp.tile(k_norm_g[0].reshape(-1), H_A).reshape(1, wa).astype(F32)
    ng0 = norm0_g[0].reshape(1, d)
    lng = ln_v_g[0].reshape(1, -1)
    lnb = ln_v_b[0].reshape(1, -1)
    sub_scaled = subln_g[0] * np.float32(1.0 - lam_init)
    subg_cols = jnp.broadcast_to(sub_scaled[:, None], (DV_A, 2 * ATTN_TILE)).astype(F32)
    subg_rows = jnp.tile(sub_scaled, H_A).reshape(1, wa).astype(F32)
    bias_tiles = _prompt_bias_tiles(rel_bias, ATTN_TILE)
    bias_past, bias_new = _sample_bias(rel_bias, past, s_len)
    wq_mask = _sample_query_mask(s_len)
    cg = ln_v_g.shape[1] // G_B
    wsp_p, bsp_p = _spatial_weights(spatial_w[0], spatial_b[0], CHUNK_B, cg)
    wsp_s, bsp_s = _spatial_weights(spatial_w[0], spatial_b[0], s_len, cg)

    w1 = w_in1[0]
    w1_bf = jnp.pad(w1, ((0, 0), (0, LANES - rank))).astype(BF16)
    wg = jnp.pad(w_gate[0], ((0, LANES - rank), (0, 0))).astype(BF16)
    bg = b_gate[0].reshape(1, hk)
    ng1 = norm1_g[0].reshape(1, d)
    og = gla_norm_g[0].reshape(1, dv)
    wo1 = w_out1[0].astype(BF16)

    xp = x_prompt.reshape(b * l, d)
    xs = x_sample.reshape(db * s_len, d)
    ck_t = jnp.transpose(cache_k[0], (0, 2, 3, 4, 1)).reshape(n_pool, wa, page)
    cv_rows = cache_v[0]

    qp, kbp, ktp, vp, vtp, gap, ubp, vbp = _proj0(xp, ng0, w0, red, expd, qg, kg, lng, lnb, vb_dtype=BF16,
                                                  prompt_batch=b)
    qs, ks, vs, gas, ubs, vbs = _proj0(xs, ng0, w0, red, expd, qg, kg, lng, lnb, vb_dtype=F32)

    score_bound = (np.float32(1.02 * math.sqrt(DH_A) * LOG2E)
                   * jnp.max(jnp.abs(q_norm_g[0])) * jnp.max(jnp.abs(k_norm_g[0]))
                   + np.float32(LOG2E) * jnp.max(jnp.abs(rel_bias - rel_bias[N_BUCKETS - 1])))
    n_rows = bias_past.shape[0]
    bias_past_halves = jnp.transpose(bias_past.reshape(n_rows, 2, past // 2), (1, 0, 2))
    attn_args = (page_table, qp.reshape(b, l, wa), kbp.reshape(b, l // ATTN_TILE, ATTN_TILE, wa), vtp, lam[0],
                 bias_tiles, subg_cols, qs.reshape(db, s_len, wa), ks.reshape(db, s_len, wa),
                 vs.reshape(db, s_len, wa), ck_t, cv_rows, wq_mask, bias_past_halves, bias_new, subg_rows)
    at_p, at_s = lax.cond(score_bound <= SAFE_SCORE_BOUND,
                          lambda args: _attention(*args, lam_init=lam_init, stable=False),
                          lambda args: _attention(*args, lam_init=lam_init, stable=True),
                          attn_args)

    yp0 = _mix0(xp, at_p.reshape(b * l, wa), gap, ubp, vbp, wsp_p, bsp_p, wo0)
    ys0 = _mix0(xs, at_s.reshape(db * s_len, wa), gas, ubs, vbs, wsp_s, bsp_s, wo0)

    q_scale = np.float32(dk ** -0.5)
    q1p, k1p, v1p, gsp, lap, lamin_p = _proj1(yp0, ng1, w1_bf, wg, bg, hk=hk, wc=wc, q_scale=q_scale)
    q1s, k1s, v1s, gss, las, lamin_s = _proj1(ys0, ng1, w1_bf, wg, bg, hk=hk, wc=wc, q_scale=q_scale)

    gla_p_args = (q1p.reshape(b, l, hk), k1p.reshape(b, l, hk), v1p.reshape(b, l, wc),
                  lap.reshape(b, l, hk), og, jnp.zeros((b, H_C, dk, dv), F32))
    o_p, st_p = lax.cond(-jnp.min(lamin_p) * min(GLA_CHUNK_P, l) <= SAFE_DECAY_BOUND,
                         lambda a: _gla_prompt(*a, bounded=True),
                         lambda a: _gla_prompt(*a, bounded=False), gla_p_args)
    gla_s_args = (q1s, k1s, v1s, las, og, state_gla[0])
    o_s, st_s = lax.cond(-jnp.min(lamin_s) * s_len <= SAFE_DECAY_BOUND,
                         lambda a: _gla_sample(*a, seg=s_len, bounded=True),
                         lambda a: _gla_sample(*a, seg=s_len, bounded=False), gla_s_args)

    yp1 = _out1(yp0, o_p.reshape(b * l, wc), gsp, wo1)
    ys1 = _out1(ys0, o_s, gss, wo1)

    return (
        yp1.reshape(b, l, d),
        ys1.reshape(db, s_len, d),
        jnp.transpose(ktp.reshape(b, H_A, 2, DH_A, l), (0, 4, 1, 2, 3))[None],
        vp.reshape(1, b, l, H_A, DV_A),
        jnp.transpose(_seq_minor(ks, db, s_len).reshape(s_len, H_A, 2, DH_A, db), (4, 0, 1, 2, 3))[None],
        vs.reshape(1, db, s_len, H_A, DV_A),
        vbs.reshape(1, db, s_len, -1),
        st_p.reshape(1, b, H_C, dk, dv),
        st_s.reshape(1, db, H_C, dk, dv),
    )
```
